```python
import jax, jax.numpy as jnp
from jax import lax
import numpy as np

D_MODEL = 2048
BATCH = 2
SEQ = 4096
DEPTH = 4

N_META = 16
RET_HEADS = 8
RET_DK = 256
RET_DV = 512
RET_CHUNK = 128
RET_QK = RET_HEADS * RET_DK
RET_V = RET_HEADS * RET_DV
HG_HEADS = 16
HG_DK = 128
HG_DV = 128
HG_CHUNK = 16
HG_K = HG_HEADS * HG_DK
HG_V = HG_HEADS * HG_DV
D_FF = 5632
N_EXPERTS = 8
TOP_K = 2
D_FF_EXPERT = 2816
N_DENSE = (DEPTH + 1) // 2
N_MOE = DEPTH // 2
ROPE_BASE = 10000.0
LN_EPS = 1e-5
DN_ALPHA = (2 * DEPTH) ** 0.25
DN_BETA = (8 * DEPTH) ** -0.25
SPLITS = (RET_QK, RET_QK, RET_V, RET_V, HG_K, HG_K, HG_V, HG_V, D_MODEL, D_MODEL)
P_IN = 24576

kernel_name = 'retnet_hgrn2_gated_merge_deepnorm_moe'


def layer_norm(x, g, b):
    xf = x.astype(jnp.float32)
    mu = xf.mean(-1, keepdims=True)
    var = jnp.square(xf - mu).mean(-1, keepdims=True)
    return ((xf - mu) * lax.rsqrt(var + LN_EPS) * g + b).astype(x.dtype)


def rotary(x, pos):
    half = x.shape[-1] // 2
    inv = 1.0 / (ROPE_BASE ** jnp.linspace(0.0, 1.0, half, dtype=jnp.float32))
    ang = pos[:, None] * inv[None, :]
    cos = jnp.cos(ang)[None, :, None, :]
    sin = jnp.sin(ang)[None, :, None, :]
    x1, x2 = x[..., :half], x[..., half:]
    return jnp.concatenate([x1 * cos - x2 * sin, x1 * sin + x2 * cos], axis=-1)


def retention_branch(q, k, v):
    B, T = q.shape[:2]
    pad = RET_CHUNK - N_META
    padf = lambda a: jnp.pad(a, ((0, 0), (pad, 0), (0, 0), (0, 0)))
    q, k, v = padf(q) * (RET_DK ** -0.5), padf(k), padf(v)
    n = (T + pad) // RET_CHUNK
    to_chunks = lambda a: a.reshape(B, n, RET_CHUNK, RET_HEADS, a.shape[-1]).transpose(1, 0, 3, 2, 4)
    log_gamma = jnp.log1p(-jnp.exp2(-5.0 - jnp.arange(RET_HEADS, dtype=jnp.float32)))
    idx = jnp.arange(RET_CHUNK, dtype=jnp.float32)
    rel = idx[:, None] - idx[None, :]
    intra = jnp.where(rel >= 0, jnp.exp(jnp.maximum(rel, 0.0) * log_gamma[:, None, None]), 0.0)
    q_decay = jnp.exp((idx + 1.0) * log_gamma[:, None])[None, :, :, None]
    k_decay = jnp.exp((RET_CHUNK - 1.0 - idx) * log_gamma[:, None])[None, :, :, None]
    c_decay = jnp.exp(RET_CHUNK * log_gamma)[None, :, None, None]

    def step(R, blk):
        qc, kc, vc = blk
        s = jnp.einsum('bhtd,bhsd->bhts', qc, kc) * intra
        o = jnp.einsum('bhts,bhsv->bhtv', s, vc) + jnp.einsum('bhtd,bhdv->bhtv', qc * q_decay, R)
        R = R * c_decay + jnp.einsum('bhsd,bhsv->bhdv', kc * k_decay, vc)
        return R, o

    R0 = jnp.zeros((B, RET_HEADS, RET_DK, RET_DV), jnp.float32)
    _, o = lax.scan(step, R0, (to_chunks(q), to_chunks(k), to_chunks(v)))
    o = o.transpose(1, 0, 3, 2, 4).reshape(B, n * RET_CHUNK, RET_HEADS, RET_DV)
    return o[:, pad:]


def hgrn2_branch(q, k, v, log_f):
    B, T = q.shape[:2]
    n = T // HG_CHUNK
    to_chunks = lambda a: a.reshape(B, n, HG_CHUNK, HG_HEADS, a.shape[-1]).transpose(1, 0, 3, 2, 4)
    tri = jnp.tril(jnp.ones((HG_CHUNK, HG_CHUNK), dtype=bool))[:, :, None]

    def step(S, blk):
        qc, kc, vc, gc = blk
        b = jnp.cumsum(gc, axis=2)
        b_last = b[:, :, -1:, :]
        o = jnp.einsum('bhtd,bhdv->bhtv', qc * jnp.exp(b), S)
        decay = jnp.exp(jnp.where(tri, b[:, :, :, None, :] - b[:, :, None, :, :], -jnp.inf))
        a = jnp.einsum('bhtsd,bhsd->bhts', qc[:, :, :, None, :] * decay, kc)
        o = o + jnp.einsum('bhts,bhsv->bhtv', a, vc)
        S = S * jnp.exp(b_last[:, :, 0, :, None]) + jnp.einsum('bhsd,bhsv->bhdv', kc * jnp.exp(b_last - b), vc)
        return S, o

    S0 = jnp.zeros((B, HG_HEADS, HG_DK, HG_DV), jnp.float32)
    _, o = lax.scan(step, S0, (to_chunks(q), to_chunks(k), to_chunks(v), to_chunks(log_f)))
    return o.transpose(1, 0, 3, 2, 4).reshape(B, T, HG_HEADS, HG_DV)


def mixer(x, pos, w_in, ret_gn_g, hg_norm_g, hg_lb, w_ret_out, w_hg_out, w_o):
    B, T, _ = x.shape
    f32 = lambda a: a.astype(jnp.float32)
    heads = lambda a, h: f32(a).reshape(B, T, h, -1)
    proj = x @ w_in
    rq, rk, rv, rg, hq, hf, hi, hg, ga, gb = jnp.split(proj, list(np.cumsum(SPLITS)[:-1]), axis=-1)
    o_r = retention_branch(rotary(heads(rq, RET_HEADS), pos), rotary(heads(rk, RET_HEADS), pos),
                           heads(rv, RET_HEADS))
    mu = o_r.mean(-1, keepdims=True)
    var = jnp.square(o_r - mu).mean(-1, keepdims=True)
    o_r = ((o_r - mu) * lax.rsqrt(var + LN_EPS)).reshape(B, T, RET_V) * ret_gn_g
    y_r = (o_r * jax.nn.silu(f32(rg))).astype(x.dtype) @ w_ret_out
    f = hg_lb + (1.0 - hg_lb) * jax.nn.sigmoid(f32(hf))
    o_h = hgrn2_branch(jax.nn.silu(heads(hq, HG_HEADS)),
                       (1.0 - f).reshape(B, T, HG_HEADS, HG_DK),
                       heads(hi, HG_HEADS),
                       jnp.log(f).reshape(B, T, HG_HEADS, HG_DK))
    o_h = (o_h * lax.rsqrt(jnp.square(o_h).mean(-1, keepdims=True) + LN_EPS)).reshape(B, T, HG_V) * hg_norm_g
    y_h = (o_h * jax.nn.silu(f32(hg))).astype(x.dtype) @ w_hg_out
    merged = jax.nn.sigmoid(f32(ga)) * f32(y_r) + jax.nn.sigmoid(f32(gb)) * f32(y_h)
    return merged.astype(x.dtype) @ w_o


def swiglu(x, w_gate, w_up, w_down):
    return (jax.nn.silu(x @ w_gate) * (x @ w_up)) @ w_down


def moe_swiglu(x, w_router, w_gate, w_up, w_down):
    logits = (x @ w_router).astype(jnp.float32)
    top_v, top_i = lax.top_k(logits, TOP_K)
    top_w = jax.nn.softmax(top_v, axis=-1)
    gates = jnp.sum(jax.nn.one_hot(top_i, N_EXPERTS, dtype=jnp.float32) * top_w[..., None], axis=-2)
    y = jnp.zeros_like(x)
    for e in range(N_EXPERTS):
        y = y + gates[..., e:e + 1].astype(x.dtype) * swiglu(x, w_gate[e], w_up[e], w_down[e])
    return y


def setup_inputs(seed: int = 0) -> dict:
    key = jax.random.key(seed)
    ks = jax.random.split(key, 24)
    nrm = lambda k, shape, scale: jax.random.normal(k, shape, jnp.float32) * scale
    return {
        'x': nrm(ks[0], (BATCH, SEQ, D_MODEL), 1.0),
        'meta_tokens': nrm(ks[1], (N_META, D_MODEL), 1.0),
        'w_in': nrm(ks[2], (DEPTH, D_MODEL, P_IN), D_MODEL ** -0.5),
        'ret_gn_g': 1.0 + nrm(ks[3], (DEPTH, RET_V), 0.02),
        'hg_norm_g': 1.0 + nrm(ks[4], (DEPTH, HG_V), 0.02),
        'hg_lb_logits': nrm(ks[5], (DEPTH, HG_K), 0.5),
        'w_ret_out': nrm(ks[6], (DEPTH, RET_V, D_MODEL), RET_V ** -0.5),
        'w_hg_out': nrm(ks[7], (DEPTH, HG_V, D_MODEL), HG_V ** -0.5),
        'w_o': nrm(ks[8], (DEPTH, D_MODEL, D_MODEL), D_MODEL ** -0.5 * DN_BETA),
        'ln1_g': 1.0 + nrm(ks[9], (DEPTH, D_MODEL), 0.02),
        'ln1_b': nrm(ks[10], (DEPTH, D_MODEL), 0.02),
        'ln2_g': 1.0 + nrm(ks[11], (DEPTH, D_MODEL), 0.02),
        'ln2_b': nrm(ks[12], (DEPTH, D_MODEL), 0.02),
        'ffn_w_gate': nrm(ks[13], (N_DENSE, D_MODEL, D_FF), D_MODEL ** -0.5),
        'ffn_w_up': nrm(ks[14], (N_DENSE, D_MODEL, D_FF), D_MODEL ** -0.5),
        'ffn_w_down': nrm(ks[15], (N_DENSE, D_FF, D_MODEL), D_FF ** -0.5 * DN_BETA),
        'moe_router': nrm(ks[16], (N_MOE, D_MODEL, N_EXPERTS), D_MODEL ** -0.5),
        'moe_w_gate': nrm(ks[17], (N_MOE, N_EXPERTS, D_MODEL, D_FF_EXPERT), D_MODEL ** -0.5),
        'moe_w_up': nrm(ks[18], (N_MOE, N_EXPERTS, D_MODEL, D_FF_EXPERT), D_MODEL ** -0.5),
        'moe_w_down': nrm(ks[19], (N_MOE, N_EXPERTS, D_FF_EXPERT, D_MODEL), D_FF_EXPERT ** -0.5 * DN_BETA),
    }


def reference(x, meta_tokens, w_in, ret_gn_g, hg_norm_g, hg_lb_logits, w_ret_out, w_hg_out, w_o,
              ln1_g, ln1_b, ln2_g, ln2_b, ffn_w_gate, ffn_w_up, ffn_w_down,
              moe_router, moe_w_gate, moe_w_up, moe_w_down):
    B = x.shape[0]
    meta = jnp.broadcast_to(meta_tokens[None].astype(x.dtype), (B, N_META, D_MODEL))
    h = jnp.concatenate([meta, x], axis=1)
    pos = jnp.arange(h.shape[1], dtype=jnp.float32)
    lb = jnp.cumsum(jax.nn.softmax(hg_lb_logits.astype(jnp.float32), axis=0), axis=0)
    lb = lb - lb[0:1]
    for l in range(DEPTH):
        m = mixer(h, pos, w_in[l], ret_gn_g[l], hg_norm_g[l], lb[l], w_ret_out[l], w_hg_out[l], w_o[l])
        h = layer_norm(DN_ALPHA * h + m, ln1_g[l], ln1_b[l])
        if l % 2 == 0:
            f = swiglu(h, ffn_w_gate[l // 2], ffn_w_up[l // 2], ffn_w_down[l // 2])
        else:
            f = moe_swiglu(h, moe_router[l // 2], moe_w_gate[l // 2], moe_w_up[l // 2], moe_w_down[l // 2])
        h = layer_norm(DN_ALPHA * h + f, ln2_g[l], ln2_b[l])
    return h[:, N_META:]
```

```python
import functools

import jax
import jax.numpy as jnp
from jax import lax
from jax.experimental import pallas as pl
from jax.experimental.pallas import tpu as pltpu

F32 = jnp.float32
BF16 = jnp.bfloat16

RET_DK = 256
RET_DV = 512
RET_CHUNK = 128
HG_DK = 128
HG_DV = 128
HG_CHUNK = 16
TOP_K = 2
ROPE_BASE = 10000.0
LN_EPS = 1e-5
ROUTER_LANES = 128
VMEM_LIMIT_CAP_MB = 56


def _pick(n, candidates):
    for c in candidates:
        if n % c == 0:
            return c
    raise ValueError(f"no tile in {candidates} divides {n}")


def _params(n_axes, vmem_mb):
    return pltpu.CompilerParams(
        dimension_semantics=("arbitrary",) * n_axes,
        vmem_limit_bytes=min(vmem_mb, VMEM_LIMIT_CAP_MB) * 1024 * 1024)


def _mm_body(*refs, n_w, swiglu, has_gate, has_add, scale_lane):
    it = iter(refs)
    x_ref = next(it)
    w_refs = [next(it) for _ in range(n_w)]
    gate_ref = next(it) if has_gate else None
    add_ref = next(it) if has_add else None
    scale_ref = next(it) if scale_lane is not None else None
    o_ref = next(it)
    wb_refs = [next(it) for _ in range(n_w)]

    @pl.when(pl.program_id(1) == 0)
    def _():
        for w_ref, wb_ref in zip(w_refs, wb_refs):
            wb_ref[...] = w_ref[...].astype(BF16)

    x = x_ref[...]
    y = jnp.dot(x, wb_refs[0][...], preferred_element_type=F32)
    if swiglu:
        u = jnp.dot(x, wb_refs[1][...], preferred_element_type=F32)
        y = (y * jax.nn.sigmoid(y)) * u
    if has_gate:
        y = jax.nn.sigmoid(gate_ref[...].astype(F32)) * y
    if scale_lane is not None:
        g = scale_ref[...]
        lane = lax.broadcasted_iota(jnp.int32, g.shape, 1)
        y = jnp.sum(jnp.where(lane == scale_lane, g, 0.0), axis=-1, keepdims=True) * y
    if has_add:
        y = add_ref[...].astype(F32) + y
    o_ref[...] = y.astype(o_ref.dtype)


def _matmul(x, ws, widx, *, bm, bn, out_dtype, swiglu=False, gate=None, add=None,
            scale=None, name):
    m, k = x.shape
    n = ws[0].shape[2]
    assert m % bm == 0 and n % bn == 0, (m, bm, n, bn)
    in_specs = [pl.BlockSpec((bm, k), lambda j, i: (i, 0))]
    args = [x]
    for w in ws:
        in_specs.append(pl.BlockSpec((None, k, bn), lambda j, i: (widx, 0, j)))
        args.append(w)
    if gate is not None:
        garr, goff = gate
        assert goff % bn == 0
        gblk = goff // bn
        in_specs.append(pl.BlockSpec((bm, bn), lambda j, i: (i, gblk + j)))
        args.append(garr)
    if add is not None:
        in_specs.append(pl.BlockSpec((bm, bn), lambda j, i: (i, j)))
        args.append(add)
    scale_lane = None
    if scale is not None:
        sarr, scale_lane = scale
        in_specs.append(pl.BlockSpec((bm, sarr.shape[1]), lambda j, i: (i, 0)))
        args.append(sarr)
    out_bytes = jnp.dtype(out_dtype).itemsize
    vmem = (2 * bm * k * 2 + len(ws) * k * bn * (2 * 4 + 2) + 2 * bm * bn * out_bytes
            + (2 * bm * bn * 2 if gate is not None else 0)
            + (2 * bm * bn * 4 if add is not None else 0)
            + 6 * bm * bn * 4)
    body = functools.partial(_mm_body, n_w=len(ws), swiglu=swiglu, has_gate=gate is not None,
                             has_add=add is not None, scale_lane=scale_lane)
    return pl.pallas_call(
        body,
        grid=(n // bn, m // bm),
        in_specs=in_specs,
        out_specs=pl.BlockSpec((bm, bn), lambda j, i: (i, j)),
        out_shape=jax.ShapeDtypeStruct((m, n), out_dtype),
        scratch_shapes=[pltpu.VMEM((k, bn), BF16) for _ in ws],
        compiler_params=_params(2, vmem // (1024 * 1024) + 4),
        name=name,
    )(*args)


def _ln_body(h_ref, m_ref, g_ref, b_ref, o_ref, ob_ref, *, alpha, rows, tp, pad):
    z = alpha * h_ref[...] + m_ref[...]
    mu = jnp.mean(z, axis=-1, keepdims=True)
    zc = z - mu
    var = jnp.mean(zc * zc, axis=-1, keepdims=True)
    y = zc * lax.rsqrt(var + LN_EPS) * g_ref[...] + b_ref[...]
    blk = lax.rem(pl.program_id(0), tp // rows)
    r = blk * rows + lax.broadcasted_iota(jnp.int32, (rows, 1), 0)
    y = jnp.where(r >= pad, y, 0.0)
    o_ref[...] = y
    ob_ref[...] = y.astype(BF16)


def _add_ln(h, m, g, b, layer, *, alpha, tp, pad):
    mp, d = h.shape
    rows = _pick(tp, (528, 384, 256, 128, 64, 32, 16))
    body = functools.partial(_ln_body, alpha=alpha, rows=rows, tp=tp, pad=pad)
    row_spec = pl.BlockSpec((rows, d), lambda i: (i, 0))
    par_spec = pl.BlockSpec((None, 1, d), lambda i: (layer, 0, 0))
    return pl.pallas_call(
        body,
        grid=(mp // rows,),
        in_specs=[row_spec, row_spec, par_spec, par_spec],
        out_specs=[row_spec, row_spec],
        out_shape=[jax.ShapeDtypeStruct((mp, d), F32), jax.ShapeDtypeStruct((mp, d), BF16)],
        compiler_params=_params(1, 40),
        name="add_layernorm",
    )(h, m, g, b)


def _ret_body(lg_ref, q_ref, k_ref, v_ref, g_ref, cos_ref, sin_ref, gn_ref, o_ref, r_ref,
              *, n_chunks):
    c_len = RET_CHUNK
    half = RET_DK // 2
    head = pl.program_id(1)

    @pl.when(pl.program_id(2) == 0)
    def _():
        r_ref[...] = jnp.zeros_like(r_ref)

    lg = lg_ref[head]
    t_col = lax.broadcasted_iota(jnp.int32, (c_len, 1), 0).astype(F32)
    rel = (lax.broadcasted_iota(jnp.int32, (c_len, c_len), 0)
           - lax.broadcasted_iota(jnp.int32, (c_len, c_len), 1))
    intra = jnp.where(rel >= 0, jnp.exp(jnp.maximum(rel, 0).astype(F32) * lg), 0.0)
    q_decay = jnp.exp((t_col + 1.0) * lg)
    k_decay = jnp.exp((c_len - 1.0 - t_col) * lg)
    c_decay = jnp.exp(jnp.full((1, 1), float(c_len), F32) * lg)
    gn = gn_ref[...]

    def rot(x, cos, sin):
        x1, x2 = x[:, :half], x[:, half:]
        return jnp.concatenate([x1 * cos - x2 * sin, x1 * sin + x2 * cos], axis=-1)

    def chunk(c, carry):
        rows = pl.ds(pl.multiple_of(c * c_len, c_len), c_len)
        cos = cos_ref[rows, :]
        sin = sin_ref[rows, :]
        q = rot(q_ref[rows, :].astype(F32), cos, sin) * (RET_DK ** -0.5)
        k = rot(k_ref[rows, :].astype(F32), cos, sin)
        v = v_ref[rows, :]
        s = lax.dot_general(q.astype(BF16), k.astype(BF16), (((1,), (1,)), ((), ())),
                            preferred_element_type=F32) * intra
        r_old = r_ref[...]
        o = (jnp.dot(s.astype(BF16), v, preferred_element_type=F32)
             + jnp.dot((q * q_decay).astype(BF16), r_old.astype(BF16),
                       preferred_element_type=F32))
        r_ref[...] = r_old * c_decay + lax.dot_general(
            (k * k_decay).astype(BF16), v, (((0,), (0,)), ((), ())),
            preferred_element_type=F32)
        mu = jnp.mean(o, axis=-1, keepdims=True)
        oc = o - mu
        var = jnp.mean(oc * oc, axis=-1, keepdims=True)
        gate = g_ref[rows, :].astype(F32)
        y = oc * lax.rsqrt(var + LN_EPS) * gn * (gate * jax.nn.sigmoid(gate))
        o_ref[rows, :] = y.astype(o_ref.dtype)
        return carry

    lax.fori_loop(0, n_chunks, chunk, 0)


def _retention(proj, cos, sin, log_gamma, gn, layer, *, batch, tp, heads, off_q, off_k, off_v,
               off_g):
    mp = proj.shape[0]
    n_chunks_total = tp // RET_CHUNK
    cpb = _pick(n_chunks_total, (11, 8, 4, 3, 2, 1))
    tb = cpb * RET_CHUNK
    nblk = tp // tb
    qb, kb, vb, gb = off_q // RET_DK, off_k // RET_DK, off_v // RET_DV, off_g // RET_DV
    row = lambda b, h, j: b * nblk + j
    body = functools.partial(_ret_body, n_chunks=cpb)
    return pl.pallas_call(
        body,
        grid=(batch, heads, nblk),
        in_specs=[
            pl.BlockSpec(memory_space=pltpu.SMEM),
            pl.BlockSpec((tb, RET_DK), lambda b, h, j: (row(b, h, j), qb + h)),
            pl.BlockSpec((tb, RET_DK), lambda b, h, j: (row(b, h, j), kb + h)),
            pl.BlockSpec((tb, RET_DV), lambda b, h, j: (row(b, h, j), vb + h)),
            pl.BlockSpec((tb, RET_DV), lambda b, h, j: (row(b, h, j), gb + h)),
            pl.BlockSpec((tb, RET_DK // 2), lambda b, h, j: (j, 0)),
            pl.BlockSpec((tb, RET_DK // 2), lambda b, h, j: (j, 0)),
            pl.BlockSpec((None, 1, RET_DV), lambda b, h, j: (layer, 0, h)),
        ],
        out_specs=pl.BlockSpec((tb, RET_DV), lambda b, h, j: (row(b, h, j), h)),
        out_shape=jax.ShapeDtypeStruct((mp, heads * RET_DV), BF16),
        scratch_shapes=[pltpu.VMEM((RET_DK, RET_DV), F32)],
        compiler_params=_params(3, 40),
        name="retention",
    )(log_gamma, proj, proj, proj, proj, cos, sin, gn)


def _hgrn_body(lbl_ref, gn_ref, q_ref, f_ref, i_ref, g_ref, o_ref, s_ref, *, layer, n_chunks,
               heads):
    c_len = HG_CHUNK

    @pl.when(pl.program_id(1) == 0)
    def _():
        s_ref[...] = jnp.zeros_like(s_ref)

    logits = lbl_ref[...]
    e = jnp.exp(logits - jnp.max(logits, axis=0, keepdims=True))
    sm = e / jnp.sum(e, axis=0, keepdims=True)
    lb = jnp.zeros_like(sm[0:1, :])
    for j in range(1, layer + 1):
        lb = lb + sm[j:j + 1, :]
    gn = gn_ref[...]
    tri = (lax.broadcasted_iota(jnp.int32, (c_len, c_len), 0)
           >= lax.broadcasted_iota(jnp.int32, (c_len, c_len), 1)).astype(BF16)
    t_idx = lax.broadcasted_iota(jnp.int32, (c_len, 1), 0)

    def chunk(c, carry):
        rows = pl.ds(pl.multiple_of(c * c_len, c_len), c_len)
        hq = q_ref[rows, :].astype(F32)
        q = hq * jax.nn.sigmoid(hq)
        f = lb + (1.0 - lb) * jax.nn.sigmoid(f_ref[rows, :].astype(F32))
        k = 1.0 - f
        v = i_ref[rows, :]
        gate = g_ref[rows, :].astype(F32)
        gate = gate * jax.nn.sigmoid(gate)
        logf = jnp.log(f)
        p0 = logf.astype(BF16)
        r1 = logf - p0.astype(F32)
        p1 = r1.astype(BF16)
        p2 = (r1 - p1.astype(F32)).astype(BF16)
        b = (jnp.dot(tri, p0, preferred_element_type=F32)
             + jnp.dot(tri, p1, preferred_element_type=F32)
             + jnp.dot(tri, p2, preferred_element_type=F32))
        b_last = b[c_len - 1:c_len, :]
        q_in = (q * jnp.exp(b)).astype(BF16)
        k_out = (k * jnp.exp(b_last - b)).astype(BF16)
        carry_decay = jnp.exp(b_last)
        for h in range(heads):
            ks = slice(h * HG_DK, (h + 1) * HG_DK)
            vs = slice(h * HG_DV, (h + 1) * HG_DV)
            bh, qh, kh = b[:, ks], q[:, ks], k[:, ks]
            vh = v[:, vs]
            vh32 = vh.astype(F32)
            st = s_ref[h]
            o = lax.dot_general(q_in[:, ks], st.astype(BF16), (((1,), (1,)), ((), ())),
                                preferred_element_type=F32)
            for s in range(c_len):
                d = jnp.where(t_idx >= s, jnp.exp(jnp.minimum(bh - bh[s:s + 1, :], 0.0)), 0.0)
                a = jnp.sum(qh * d * kh[s:s + 1, :], axis=-1, keepdims=True)
                o = o + a * vh32[s:s + 1, :]
            s_ref[h] = st * carry_decay[:, ks] + lax.dot_general(
                vh, k_out[:, ks], (((0,), (0,)), ((), ())), preferred_element_type=F32)
            y = o * lax.rsqrt(jnp.mean(o * o, axis=-1, keepdims=True) + LN_EPS)
            y = y * gn[:, vs] * gate[:, vs]
            o_ref[rows, vs] = y.astype(o_ref.dtype)
        return carry

    lax.fori_loop(0, n_chunks, chunk, 0)


def _hgrn2(proj, lb_logits, gn, layer, *, batch, tp, heads, off_q, off_f, off_i, off_g):
    mp = proj.shape[0]
    hk, hv = heads * HG_DK, heads * HG_DV
    rb = _pick(tp, (528, 384, 256, 128, 64, 32, 16))
    nblk = tp // rb
    depth = lb_logits.shape[0]
    body = functools.partial(_hgrn_body, layer=layer, n_chunks=rb // HG_CHUNK, heads=heads)
    row = lambda b, j: b * nblk + j
    return pl.pallas_call(
        body,
        grid=(batch, nblk),
        in_specs=[
            pl.BlockSpec((depth, hk), lambda b, j: (0, 0)),
            pl.BlockSpec((None, 1, hv), lambda b, j: (layer, 0, 0)),
            pl.BlockSpec((rb, hk), lambda b, j: (row(b, j), off_q // hk)),
            pl.BlockSpec((rb, hk), lambda b, j: (row(b, j), off_f // hk)),
            pl.BlockSpec((rb, hv), lambda b, j: (row(b, j), off_i // hv)),
            pl.BlockSpec((rb, hv), lambda b, j: (row(b, j), off_g // hv)),
        ],
        out_specs=pl.BlockSpec((rb, hv), lambda b, j: (row(b, j), 0)),
        out_shape=jax.ShapeDtypeStruct((mp, hv), BF16),
        scratch_shapes=[pltpu.VMEM((heads, HG_DV, HG_DK), F32)],
        compiler_params=_params(2, 40),
        name="hgrn2",
    )(lb_logits, gn, proj, proj, proj, proj)


def _router_body(x_ref, w_ref, o_ref, *, n_experts):
    x = x_ref[...]
    w = w_ref[...]
    xh = x.astype(BF16)
    xl = (x - xh.astype(F32)).astype(BF16)
    wh = w.astype(BF16)
    wl = (w - wh.astype(F32)).astype(BF16)
    logits = (jnp.dot(xh, wh, preferred_element_type=F32)
              + jnp.dot(xh, wl, preferred_element_type=F32)
              + jnp.dot(xl, wh, preferred_element_type=F32))
    lane = lax.broadcasted_iota(jnp.int32, logits.shape, 1)
    neg = jnp.float32(-jnp.inf)
    logits = jnp.where(lane < n_experts, logits, neg)
    v1 = jnp.max(logits, axis=-1, keepdims=True)
    i1 = jnp.min(jnp.where(logits == v1, lane, ROUTER_LANES), axis=-1, keepdims=True)
    rest = jnp.where(lane == i1, neg, logits)
    v2 = jnp.max(rest, axis=-1, keepdims=True)
    i2 = jnp.min(jnp.where(rest == v2, lane, ROUTER_LANES), axis=-1, keepdims=True)
    e2 = jnp.exp(v2 - v1)
    w1 = 1.0 / (1.0 + e2)
    w2 = e2 / (1.0 + e2)
    o_ref[...] = jnp.where(lane == i1, w1, 0.0) + jnp.where(lane == i2, w2, 0.0)


def _router(h, w_router_padded, layer_idx, n_experts):
    mp, d = h.shape
    rows = _pick(mp, (528, 384, 256, 128, 64, 32, 16))
    body = functools.partial(_router_body, n_experts=n_experts)
    return pl.pallas_call(
        body,
        grid=(mp // rows,),
        in_specs=[pl.BlockSpec((rows, d), lambda i: (i, 0)),
                  pl.BlockSpec((None, d, ROUTER_LANES), lambda i: (layer_idx, 0, 0))],
        out_specs=pl.BlockSpec((rows, ROUTER_LANES), lambda i: (i, 0)),
        out_shape=jax.ShapeDtypeStruct((mp, ROUTER_LANES), F32),
        compiler_params=_params(1, 40),
        name="moe_router",
    )(h, w_router_padded)


def kernel(x, meta_tokens, w_in, ret_gn_g, hg_norm_g, hg_lb_logits, w_ret_out, w_hg_out, w_o,
           ln1_g, ln1_b, ln2_g, ln2_b, ffn_w_gate, ffn_w_up, ffn_w_down,
           moe_router, moe_w_gate, moe_w_up, moe_w_down):
    batch, seq, d = x.shape
    n_meta = meta_tokens.shape[0]
    depth, _, p_in = w_in.shape
    ret_v = w_ret_out.shape[1]
    hg_v = w_hg_out.shape[1]
    hg_k = hg_lb_logits.shape[1]
    ret_qk = (p_in - 2 * ret_v - 2 * hg_k - 2 * hg_v - 2 * d) // 2
    ret_heads = ret_qk // RET_DK
    hg_heads = hg_k // HG_DK
    n_experts = moe_router.shape[2]
    assert ret_v == ret_heads * RET_DV and hg_v == hg_heads * HG_DV and hg_k == hg_v
    assert n_experts <= ROUTER_LANES

    pad = RET_CHUNK - n_meta
    tp = pad + n_meta + seq
    assert tp % RET_CHUNK == 0 and tp % HG_CHUNK == 0 and pad % HG_CHUNK == 0
    mp = batch * tp
    alpha = (2 * depth) ** 0.25

    off = {}
    acc = 0
    for nm, wd in (("rq", ret_qk), ("rk", ret_qk), ("rv", ret_v), ("rg", ret_v), ("hq", hg_k),
                   ("hf", hg_k), ("hi", hg_v), ("hg", hg_v), ("ga", d), ("gb", d)):
        off[nm] = acc
        acc += wd

    meta = jnp.broadcast_to(meta_tokens[None].astype(F32), (batch, n_meta, d))
    h = jnp.concatenate([jnp.zeros((batch, pad, d), F32), meta, x.astype(F32)], axis=1)
    h = h.reshape(mp, d)
    hb = h.astype(BF16)

    pos = jnp.arange(tp, dtype=F32) - float(pad)
    inv = 1.0 / (ROPE_BASE ** jnp.linspace(0.0, 1.0, RET_DK // 2, dtype=F32))
    ang = pos[:, None] * inv[None, :]
    cos, sin = jnp.cos(ang), jnp.sin(ang)
    log_gamma = jnp.log1p(-jnp.exp2(-5.0 - jnp.arange(ret_heads, dtype=F32)))

    ret_gn3 = ret_gn_g.astype(F32).reshape(depth, 1, ret_v)
    hg_gn3 = hg_norm_g.astype(F32).reshape(depth, 1, hg_v)
    ln1_g3, ln1_b3 = ln1_g.reshape(depth, 1, d), ln1_b.reshape(depth, 1, d)
    ln2_g3, ln2_b3 = ln2_g.reshape(depth, 1, d), ln2_b.reshape(depth, 1, d)
    n_moe = moe_router.shape[0]
    d_ffe = moe_w_gate.shape[3]
    router_w = jnp.pad(moe_router.astype(F32), ((0, 0), (0, 0), (0, ROUTER_LANES - n_experts)))
    moe_wg = moe_w_gate.reshape(n_moe * n_experts, d, d_ffe)
    moe_wu = moe_w_up.reshape(n_moe * n_experts, d, d_ffe)
    moe_wd = moe_w_down.reshape(n_moe * n_experts, d_ffe, d)

    bm = _pick(mp, (1056, 768, 512, 384, 256, 128, 64, 32, 16))
    bm_half = _pick(mp, (528, 384, 256, 128, 64, 32, 16))
    tile = lambda n, cands: _pick(n, cands)

    for l in range(depth):
        proj = _matmul(hb, [w_in], l, bm=bm, bn=tile(p_in, (1024, 512, 256, 128)),
                       out_dtype=BF16, name="in_proj")
        yr = _retention(proj, cos, sin, log_gamma, ret_gn3, l, batch=batch, tp=tp,
                        heads=ret_heads, off_q=off["rq"], off_k=off["rk"], off_v=off["rv"],
                        off_g=off["rg"])
        yh = _hgrn2(proj, hg_lb_logits.astype(F32), hg_gn3, l, batch=batch, tp=tp,
                    heads=hg_heads, off_q=off["hq"], off_f=off["hf"], off_i=off["hi"],
                    off_g=off["hg"])
        bn_o = tile(d, (512, 256, 128))
        t_r = _matmul(yr, [w_ret_out], l, bm=bm_half, bn=bn_o, out_dtype=F32,
                      gate=(proj, off["ga"]), name="ret_out_proj")
        merged = _matmul(yh, [w_hg_out], l, bm=bm, bn=bn_o, out_dtype=BF16,
                         gate=(proj, off["gb"]), add=t_r, name="hg_out_proj")
        m = _matmul(merged, [w_o], l, bm=bm, bn=bn_o, out_dtype=F32, name="mix_out_proj")
        h, hb = _add_ln(h, m, ln1_g3, ln1_b3, l, alpha=alpha, tp=tp, pad=pad)
        if l % 2 == 0:
            d_ff = ffn_w_gate.shape[2]
            act = _matmul(hb, [ffn_w_gate, ffn_w_up], l // 2, bm=bm,
                          bn=tile(d_ff, (512, 256, 128)), out_dtype=BF16, swiglu=True,
                          name="ffn_up")
            f = _matmul(act, [ffn_w_down], l // 2, bm=bm_half, bn=tile(d, (256, 128)),
                        out_dtype=F32, name="ffn_down")
        else:
            gates = _router(h, router_w, l // 2, n_experts)
            f = None
            for e in range(n_experts):
                we = (l // 2) * n_experts + e
                act = _matmul(hb, [moe_wg, moe_wu], we, bm=bm, bn=tile(d_ffe, (256, 128)),
                              out_dtype=BF16, swiglu=True, name="moe_up")
                f = _matmul(act, [moe_wd], we, bm=bm, bn=tile(d, (512, 256, 128)),
                            out_dtype=F32, scale=(gates, e), add=f, name="moe_down")
        h, hb = _add_ln(h, f, ln2_g3, ln2_b3, l, alpha=alpha, tp=tp, pad=pad)

    out = h.reshape(batch, tp, d)[:, pad + n_meta:, :]
    return out.astype(x.dtype)
```

```python
import functools

import jax
import jax.numpy as jnp
from jax import lax
from jax.experimental import pallas as pl
from jax.experimental.pallas import tpu as pltpu

F32 = jnp.float32
BF16 = jnp.bfloat16

RET_DK = 256
RET_DV = 512
RET_CHUNK = 128
HG_DK = 128
HG_DV = 128
HG_CHUNK = 16
TOP_K = 2
ROPE_BASE = 10000.0
LN_EPS = 1e-5
ROUTER_LANES = 128
VMEM_LIMIT_CAP_MB = 56


def _pick(n, candidates):
    for c in candidates:
        if n % c == 0:
            return c
    raise ValueError(f"no tile in {candidates} divides {n}")


def _params(n_axes, vmem_mb):
    return pltpu.CompilerParams(
        dimension_semantics=("arbitrary",) * n_axes,
        vmem_limit_bytes=min(vmem_mb, VMEM_LIMIT_CAP_MB) * 1024 * 1024)


def _mm_body(*refs, n_w, swiglu, has_gate, has_add, scale_lane):
    it = iter(refs)
    x_ref = next(it)
    w_refs = [next(it) for _ in range(n_w)]
    gate_ref = next(it) if has_gate else None
    add_ref = next(it) if has_add else None
    scale_ref = next(it) if scale_lane is not None else None
    o_ref = next(it)
    wb_refs = [next(it) for _ in range(n_w)]

    @pl.when(pl.program_id(1) == 0)
    def _():
        for w_ref, wb_ref in zip(w_refs, wb_refs):
            wb_ref[...] = w_ref[...].astype(BF16)

    x = x_ref[...]
    y = jnp.dot(x, wb_refs[0][...], preferred_element_type=F32)
    if swiglu:
        u = jnp.dot(x, wb_refs[1][...], preferred_element_type=F32)
        y = (y * jax.nn.sigmoid(y)) * u
    if has_gate:
        y = jax.nn.sigmoid(gate_ref[...].astype(F32)) * y
    if scale_lane is not None:
        g = scale_ref[...]
        lane = lax.broadcasted_iota(jnp.int32, g.shape, 1)
        y = jnp.sum(jnp.where(lane == scale_lane, g, 0.0), axis=-1, keepdims=True) * y
    if has_add:
        y = add_ref[...].astype(F32) + y
    o_ref[...] = y.astype(o_ref.dtype)


def _matmul(x, ws, widx, *, bm, bn, out_dtype, swiglu=False, gate=None, add=None,
            scale=None, name):
    m, k = x.shape
    n = ws[0].shape[2]
    assert m % bm == 0 and n % bn == 0, (m, bm, n, bn)
    in_specs = [pl.BlockSpec((bm, k), lambda j, i: (i, 0))]
    args = [x]
    for w in ws:
        in_specs.append(pl.BlockSpec((None, k, bn), lambda j, i: (widx, 0, j)))
        args.append(w)
    if gate is not None:
        garr, goff = gate
        assert goff % bn == 0
        gblk = goff // bn
        in_specs.append(pl.BlockSpec((bm, bn), lambda j, i: (i, gblk + j)))
        args.append(garr)
    if add is not None:
        in_specs.append(pl.BlockSpec((bm, bn), lambda j, i: (i, j)))
        args.append(add)
    scale_lane = None
    if scale is not None:
        sarr, scale_lane = scale
        in_specs.append(pl.BlockSpec((bm, sarr.shape[1]), lambda j, i: (i, 0)))
        args.append(sarr)
    out_bytes = jnp.dtype(out_dtype).itemsize
    vmem = (2 * bm * k * 2 + len(ws) * k * bn * (2 * 4 + 2) + 2 * bm * bn * out_bytes
            + (2 * bm * bn * 2 if gate is not None else 0)
            + (2 * bm * bn * 4 if add is not None else 0)
            + 6 * bm * bn * 4)
    body = functools.partial(_mm_body, n_w=len(ws), swiglu=swiglu, has_gate=gate is not None,
                             has_add=add is not None, scale_lane=scale_lane)
    return pl.pallas_call(
        body,
        grid=(n // bn, m // bm),
        in_specs=in_specs,
        out_specs=pl.BlockSpec((bm, bn), lambda j, i: (i, j)),
        out_shape=jax.ShapeDtypeStruct((m, n), out_dtype),
        scratch_shapes=[pltpu.VMEM((k, bn), BF16) for _ in ws],
        compiler_params=_params(2, vmem // (1024 * 1024) + 4),
        name=name,
    )(*args)


def _ln_body(h_ref, m_ref, g_ref, b_ref, o_ref, ob_ref, *, alpha, rows, tp, pad):
    z = alpha * h_ref[...] + m_ref[...]
    mu = jnp.mean(z, axis=-1, keepdims=True)
    zc = z - mu
    var = jnp.mean(zc * zc, axis=-1, keepdims=True)
    y = zc * lax.rsqrt(var + LN_EPS) * g_ref[...] + b_ref[...]
    blk = lax.rem(pl.program_id(0), tp // rows)
    r = blk * rows + lax.broadcasted_iota(jnp.int32, (rows, 1), 0)
    y = jnp.where(r >= pad, y, 0.0)
    o_ref[...] = y
    ob_ref[...] = y.astype(BF16)


def _add_ln(h, m, g, b, layer, *, alpha, tp, pad):
    mp, d = h.shape
    rows = _pick(tp, (528, 384, 256, 128, 64, 32, 16))
    body = functools.partial(_ln_body, alpha=alpha, rows=rows, tp=tp, pad=pad)
    row_spec = pl.BlockSpec((rows, d), lambda i: (i, 0))
    par_spec = pl.BlockSpec((None, 1, d), lambda i: (layer, 0, 0))
    return pl.pallas_call(
        body,
        grid=(mp // rows,),
        in_specs=[row_spec, row_spec, par_spec, par_spec],
        out_specs=[row_spec, row_spec],
        out_shape=[jax.ShapeDtypeStruct((mp, d), F32), jax.ShapeDtypeStruct((mp, d), BF16)],
        compiler_params=_params(1, 40),
        name="add_layernorm",
    )(h, m, g, b)


def _ret_body(lg_ref, q_ref, k_ref, v_ref, g_ref, cos_ref, sin_ref, gn_ref, o_ref, r_ref,
              *, n_chunks):
    c_len = RET_CHUNK
    half = RET_DK // 2
    head = pl.program_id(1)

    @pl.when(pl.program_id(2) == 0)
    def _():
        r_ref[...] = jnp.zeros_like(r_ref)

    lg = lg_ref[head]
    t_col = lax.broadcasted_iota(jnp.int32, (c_len, 1), 0).astype(F32)
    rel = (lax.broadcasted_iota(jnp.int32, (c_len, c_len), 0)
           - lax.broadcasted_iota(jnp.int32, (c_len, c_len), 1))
    intra = jnp.where(rel >= 0, jnp.exp(jnp.maximum(rel, 0).astype(F32) * lg), 0.0)
    q_decay = jnp.exp((t_col + 1.0) * lg)
    k_decay = jnp.exp((c_len - 1.0 - t_col) * lg)
    c_decay = jnp.exp(jnp.full((1, 1), float(c_len), F32) * lg)
    gn = gn_ref[...]

    def rot(x, cos, sin):
        x1, x2 = x[:, :half], x[:, half:]
        return jnp.concatenate([x1 * cos - x2 * sin, x1 * sin + x2 * cos], axis=-1)

    def chunk(c, carry):
        rows = pl.ds(pl.multiple_of(c * c_len, c_len), c_len)
        cos = cos_ref[rows, :]
        sin = sin_ref[rows, :]
        q = rot(q_ref[rows, :].astype(F32), cos, sin) * (RET_DK ** -0.5)
        k = rot(k_ref[rows, :].astype(F32), cos, sin)
        v = v_ref[rows, :]
        s = lax.dot_general(q.astype(BF16), k.astype(BF16), (((1,), (1,)), ((), ())),
                            preferred_element_type=F32) * intra
        r_old = r_ref[...]
        o = (jnp.dot(s.astype(BF16), v, preferred_element_type=F32)
             + jnp.dot((q * q_decay).astype(BF16), r_old.astype(BF16),
                       preferred_element_type=F32))
        r_ref[...] = r_old * c_decay + lax.dot_general(
            (k * k_decay).astype(BF16), v, (((0,), (0,)), ((), ())),
            preferred_element_type=F32)
        mu = jnp.mean(o, axis=-1, keepdims=True)
        oc = o - mu
        var = jnp.mean(oc * oc, axis=-1, keepdims=True)
        gate = g_ref[rows, :].astype(F32)
        y = oc * lax.rsqrt(var + LN_EPS) * gn * (gate * jax.nn.sigmoid(gate))
        o_ref[rows, :] = y.astype(o_ref.dtype)
        return carry

    lax.fori_loop(0, n_chunks, chunk, 0)


def _retention(proj, cos, sin, log_gamma, gn, layer, *, batch, tp, heads, off_q, off_k, off_v,
               off_g):
    mp = proj.shape[0]
    n_chunks_total = tp // RET_CHUNK
    cpb = _pick(n_chunks_total, (11, 8, 4, 3, 2, 1))
    tb = cpb * RET_CHUNK
    nblk = tp // tb
    qb, kb, vb, gb = off_q // RET_DK, off_k // RET_DK, off_v // RET_DV, off_g // RET_DV
    row = lambda b, h, j: b * nblk + j
    body = functools.partial(_ret_body, n_chunks=cpb)
    return pl.pallas_call(
        body,
        grid=(batch, heads, nblk),
        in_specs=[
            pl.BlockSpec(memory_space=pltpu.SMEM),
            pl.BlockSpec((tb, RET_DK), lambda b, h, j: (row(b, h, j), qb + h)),
            pl.BlockSpec((tb, RET_DK), lambda b, h, j: (row(b, h, j), kb + h)),
            pl.BlockSpec((tb, RET_DV), lambda b, h, j: (row(b, h, j), vb + h)),
            pl.BlockSpec((tb, RET_DV), lambda b, h, j: (row(b, h, j), gb + h)),
            pl.BlockSpec((tb, RET_DK // 2), lambda b, h, j: (j, 0)),
            pl.BlockSpec((tb, RET_DK // 2), lambda b, h, j: (j, 0)),
            pl.BlockSpec((None, 1, RET_DV), lambda b, h, j: (layer, 0, h)),
        ],
        out_specs=pl.BlockSpec((tb, RET_DV), lambda b, h, j: (row(b, h, j), h)),
        out_shape=jax.ShapeDtypeStruct((mp, heads * RET_DV), BF16),
        scratch_shapes=[pltpu.VMEM((RET_DK, RET_DV), F32)],
        compiler_params=_params(3, 40),
        name="retention",
    )(log_gamma, proj, proj, proj, proj, cos, sin, gn)


def _hgrn_body(lbl_ref, gn_ref, q_ref, f_ref, i_ref, g_ref, o_ref, s_ref, *, layer, n_chunks,
               heads):
    c_len = HG_CHUNK

    @pl.when(pl.program_id(1) == 0)
    def _():
        s_ref[...] = jnp.zeros_like(s_ref)

    logits = lbl_ref[...]
    e = jnp.exp(logits - jnp.max(logits, axis=0, keepdims=True))
    sm = e / jnp.sum(e, axis=0, keepdims=True)
    lb = jnp.zeros_like(sm[0:1, :])
    for j in range(1, layer + 1):
        lb = lb + sm[j:j + 1, :]
    gn = gn_ref[...]
    tri = (lax.broadcasted_iota(jnp.int32, (c_len, c_len), 0)
           >= lax.broadcasted_iota(jnp.int32, (c_len, c_len), 1)).astype(BF16)
    t_idx = lax.broadcasted_iota(jnp.int32, (c_len, 1), 0)

    def chunk(c, carry):
        rows = pl.ds(pl.multiple_of(c * c_len, c_len), c_len)
        hq = q_ref[rows, :].astype(F32)
        q = hq * jax.nn.sigmoid(hq)
        f = lb + (1.0 - lb) * jax.nn.sigmoid(f_ref[rows, :].astype(F32))
        k = 1.0 - f
        v = i_ref[rows, :]
        gate = g_ref[rows, :].astype(F32)
        gate = gate * jax.nn.sigmoid(gate)
        logf = jnp.log(f)
        p0 = logf.astype(BF16)
        r1 = logf - p0.astype(F32)
        p1 = r1.astype(BF16)
        p2 = (r1 - p1.astype(F32)).astype(BF16)
        b = (jnp.dot(tri, p0, preferred_element_type=F32)
             + jnp.dot(tri, p1, preferred_element_type=F32)
             + jnp.dot(tri, p2, preferred_element_type=F32))
        b_last = b[c_len - 1:c_len, :]
        q_in = (q * jnp.exp(b)).astype(BF16)
        k_out = (k * jnp.exp(b_last - b)).astype(BF16)
        carry_decay = jnp.exp(b_last)
        for h in range(heads):
            ks = slice(h * HG_DK, (h + 1) * HG_DK)
            vs = slice(h * HG_DV, (h + 1) * HG_DV)
            bh, qh, kh = b[:, ks], q[:, ks], k[:, ks]
            vh = v[:, vs]
            vh32 = vh.astype(F32)
            st = s_ref[h]
            o = lax.dot_general(q_in[:, ks], st.astype(BF16), (((1,), (1,)), ((), ())),
                                preferred_element_type=F32)
            for s in range(c_len):
                d = jnp.where(t_idx >= s, jnp.exp(jnp.minimum(bh - bh[s:s + 1, :], 0.0)), 0.0)
                a = jnp.sum(qh * d * kh[s:s + 1, :], axis=-1, keepdims=True)
                o = o + a * vh32[s:s + 1, :]
            s_ref[h] = st * carry_decay[:, ks] + lax.dot_general(
                vh, k_out[:, ks], (((0,), (0,)), ((), ())), preferred_element_type=F32)
            y = o * lax.rsqrt(jnp.mean(o * o, axis=-1, keepdims=True) + LN_EPS)
            y = y * gn[:, vs] * gate[:, vs]
            o_ref[rows, vs] = y.astype(o_ref.dtype)
        return carry

    lax.fori_loop(0, n_chunks, chunk, 0)


def _hgrn2(proj, lb_logits, gn, layer, *, batch, tp, heads, off_q, off_f, off_i, off_g):
    mp = proj.shape[0]
    hk, hv = heads * HG_DK, heads * HG_DV
    rb = _pick(tp, (528, 384, 256, 128, 64, 32, 16))
    nblk = tp // rb
    depth = lb_logits.shape[0]
    body = functools.partial(_hgrn_body, layer=layer, n_chunks=rb // HG_CHUNK, heads=heads)
    row = lambda b, j: b * nblk + j
    return pl.pallas_call(
        body,
        grid=(batch, nblk),
        in_specs=[
            pl.BlockSpec((depth, hk), lambda b, j: (0, 0)),
            pl.BlockSpec((None, 1, hv), lambda b, j: (layer, 0, 0)),
            pl.BlockSpec((rb, hk), lambda b, j: (row(b, j), off_q // hk)),
            pl.BlockSpec((rb, hk), lambda b, j: (row(b, j), off_f // hk)),
            pl.BlockSpec((rb, hv), lambda b, j: (row(b, j), off_i // hv)),
            pl.BlockSpec((rb, hv), lambda b, j: (row(b, j), off_g // hv)),
        ],
        out_specs=pl.BlockSpec((rb, hv), lambda b, j: (row(b, j), 0)),
        out_shape=jax.ShapeDtypeStruct((mp, hv), BF16),
        scratch_shapes=[pltpu.VMEM((heads, HG_DV, HG_DK), F32)],
        compiler_params=_params(2, 40),
        name="hgrn2",
    )(lb_logits, gn, proj, proj, proj, proj)


def _router_body(x_ref, w_ref, idx_ref, wts_ref, *, n_experts):
    x = x_ref[...]
    w = w_ref[...]
    xh = x.astype(BF16)
    xl = (x - xh.astype(F32)).astype(BF16)
    wh = w.astype(BF16)
    wl = (w - wh.astype(F32)).astype(BF16)
    logits = (jnp.dot(xh, wh, preferred_element_type=F32)
              + jnp.dot(xh, wl, preferred_element_type=F32)
              + jnp.dot(xl, wh, preferred_element_type=F32))
    lane = lax.broadcasted_iota(jnp.int32, logits.shape, 1)
    neg = jnp.float32(-jnp.inf)
    logits = jnp.where(lane < n_experts, logits, neg)
    v1 = jnp.max(logits, axis=-1, keepdims=True)
    i1 = jnp.min(jnp.where(logits == v1, lane, ROUTER_LANES), axis=-1, keepdims=True)
    rest = jnp.where(lane == i1, neg, logits)
    v2 = jnp.max(rest, axis=-1, keepdims=True)
    i2 = jnp.min(jnp.where(rest == v2, lane, ROUTER_LANES), axis=-1, keepdims=True)
    e2 = jnp.exp(v2 - v1)
    w1 = 1.0 / (1.0 + e2)
    w2 = e2 / (1.0 + e2)
    idx_ref[...] = jnp.where(lane == 0, i1, jnp.where(lane == 1, i2, 0))
    wts_ref[...] = jnp.where(lane == 0, w1, jnp.where(lane == 1, w2, 0.0))


def _router(h, w_router_padded, layer_idx, n_experts, tp):
    mp, d = h.shape
    rows = _pick(tp, (528, 384, 256, 128, 64, 32, 16))
    body = functools.partial(_router_body, n_experts=n_experts)
    out_spec = pl.BlockSpec((rows, ROUTER_LANES), lambda i: (i, 0))
    return pl.pallas_call(
        body,
        grid=(mp // rows,),
        in_specs=[pl.BlockSpec((rows, d), lambda i: (i, 0)),
                  pl.BlockSpec((None, d, ROUTER_LANES), lambda i: (layer_idx, 0, 0))],
        out_specs=[out_spec, out_spec],
        out_shape=[jax.ShapeDtypeStruct((mp, ROUTER_LANES), jnp.int32),
                   jax.ShapeDtypeStruct((mp, ROUTER_LANES), F32)],
        compiler_params=_params(1, 40),
        name="moe_router",
    )(h, w_router_padded)


MOE_TM = 512
GATHER_ROWS = 256


def _gather_body(src_ref, tv_ref, x_hbm, o_ref, buf, sem, *, rows, tiles_per_step_inv):
    step = pl.program_id(0)
    base = step * rows
    valid = tv_ref[step // tiles_per_step_inv] == 1

    @pl.when(valid)
    def _():
        def issue(r, c):
            t = src_ref[base + r]
            pltpu.make_async_copy(x_hbm.at[pl.ds(t, 1), :], buf.at[pl.ds(r, 1), :], sem).start()
            return c
        lax.fori_loop(0, rows, issue, 0, unroll=8)
        pltpu.make_async_copy(x_hbm.at[pl.ds(0, rows), :], buf, sem).wait()
        o_ref[...] = buf[...].astype(o_ref.dtype)

    @pl.when(jnp.logical_not(valid))
    def _():
        o_ref[...] = jnp.zeros_like(o_ref)


def _gather_rows(h, src, tile_valid, *, n_rows):
    mp, d = h.shape
    rows = GATHER_ROWS
    assert n_rows % rows == 0 and MOE_TM % rows == 0
    body = functools.partial(_gather_body, rows=rows, tiles_per_step_inv=MOE_TM // rows)
    return pl.pallas_call(
        body,
        grid_spec=pltpu.PrefetchScalarGridSpec(
            num_scalar_prefetch=2,
            grid=(n_rows // rows,),
            in_specs=[pl.BlockSpec(memory_space=pl.ANY)],
            out_specs=pl.BlockSpec((rows, d), lambda i, src, tv: (i, 0)),
            scratch_shapes=[pltpu.VMEM((rows, d), F32), pltpu.SemaphoreType.DMA(())],
        ),
        out_shape=jax.ShapeDtypeStruct((n_rows, d), BF16),
        compiler_params=_params(1, 32),
        name="moe_gather",
    )(src, tile_valid, h)


def _gmm_body(te_ref, tv_ref, *refs, n_w, swiglu):
    it = iter(refs)
    x_ref = next(it)
    w_refs = [next(it) for _ in range(n_w)]
    o_ref = next(it)
    wb_refs = [next(it) for _ in range(n_w)]
    i = pl.program_id(1)
    fresh = jnp.logical_or(i == 0, te_ref[i] != te_ref[jnp.maximum(i - 1, 0)])

    @pl.when(fresh)
    def _():
        for w_ref, wb_ref in zip(w_refs, wb_refs):
            wb_ref[...] = w_ref[...].astype(BF16)

    @pl.when(tv_ref[i] == 1)
    def _():
        x = x_ref[...]
        y = jnp.dot(x, wb_refs[0][...], preferred_element_type=F32)
        if swiglu:
            u = jnp.dot(x, wb_refs[1][...], preferred_element_type=F32)
            y = (y * jax.nn.sigmoid(y)) * u
        o_ref[...] = y.astype(o_ref.dtype)

    @pl.when(tv_ref[i] == 0)
    def _():
        o_ref[...] = jnp.zeros_like(o_ref)


def _grouped_matmul(x, ws, tile_expert, tile_valid, wbase, *, bn, out_dtype, swiglu, name):
    m, k = x.shape
    n = ws[0].shape[2]
    tm = MOE_TM
    assert m % tm == 0 and n % bn == 0
    in_specs = [pl.BlockSpec((tm, k), lambda j, i, te, tv: (i, 0))]
    for _ in ws:
        in_specs.append(pl.BlockSpec((None, k, bn), lambda j, i, te, tv: (wbase + te[i], 0, j)))
    out_bytes = jnp.dtype(out_dtype).itemsize
    vmem = (2 * tm * k * 2 + len(ws) * k * bn * (2 * 4 + 2) + 2 * tm * bn * out_bytes
            + 6 * tm * bn * 4)
    body = functools.partial(_gmm_body, n_w=len(ws), swiglu=swiglu)
    return pl.pallas_call(
        body,
        grid_spec=pltpu.PrefetchScalarGridSpec(
            num_scalar_prefetch=2,
            grid=(n // bn, m // tm),
            in_specs=in_specs,
            out_specs=pl.BlockSpec((tm, bn), lambda j, i, te, tv: (i, j)),
            scratch_shapes=[pltpu.VMEM((k, bn), BF16) for _ in ws],
        ),
        out_shape=jax.ShapeDtypeStruct((m, n), out_dtype),
        compiler_params=_params(2, vmem // (1024 * 1024) + 4),
        name=name,
    )(tile_expert, tile_valid, x, *ws)


def _combine_ln_body(pos_ref, h_ref, w_ref, g_ref, b_ref, y_hbm, o_ref, ob_ref, buf, sem,
                     *, alpha, rows, tp, pad):
    base = pl.program_id(0) * rows

    def issue(r, c):
        for s in range(TOP_K):
            p = pos_ref[(base + r) * TOP_K + s]
            pltpu.make_async_copy(y_hbm.at[pl.ds(p, 1), :], buf.at[s, pl.ds(r, 1), :], sem).start()
        return c
    lax.fori_loop(0, rows, issue, 0, unroll=4)
    for s in range(TOP_K):
        pltpu.make_async_copy(y_hbm.at[pl.ds(0, rows), :], buf.at[s], sem).wait()

    w = w_ref[...]
    lane = lax.broadcasted_iota(jnp.int32, w.shape, 1)
    f = None
    for s in range(TOP_K):
        ws = jnp.sum(jnp.where(lane == s, w, 0.0), axis=-1, keepdims=True)
        term = ws * buf[s]
        f = term if f is None else f + term
    z = alpha * h_ref[...] + f
    mu = jnp.mean(z, axis=-1, keepdims=True)
    zc = z - mu
    var = jnp.mean(zc * zc, axis=-1, keepdims=True)
    y = zc * lax.rsqrt(var + LN_EPS) * g_ref[...] + b_ref[...]
    blk = lax.rem(pl.program_id(0), tp // rows)
    r = blk * rows + lax.broadcasted_iota(jnp.int32, (rows, 1), 0)
    y = jnp.where(r >= pad, y, 0.0)
    o_ref[...] = y
    ob_ref[...] = y.astype(BF16)


def _combine_ln(h, ys, pos, wts, g, b, layer, *, alpha, tp, pad):
    mp, d = h.shape
    rows = _pick(tp, (384, 256, 128, 64, 32, 16))
    body = functools.partial(_combine_ln_body, alpha=alpha, rows=rows, tp=tp, pad=pad)
    row_spec = pl.BlockSpec((rows, d), lambda i, p: (i, 0))
    par_spec = pl.BlockSpec((None, 1, d), lambda i, p: (layer, 0, 0))
    return pl.pallas_call(
        body,
        grid_spec=pltpu.PrefetchScalarGridSpec(
            num_scalar_prefetch=1,
            grid=(mp // rows,),
            in_specs=[row_spec, pl.BlockSpec((rows, ROUTER_LANES), lambda i, p: (i, 0)),
                      par_spec, par_spec, pl.BlockSpec(memory_space=pl.ANY)],
            out_specs=[row_spec, row_spec],
            scratch_shapes=[pltpu.VMEM((TOP_K, rows, d), F32), pltpu.SemaphoreType.DMA(())],
        ),
        out_shape=[jax.ShapeDtypeStruct((mp, d), F32), jax.ShapeDtypeStruct((mp, d), BF16)],
        compiler_params=_params(1, 40),
        name="moe_combine_layernorm",
    )(pos, h, wts, g, b, ys)


def _route(idx, *, n_experts, tp, pad):
    mp = idx.shape[0]
    tm = MOE_TM
    n_assign = mp * TOP_K
    n_tiles = -(-n_assign // tm) + n_experts
    n_rows = n_tiles * tm
    token = jnp.arange(mp, dtype=jnp.int32)
    real = (token % tp) >= pad
    e = jnp.where(real[:, None], idx[:, :TOP_K], n_experts).reshape(n_assign)
    onehot = (e[:, None] == jnp.arange(n_experts, dtype=jnp.int32)[None, :]).astype(jnp.int32)
    csum = jnp.cumsum(onehot, axis=0)
    counts = csum[-1]
    rank = jnp.sum(onehot * csum, axis=1) - 1
    gsz = ((counts + tm - 1) // tm) * tm
    gend = jnp.cumsum(gsz)
    gstart = gend - gsz
    routed = e < n_experts
    dst = jnp.where(routed, gstart[jnp.minimum(e, n_experts - 1)] + rank, n_rows)
    assign_token = jnp.arange(n_assign, dtype=jnp.int32) // TOP_K
    src = jnp.zeros((n_rows,), jnp.int32).at[dst].set(assign_token, mode="drop")
    pos = jnp.where(routed, dst, 0).astype(jnp.int32)
    tile_start = jnp.arange(n_tiles, dtype=jnp.int32) * tm
    tile_expert = jnp.minimum(jnp.searchsorted(gend, tile_start, side="right"),
                              n_experts - 1).astype(jnp.int32)
    tile_valid = (tile_start < gend[-1]).astype(jnp.int32)
    return src, pos, tile_expert, tile_valid, n_rows


def kernel(x, meta_tokens, w_in, ret_gn_g, hg_norm_g, hg_lb_logits, w_ret_out, w_hg_out, w_o,
           ln1_g, ln1_b, ln2_g, ln2_b, ffn_w_gate, ffn_w_up, ffn_w_down,
           moe_router, moe_w_gate, moe_w_up, moe_w_down):
    batch, seq, d = x.shape
    n_meta = meta_tokens.shape[0]
    depth, _, p_in = w_in.shape
    ret_v = w_ret_out.shape[1]
    hg_v = w_hg_out.shape[1]
    hg_k = hg_lb_logits.shape[1]
    ret_qk = (p_in - 2 * ret_v - 2 * hg_k - 2 * hg_v - 2 * d) // 2
    ret_heads = ret_qk // RET_DK
    hg_heads = hg_k // HG_DK
    n_experts = moe_router.shape[2]
    assert ret_v == ret_heads * RET_DV and hg_v == hg_heads * HG_DV and hg_k == hg_v
    assert n_experts <= ROUTER_LANES

    pad = RET_CHUNK - n_meta
    tp = pad + n_meta + seq
    assert tp % RET_CHUNK == 0 and tp % HG_CHUNK == 0 and pad % HG_CHUNK == 0
    mp = batch * tp
    alpha = (2 * depth) ** 0.25

    off = {}
    acc = 0
    for nm, wd in (("rq", ret_qk), ("rk", ret_qk), ("rv", ret_v), ("rg", ret_v), ("hq", hg_k),
                   ("hf", hg_k), ("hi", hg_v), ("hg", hg_v), ("ga", d), ("gb", d)):
        off[nm] = acc
        acc += wd

    meta = jnp.broadcast_to(meta_tokens[None].astype(F32), (batch, n_meta, d))
    h = jnp.concatenate([jnp.zeros((batch, pad, d), F32), meta, x.astype(F32)], axis=1)
    h = h.reshape(mp, d)
    hb = h.astype(BF16)

    pos = jnp.arange(tp, dtype=F32) - float(pad)
    inv = 1.0 / (ROPE_BASE ** jnp.linspace(0.0, 1.0, RET_DK // 2, dtype=F32))
    ang = pos[:, None] * inv[None, :]
    cos, sin = jnp.cos(ang), jnp.sin(ang)
    log_gamma = jnp.log1p(-jnp.exp2(-5.0 - jnp.arange(ret_heads, dtype=F32)))

    ret_gn3 = ret_gn_g.astype(F32).reshape(depth, 1, ret_v)
    hg_gn3 = hg_norm_g.astype(F32).reshape(depth, 1, hg_v)
    ln1_g3, ln1_b3 = ln1_g.reshape(depth, 1, d), ln1_b.reshape(depth, 1, d)
    ln2_g3, ln2_b3 = ln2_g.reshape(depth, 1, d), ln2_b.reshape(depth, 1, d)
    n_moe = moe_router.shape[0]
    d_ffe = moe_w_gate.shape[3]
    router_w = jnp.pad(moe_router.astype(F32), ((0, 0), (0, 0), (0, ROUTER_LANES - n_experts)))
    moe_wg = moe_w_gate.reshape(n_moe * n_experts, d, d_ffe)
    moe_wu = moe_w_up.reshape(n_moe * n_experts, d, d_ffe)
    moe_wd = moe_w_down.reshape(n_moe * n_experts, d_ffe, d)

    bm = _pick(mp, (1056, 768, 512, 384, 256, 128, 64, 32, 16))
    bm_half = _pick(mp, (528, 384, 256, 128, 64, 32, 16))
    tile = lambda n, cands: _pick(n, cands)

    for l in range(depth):
        proj = _matmul(hb, [w_in], l, bm=bm, bn=tile(p_in, (1024, 512, 256, 128)),
                       out_dtype=BF16, name="in_proj")
        yr = _retention(proj, cos, sin, log_gamma, ret_gn3, l, batch=batch, tp=tp,
                        heads=ret_heads, off_q=off["rq"], off_k=off["rk"], off_v=off["rv"],
                        off_g=off["rg"])
        yh = _hgrn2(proj, hg_lb_logits.astype(F32), hg_gn3, l, batch=batch, tp=tp,
                    heads=hg_heads, off_q=off["hq"], off_f=off["hf"], off_i=off["hi"],
                    off_g=off["hg"])
        bn_o = tile(d, (512, 256, 128))
        t_r = _matmul(yr, [w_ret_out], l, bm=bm_half, bn=bn_o, out_dtype=F32,
                      gate=(proj, off["ga"]), name="ret_out_proj")
        merged = _matmul(yh, [w_hg_out], l, bm=bm, bn=bn_o, out_dtype=BF16,
                         gate=(proj, off["gb"]), add=t_r, name="hg_out_proj")
        m = _matmul(merged, [w_o], l, bm=bm, bn=bn_o, out_dtype=F32, name="mix_out_proj")
        h, hb = _add_ln(h, m, ln1_g3, ln1_b3, l, alpha=alpha, tp=tp, pad=pad)
        if l % 2 == 0:
            d_ff = ffn_w_gate.shape[2]
            act = _matmul(hb, [ffn_w_gate, ffn_w_up], l // 2, bm=bm,
                          bn=tile(d_ff, (512, 256, 128)), out_dtype=BF16, swiglu=True,
                          name="ffn_up")
            f = _matmul(act, [ffn_w_down], l // 2, bm=bm_half, bn=tile(d, (256, 128)),
                        out_dtype=F32, name="ffn_down")
            h, hb = _add_ln(h, f, ln2_g3, ln2_b3, l, alpha=alpha, tp=tp, pad=pad)
        else:
            idx, wts = _router(h, router_w, l // 2, n_experts, tp)
            src, pos, tile_expert, tile_valid, n_rows = _route(idx, n_experts=n_experts, tp=tp,
                                                               pad=pad)
            xs = _gather_rows(h, src, tile_valid, n_rows=n_rows)
            wbase = (l // 2) * n_experts
            act = _grouped_matmul(xs, [moe_wg, moe_wu], tile_expert, tile_valid, wbase,
                                  bn=tile(d_ffe, (256, 128)), out_dtype=BF16, swiglu=True,
                                  name="moe_up")
            ys = _grouped_matmul(act, [moe_wd], tile_expert, tile_valid, wbase,
                                 bn=tile(d, (512, 256, 128)), out_dtype=F32, swiglu=False,
                                 name="moe_down")
            h, hb = _combine_ln(h, ys, pos, wts, ln2_g3, ln2_b3, l, alpha=alpha, tp=tp, pad=pad)

    out = h.reshape(batch, tp, d)[:, pad + n_meta:, :]
    return out.astype(x.dtype)
```

```python
import functools

import jax
import jax.numpy as jnp
from jax import lax
from jax.experimental import pallas as pl
from jax.experimental.pallas import tpu as pltpu

F32 = jnp.float32
BF16 = jnp.bfloat16

RET_DK = 256
RET_DV = 512
RET_CHUNK = 128
HG_DK = 128
HG_DV = 128
HG_CHUNK = 64
TOP_K = 2
ROPE_BASE = 10000.0
LN_EPS = 1e-5
ROUTER_LANES = 128
VMEM_LIMIT_CAP_MB = 56


def _pick(n, candidates):
    for c in candidates:
        if n % c == 0:
            return c
    raise ValueError(f"no tile in {candidates} divides {n}")


def _params(n_axes, vmem_mb):
    return pltpu.CompilerParams(
        dimension_semantics=("arbitrary",) * n_axes,
        vmem_limit_bytes=min(vmem_mb, VMEM_LIMIT_CAP_MB) * 1024 * 1024)


def _mm_body(*refs, n_w, swiglu, has_gate, has_add, scale_lane):
    it = iter(refs)
    x_ref = next(it)
    w_refs = [next(it) for _ in range(n_w)]
    gate_ref = next(it) if has_gate else None
    add_ref = next(it) if has_add else None
    scale_ref = next(it) if scale_lane is not None else None
    o_ref = next(it)
    wb_refs = [next(it) for _ in range(n_w)]

    @pl.when(pl.program_id(1) == 0)
    def _():
        for w_ref, wb_ref in zip(w_refs, wb_refs):
            wb_ref[...] = w_ref[...].astype(BF16)

    x = x_ref[...]
    y = jnp.dot(x, wb_refs[0][...], preferred_element_type=F32)
    if swiglu:
        u = jnp.dot(x, wb_refs[1][...], preferred_element_type=F32)
        y = (y * jax.nn.sigmoid(y)) * u
    if has_gate:
        y = jax.nn.sigmoid(gate_ref[...].astype(F32)) * y
    if scale_lane is not None:
        g = scale_ref[...]
        lane = lax.broadcasted_iota(jnp.int32, g.shape, 1)
        y = jnp.sum(jnp.where(lane == scale_lane, g, 0.0), axis=-1, keepdims=True) * y
    if has_add:
        y = add_ref[...].astype(F32) + y
    o_ref[...] = y.astype(o_ref.dtype)


def _matmul(x, ws, widx, *, bm, bn, out_dtype, swiglu=False, gate=None, add=None,
            scale=None, name):
    m, k = x.shape
    n = ws[0].shape[2]
    assert m % bm == 0 and n % bn == 0, (m, bm, n, bn)
    in_specs = [pl.BlockSpec((bm, k), lambda j, i: (i, 0))]
    args = [x]
    for w in ws:
        in_specs.append(pl.BlockSpec((None, k, bn), lambda j, i: (widx, 0, j)))
        args.append(w)
    if gate is not None:
        garr, goff = gate
        assert goff % bn == 0
        gblk = goff // bn
        in_specs.append(pl.BlockSpec((bm, bn), lambda j, i: (i, gblk + j)))
        args.append(garr)
    if add is not None:
        in_specs.append(pl.BlockSpec((bm, bn), lambda j, i: (i, j)))
        args.append(add)
    scale_lane = None
    if scale is not None:
        sarr, scale_lane = scale
        in_specs.append(pl.BlockSpec((bm, sarr.shape[1]), lambda j, i: (i, 0)))
        args.append(sarr)
    out_bytes = jnp.dtype(out_dtype).itemsize
    vmem = (2 * bm * k * 2 + len(ws) * k * bn * (2 * 4 + 2) + 2 * bm * bn * out_bytes
            + (2 * bm * bn * 2 if gate is not None else 0)
            + (2 * bm * bn * 4 if add is not None else 0)
            + 6 * bm * bn * 4)
    body = functools.partial(_mm_body, n_w=len(ws), swiglu=swiglu, has_gate=gate is not None,
                             has_add=add is not None, scale_lane=scale_lane)
    return pl.pallas_call(
        body,
        grid=(n // bn, m // bm),
        in_specs=in_specs,
        out_specs=pl.BlockSpec((bm, bn), lambda j, i: (i, j)),
        out_shape=jax.ShapeDtypeStruct((m, n), out_dtype),
        scratch_shapes=[pltpu.VMEM((k, bn), BF16) for _ in ws],
        compiler_params=_params(2, vmem // (1024 * 1024) + 4),
        name=name,
    )(*args)


def _ln_body(h_ref, m_ref, g_ref, b_ref, o_ref, ob_ref, *, alpha, rows, tp, pad):
    z = alpha * h_ref[...] + m_ref[...]
    mu = jnp.mean(z, axis=-1, keepdims=True)
    zc = z - mu
    var = jnp.mean(zc * zc, axis=-1, keepdims=True)
    y = zc * lax.rsqrt(var + LN_EPS) * g_ref[...] + b_ref[...]
    blk = lax.rem(pl.program_id(0), tp // rows)
    r = blk * rows + lax.broadcasted_iota(jnp.int32, (rows, 1), 0)
    y = jnp.where(r >= pad, y, 0.0)
    o_ref[...] = y
    ob_ref[...] = y.astype(BF16)


def _add_ln(h, m, g, b, layer, *, alpha, tp, pad):
    mp, d = h.shape
    rows = _pick(tp, (528, 384, 256, 128, 64, 32, 16))
    body = functools.partial(_ln_body, alpha=alpha, rows=rows, tp=tp, pad=pad)
    row_spec = pl.BlockSpec((rows, d), lambda i: (i, 0))
    par_spec = pl.BlockSpec((None, 1, d), lambda i: (layer, 0, 0))
    return pl.pallas_call(
        body,
        grid=(mp // rows,),
        in_specs=[row_spec, row_spec, par_spec, par_spec],
        out_specs=[row_spec, row_spec],
        out_shape=[jax.ShapeDtypeStruct((mp, d), F32), jax.ShapeDtypeStruct((mp, d), BF16)],
        compiler_params=_params(1, 40),
        name="add_layernorm",
    )(h, m, g, b)


def _ret_body(lg_ref, q_ref, k_ref, v_ref, g_ref, cos_ref, sin_ref, gn_ref, o_ref, r_ref,
              *, n_chunks, hpg):
    c_len = RET_CHUNK
    half = RET_DK // 2
    hs = range(hpg)
    ksl = [slice(i * RET_DK, (i + 1) * RET_DK) for i in hs]
    vsl = [slice(i * RET_DV, (i + 1) * RET_DV) for i in hs]

    @pl.when(pl.program_id(2) == 0)
    def _():
        r_ref[...] = jnp.zeros_like(r_ref)

    t_col = lax.broadcasted_iota(jnp.int32, (c_len, 1), 0).astype(F32)
    rel = (lax.broadcasted_iota(jnp.int32, (c_len, c_len), 0)
           - lax.broadcasted_iota(jnp.int32, (c_len, c_len), 1))
    rel_f = jnp.maximum(rel, 0).astype(F32)
    lg = [lg_ref[pl.program_id(1) * hpg + i] for i in hs]
    intra = [jnp.where(rel >= 0, jnp.exp(rel_f * x), 0.0) for x in lg]
    q_decay = [jnp.exp((t_col + 1.0) * x) * (RET_DK ** -0.5) for x in lg]
    k_decay = [jnp.exp((c_len - 1.0 - t_col) * x) for x in lg]
    c_decay = [jnp.exp(jnp.full((1, 1), float(c_len), F32) * x) for x in lg]
    gn = gn_ref[...]

    def rot(x, cos, sin):
        x1, x2 = x[:, :half], x[:, half:]
        return jnp.concatenate([x1 * cos - x2 * sin, x1 * sin + x2 * cos], axis=-1)

    def chunk(c, carry):
        rows = pl.ds(pl.multiple_of(c * c_len, c_len), c_len)
        cos = cos_ref[rows, :]
        sin = sin_ref[rows, :]
        q = [rot(q_ref[rows, ksl[i]].astype(F32), cos, sin) for i in hs]
        k = [rot(k_ref[rows, ksl[i]].astype(F32), cos, sin) for i in hs]
        v = [v_ref[rows, vsl[i]] for i in hs]
        s = [lax.dot_general((q[i] * (RET_DK ** -0.5)).astype(BF16), k[i].astype(BF16),
                             (((1,), (1,)), ((), ())), preferred_element_type=F32) * intra[i]
             for i in hs]
        r_old = [r_ref[i] for i in hs]
        o = [jnp.dot(s[i].astype(BF16), v[i], preferred_element_type=F32)
             + jnp.dot((q[i] * q_decay[i]).astype(BF16), r_old[i].astype(BF16),
                       preferred_element_type=F32) for i in hs]
        for i in hs:
            r_ref[i] = r_old[i] * c_decay[i] + lax.dot_general(
                (k[i] * k_decay[i]).astype(BF16), v[i], (((0,), (0,)), ((), ())),
                preferred_element_type=F32)
        for i in hs:
            mu = jnp.mean(o[i], axis=-1, keepdims=True)
            oc = o[i] - mu
            var = jnp.mean(oc * oc, axis=-1, keepdims=True)
            gate = g_ref[rows, vsl[i]].astype(F32)
            y = oc * lax.rsqrt(var + LN_EPS) * gn[:, vsl[i]] * (gate * jax.nn.sigmoid(gate))
            o_ref[rows, vsl[i]] = y.astype(o_ref.dtype)
        return carry

    lax.fori_loop(0, n_chunks, chunk, 0)


def _retention(proj, cos, sin, log_gamma, gn, layer, *, batch, tp, heads, off_q, off_k, off_v,
               off_g):
    mp = proj.shape[0]
    n_chunks_total = tp // RET_CHUNK
    hpg = _pick(heads, (4, 2, 1))
    cpb = _pick(n_chunks_total, (3, 2, 1))
    tb = cpb * RET_CHUNK
    nblk = tp // tb
    gk, gv = hpg * RET_DK, hpg * RET_DV
    assert off_q % gk == 0 and off_k % gk == 0 and off_v % gv == 0 and off_g % gv == 0
    qb, kb, vb, gb = off_q // gk, off_k // gk, off_v // gv, off_g // gv
    row = lambda b, h, j: b * nblk + j
    body = functools.partial(_ret_body, n_chunks=cpb, hpg=hpg)
    return pl.pallas_call(
        body,
        grid=(batch, heads // hpg, nblk),
        in_specs=[
            pl.BlockSpec(memory_space=pltpu.SMEM),
            pl.BlockSpec((tb, gk), lambda b, h, j: (row(b, h, j), qb + h)),
            pl.BlockSpec((tb, gk), lambda b, h, j: (row(b, h, j), kb + h)),
            pl.BlockSpec((tb, gv), lambda b, h, j: (row(b, h, j), vb + h)),
            pl.BlockSpec((tb, gv), lambda b, h, j: (row(b, h, j), gb + h)),
            pl.BlockSpec((tb, RET_DK // 2), lambda b, h, j: (j, 0)),
            pl.BlockSpec((tb, RET_DK // 2), lambda b, h, j: (j, 0)),
            pl.BlockSpec((None, 1, gv), lambda b, h, j: (layer, 0, h)),
        ],
        out_specs=pl.BlockSpec((tb, gv), lambda b, h, j: (row(b, h, j), h)),
        out_shape=jax.ShapeDtypeStruct((mp, heads * RET_DV), BF16),
        scratch_shapes=[pltpu.VMEM((hpg, RET_DK, RET_DV), F32)],
        compiler_params=_params(3, 40),
        name="retention",
    )(log_gamma, proj, proj, proj, proj, cos, sin, gn)


HG_SAFE_RANGE = 85.0


def _hgrn_body(lbl_ref, gn_ref, q_ref, f_ref, i_ref, g_ref, o_ref, s_ref, acc_ref, bx_ref,
               qx_ref, kx_ref, vx_ref, *, layer, n_chunks, heads):
    c_len = HG_CHUNK

    @pl.when(pl.program_id(1) == 0)
    def _():
        s_ref[...] = jnp.zeros_like(s_ref)

    logits = lbl_ref[...]
    e = jnp.exp(logits - jnp.max(logits, axis=0, keepdims=True))
    sm = e / jnp.sum(e, axis=0, keepdims=True)
    lb = jnp.zeros_like(sm[0:1, :])
    for j in range(1, layer + 1):
        lb = lb + sm[j:j + 1, :]
    gn = gn_ref[...]
    tri = (lax.broadcasted_iota(jnp.int32, (c_len, c_len), 0)
           >= lax.broadcasted_iota(jnp.int32, (c_len, c_len), 1)).astype(BF16)
    t_idx = lax.broadcasted_iota(jnp.int32, (c_len, 1), 0)
    causal = (lax.broadcasted_iota(jnp.int32, (c_len, c_len), 0)
              >= lax.broadcasted_iota(jnp.int32, (c_len, c_len), 1))

    def chunk(c, carry):
        rows = pl.ds(pl.multiple_of(c * c_len, c_len), c_len)

        def head_terms(h):
            ks = slice(h * HG_DK, (h + 1) * HG_DK)
            hq = q_ref[rows, ks].astype(F32)
            q = hq * jax.nn.sigmoid(hq)
            f = lb[:, ks] + (1.0 - lb[:, ks]) * jax.nn.sigmoid(f_ref[rows, ks].astype(F32))
            logf = jnp.log(f)
            p0 = logf.astype(BF16)
            r1 = logf - p0.astype(F32)
            p1 = r1.astype(BF16)
            p2 = (r1 - p1.astype(F32)).astype(BF16)
            b = (jnp.dot(tri, p0, preferred_element_type=F32)
                 + jnp.dot(tri, p1, preferred_element_type=F32)
                 + jnp.dot(tri, p2, preferred_element_type=F32))
            return q, 1.0 - f, b

        def write_out(h, o):
            vs = slice(h * HG_DV, (h + 1) * HG_DV)
            gate = g_ref[rows, vs].astype(F32)
            gate = gate * jax.nn.sigmoid(gate)
            y = o * lax.rsqrt(jnp.mean(o * o, axis=-1, keepdims=True) + LN_EPS)
            o_ref[rows, vs] = (y * gn[:, vs] * gate).astype(o_ref.dtype)

        hs = range(heads)
        vsl = [slice(h * HG_DV, (h + 1) * HG_DV) for h in hs]
        terms = [head_terms(h) for h in hs]
        b_last = [t[2][c_len - 1:c_len, :] for t in terms]
        q_in = [t[0] * jnp.exp(t[2]) for t in terms]
        k_out = [t[1] * jnp.exp(bl - t[2]) for t, bl in zip(terms, b_last)]
        un_mid = [jnp.exp(-0.5 * bl) for bl in b_last]
        q_mid = [jnp.where(bl < -HG_SAFE_RANGE, 0.0, x * u).astype(BF16)
                 for x, u, bl in zip(q_in, un_mid, b_last)]
        k_mid = [jnp.where(bl < -HG_SAFE_RANGE, 0.0, x * u).astype(BF16)
                 for x, u, bl in zip(k_out, un_mid, b_last)]
        sc = [lax.dot_general(qm, km, (((1,), (1,)), ((), ())), preferred_element_type=F32)
              for qm, km in zip(q_mid, k_mid)]
        a = [jnp.where(causal, x, 0.0).astype(BF16) for x in sc]
        vh = [i_ref[rows, vs] for vs in vsl]
        st = [s_ref[h] for h in hs]
        o = [jnp.dot(a[h], vh[h], preferred_element_type=F32)
             + lax.dot_general(q_in[h].astype(BF16), st[h].astype(BF16), (((1,), (1,)), ((), ())),
                               preferred_element_type=F32) for h in hs]
        for h in hs:
            s_ref[h] = st[h] * jnp.exp(b_last[h]) + lax.dot_general(
                vh[h], k_out[h].astype(BF16), (((0,), (0,)), ((), ())),
                preferred_element_type=F32)
        b_min = b_last[0]
        for h in hs:
            acc_ref[:, vsl[h]] = o[h]
            write_out(h, o[h])
            b_min = jnp.minimum(b_min, b_last[h])

        @pl.when(jnp.min(b_min) < -HG_SAFE_RANGE)
        def _():
            for h in range(heads):
                ks = slice(h * HG_DK, (h + 1) * HG_DK)
                vs = slice(h * HG_DV, (h + 1) * HG_DV)
                q, k, b = head_terms(h)
                bx_ref[:, ks] = b
                qx_ref[:, ks] = jnp.where(b[c_len - 1:c_len, :] < -HG_SAFE_RANGE, q, 0.0)
                kx_ref[:, ks] = k
                vx_ref[:, vs] = i_ref[rows, vs].astype(F32)

            def pair(s, cc):
                bs = bx_ref[pl.ds(s, 1), :]
                d = jnp.where(t_idx >= s, jnp.exp(jnp.minimum(bx_ref[...] - bs, 0.0)), 0.0)
                p = qx_ref[...] * d * kx_ref[pl.ds(s, 1), :]
                v_s = vx_ref[pl.ds(s, 1), :]
                for h in range(heads):
                    ks = slice(h * HG_DK, (h + 1) * HG_DK)
                    vs = slice(h * HG_DV, (h + 1) * HG_DV)
                    acc_ref[:, vs] += jnp.sum(p[:, ks], axis=-1, keepdims=True) * v_s[:, vs]
                return cc
            lax.fori_loop(0, c_len, pair, 0)
            for h in range(heads):
                write_out(h, acc_ref[:, h * HG_DV:(h + 1) * HG_DV])
        return carry

    lax.fori_loop(0, n_chunks, chunk, 0)


def _hgrn2(proj, lb_logits, gn, layer, *, batch, tp, heads, off_q, off_f, off_i, off_g):
    mp = proj.shape[0]
    hk, hv = heads * HG_DK, heads * HG_DV
    rb = _pick(tp, (704, 384, 256, 128, 64))
    nblk = tp // rb
    depth = lb_logits.shape[0]
    body = functools.partial(_hgrn_body, layer=layer, n_chunks=rb // HG_CHUNK, heads=heads)
    slab = pltpu.VMEM((HG_CHUNK, hv), F32)
    row = lambda b, j: b * nblk + j
    return pl.pallas_call(
        body,
        grid=(batch, nblk),
        in_specs=[
            pl.BlockSpec((depth, hk), lambda b, j: (0, 0)),
            pl.BlockSpec((None, 1, hv), lambda b, j: (layer, 0, 0)),
            pl.BlockSpec((rb, hk), lambda b, j: (row(b, j), off_q // hk)),
            pl.BlockSpec((rb, hk), lambda b, j: (row(b, j), off_f // hk)),
            pl.BlockSpec((rb, hv), lambda b, j: (row(b, j), off_i // hv)),
            pl.BlockSpec((rb, hv), lambda b, j: (row(b, j), off_g // hv)),
        ],
        out_specs=pl.BlockSpec((rb, hv), lambda b, j: (row(b, j), 0)),
        out_shape=jax.ShapeDtypeStruct((mp, hv), BF16),
        scratch_shapes=[pltpu.VMEM((heads, HG_DV, HG_DK), F32), slab, slab, slab, slab, slab],
        compiler_params=_params(2, 48),
        name="hgrn2",
    )(lb_logits, gn, proj, proj, proj, proj)


def _router_body(x_ref, w_ref, idx_ref, wts_ref, *, n_experts):
    x = x_ref[...]
    w = w_ref[...]
    xh = x.astype(BF16)
    xl = (x - xh.astype(F32)).astype(BF16)
    wh = w.astype(BF16)
    wl = (w - wh.astype(F32)).astype(BF16)
    logits = (jnp.dot(xh, wh, preferred_element_type=F32)
              + jnp.dot(xh, wl, preferred_element_type=F32)
              + jnp.dot(xl, wh, preferred_element_type=F32))
    lane = lax.broadcasted_iota(jnp.int32, logits.shape, 1)
    neg = jnp.float32(-jnp.inf)
    logits = jnp.where(lane < n_experts, logits, neg)
    v1 = jnp.max(logits, axis=-1, keepdims=True)
    i1 = jnp.min(jnp.where(logits == v1, lane, ROUTER_LANES), axis=-1, keepdims=True)
    rest = jnp.where(lane == i1, neg, logits)
    v2 = jnp.max(rest, axis=-1, keepdims=True)
    i2 = jnp.min(jnp.where(rest == v2, lane, ROUTER_LANES), axis=-1, keepdims=True)
    e2 = jnp.exp(v2 - v1)
    w1 = 1.0 / (1.0 + e2)
    w2 = e2 / (1.0 + e2)
    idx_ref[...] = jnp.where(lane == 0, i1, jnp.where(lane == 1, i2, 0))
    wts_ref[...] = jnp.where(lane == 0, w1, jnp.where(lane == 1, w2, 0.0))


def _router(h, w_router_padded, layer_idx, n_experts, tp):
    mp, d = h.shape
    rows = _pick(tp, (528, 384, 256, 128, 64, 32, 16))
    body = functools.partial(_router_body, n_experts=n_experts)
    out_spec = pl.BlockSpec((rows, ROUTER_LANES), lambda i: (i, 0))
    return pl.pallas_call(
        body,
        grid=(mp // rows,),
        in_specs=[pl.BlockSpec((rows, d), lambda i: (i, 0)),
                  pl.BlockSpec((None, d, ROUTER_LANES), lambda i: (layer_idx, 0, 0))],
        out_specs=[out_spec, out_spec],
        out_shape=[jax.ShapeDtypeStruct((mp, ROUTER_LANES), jnp.int32),
                   jax.ShapeDtypeStruct((mp, ROUTER_LANES), F32)],
        compiler_params=_params(1, 40),
        name="moe_router",
    )(h, w_router_padded)


MOE_TM = 512
GATHER_ROWS = 256


def _gather_body(src_ref, tv_ref, x_hbm, o_ref, buf, sem, *, rows, tiles_per_step_inv):
    step = pl.program_id(0)
    base = step * rows
    valid = tv_ref[step // tiles_per_step_inv] == 1

    @pl.when(valid)
    def _():
        def issue(r, c):
            t = src_ref[base + r]
            pltpu.make_async_copy(x_hbm.at[pl.ds(t, 1), :], buf.at[pl.ds(r, 1), :], sem).start()
            return c
        lax.fori_loop(0, rows, issue, 0, unroll=8)
        pltpu.make_async_copy(x_hbm.at[pl.ds(0, rows), :], buf, sem).wait()
        o_ref[...] = buf[...].astype(o_ref.dtype)

    @pl.when(jnp.logical_not(valid))
    def _():
        o_ref[...] = jnp.zeros_like(o_ref)


def _gather_rows(h, src, tile_valid, *, n_rows):
    mp, d = h.shape
    rows = GATHER_ROWS
    assert n_rows % rows == 0 and MOE_TM % rows == 0
    body = functools.partial(_gather_body, rows=rows, tiles_per_step_inv=MOE_TM // rows)
    return pl.pallas_call(
        body,
        grid_spec=pltpu.PrefetchScalarGridSpec(
            num_scalar_prefetch=2,
            grid=(n_rows // rows,),
            in_specs=[pl.BlockSpec(memory_space=pl.ANY)],
            out_specs=pl.BlockSpec((rows, d), lambda i, src, tv: (i, 0)),
            scratch_shapes=[pltpu.VMEM((rows, d), F32), pltpu.SemaphoreType.DMA(())],
        ),
        out_shape=jax.ShapeDtypeStruct((n_rows, d), BF16),
        compiler_params=_params(1, 32),
        name="moe_gather",
    )(src, tile_valid, h)


def _gmm_body(te_ref, tv_ref, *refs, n_w, swiglu):
    it = iter(refs)
    x_ref = next(it)
    w_refs = [next(it) for _ in range(n_w)]
    o_ref = next(it)
    wb_refs = [next(it) for _ in range(n_w)]
    i = pl.program_id(1)
    fresh = jnp.logical_or(i == 0, te_ref[i] != te_ref[jnp.maximum(i - 1, 0)])

    @pl.when(fresh)
    def _():
        for w_ref, wb_ref in zip(w_refs, wb_refs):
            wb_ref[...] = w_ref[...].astype(BF16)

    @pl.when(tv_ref[i] == 1)
    def _():
        x = x_ref[...]
        y = jnp.dot(x, wb_refs[0][...], preferred_element_type=F32)
        if swiglu:
            u = jnp.dot(x, wb_refs[1][...], preferred_element_type=F32)
            y = (y * jax.nn.sigmoid(y)) * u
        o_ref[...] = y.astype(o_ref.dtype)

    @pl.when(tv_ref[i] == 0)
    def _():
        o_ref[...] = jnp.zeros_like(o_ref)


def _grouped_matmul(x, ws, tile_expert, tile_valid, wbase, *, bn, out_dtype, swiglu, name):
    m, k = x.shape
    n = ws[0].shape[2]
    tm = MOE_TM
    assert m % tm == 0 and n % bn == 0
    in_specs = [pl.BlockSpec((tm, k), lambda j, i, te, tv: (i, 0))]
    for _ in ws:
        in_specs.append(pl.BlockSpec((None, k, bn), lambda j, i, te, tv: (wbase + te[i], 0, j)))
    out_bytes = jnp.dtype(out_dtype).itemsize
    vmem = (2 * tm * k * 2 + len(ws) * k * bn * (2 * 4 + 2) + 2 * tm * bn * out_bytes
            + 6 * tm * bn * 4)
    body = functools.partial(_gmm_body, n_w=len(ws), swiglu=swiglu)
    return pl.pallas_call(
        body,
        grid_spec=pltpu.PrefetchScalarGridSpec(
            num_scalar_prefetch=2,
            grid=(n // bn, m // tm),
            in_specs=in_specs,
            out_specs=pl.BlockSpec((tm, bn), lambda j, i, te, tv: (i, j)),
            scratch_shapes=[pltpu.VMEM((k, bn), BF16) for _ in ws],
        ),
        out_shape=jax.ShapeDtypeStruct((m, n), out_dtype),
        compiler_params=_params(2, vmem // (1024 * 1024) + 4),
        name=name,
    )(tile_expert, tile_valid, x, *ws)


def _combine_ln_body(pos_ref, h_ref, w_ref, g_ref, b_ref, y_hbm, o_ref, ob_ref, buf, sem,
                     *, alpha, rows, tp, pad):
    base = pl.program_id(0) * rows

    def issue(r, c):
        for s in range(TOP_K):
            p = pos_ref[(base + r) * TOP_K + s]
            pltpu.make_async_copy(y_hbm.at[pl.ds(p, 1), :], buf.at[s, pl.ds(r, 1), :], sem).start()
        return c
    lax.fori_loop(0, rows, issue, 0, unroll=4)
    for s in range(TOP_K):
        pltpu.make_async_copy(y_hbm.at[pl.ds(0, rows), :], buf.at[s], sem).wait()

    w = w_ref[...]
    lane = lax.broadcasted_iota(jnp.int32, w.shape, 1)
    f = None
    for s in range(TOP_K):
        ws = jnp.sum(jnp.where(lane == s, w, 0.0), axis=-1, keepdims=True)
        term = ws * buf[s]
        f = term if f is None else f + term
    z = alpha * h_ref[...] + f
    mu = jnp.mean(z, axis=-1, keepdims=True)
    zc = z - mu
    var = jnp.mean(zc * zc, axis=-1, keepdims=True)
    y = zc * lax.rsqrt(var + LN_EPS) * g_ref[...] + b_ref[...]
    blk = lax.rem(pl.program_id(0), tp // rows)
    r = blk * rows + lax.broadcasted_iota(jnp.int32, (rows, 1), 0)
    y = jnp.where(r >= pad, y, 0.0)
    o_ref[...] = y
    ob_ref[...] = y.astype(BF16)


def _combine_ln(h, ys, pos, wts, g, b, layer, *, alpha, tp, pad):
    mp, d = h.shape
    rows = _pick(tp, (384, 256, 128, 64, 32, 16))
    body = functools.partial(_combine_ln_body, alpha=alpha, rows=rows, tp=tp, pad=pad)
    row_spec = pl.BlockSpec((rows, d), lambda i, p: (i, 0))
    par_spec = pl.BlockSpec((None, 1, d), lambda i, p: (layer, 0, 0))
    return pl.pallas_call(
        body,
        grid_spec=pltpu.PrefetchScalarGridSpec(
            num_scalar_prefetch=1,
            grid=(mp // rows,),
            in_specs=[row_spec, pl.BlockSpec((rows, ROUTER_LANES), lambda i, p: (i, 0)),
                      par_spec, par_spec, pl.BlockSpec(memory_space=pl.ANY)],
            out_specs=[row_spec, row_spec],
            scratch_shapes=[pltpu.VMEM((TOP_K, rows, d), F32), pltpu.SemaphoreType.DMA(())],
        ),
        out_shape=[jax.ShapeDtypeStruct((mp, d), F32), jax.ShapeDtypeStruct((mp, d), BF16)],
        compiler_params=_params(1, 40),
        name="moe_combine_layernorm",
    )(pos, h, wts, g, b, ys)


def _route(idx, *, n_experts, tp, pad):
    mp = idx.shape[0]
    tm = MOE_TM
    n_assign = mp * TOP_K
    n_tiles = -(-n_assign // tm) + n_experts
    n_rows = n_tiles * tm
    token = jnp.arange(mp, dtype=jnp.int32)
    real = (token % tp) >= pad
    e = jnp.where(real[:, None], idx[:, :TOP_K], n_experts).reshape(n_assign)
    onehot = (e[:, None] == jnp.arange(n_experts, dtype=jnp.int32)[None, :]).astype(jnp.int32)
    csum = jnp.cumsum(onehot, axis=0)
    counts = csum[-1]
    rank = jnp.sum(onehot * csum, axis=1) - 1
    gsz = ((counts + tm - 1) // tm) * tm
    gend = jnp.cumsum(gsz)
    gstart = gend - gsz
    routed = e < n_experts
    dst = jnp.where(routed, gstart[jnp.minimum(e, n_experts - 1)] + rank, n_rows)
    assign_token = jnp.arange(n_assign, dtype=jnp.int32) // TOP_K
    src = jnp.zeros((n_rows,), jnp.int32).at[dst].set(assign_token, mode="drop")
    pos = jnp.where(routed, dst, 0).astype(jnp.int32)
    tile_start = jnp.arange(n_tiles, dtype=jnp.int32) * tm
    tile_expert = jnp.minimum(jnp.searchsorted(gend, tile_start, side="right"),
                              n_experts - 1).astype(jnp.int32)
    tile_valid = (tile_start < gend[-1]).astype(jnp.int32)
    return src, pos, tile_expert, tile_valid, n_rows


def kernel(x, meta_tokens, w_in, ret_gn_g, hg_norm_g, hg_lb_logits, w_ret_out, w_hg_out, w_o,
           ln1_g, ln1_b, ln2_g, ln2_b, ffn_w_gate, ffn_w_up, ffn_w_down,
           moe_router, moe_w_gate, moe_w_up, moe_w_down):
    batch, seq, d = x.shape
    n_meta = meta_tokens.shape[0]
    depth, _, p_in = w_in.shape
    ret_v = w_ret_out.shape[1]
    hg_v = w_hg_out.shape[1]
    hg_k = hg_lb_logits.shape[1]
    ret_qk = (p_in - 2 * ret_v - 2 * hg_k - 2 * hg_v - 2 * d) // 2
    ret_heads = ret_qk // RET_DK
    hg_heads = hg_k // HG_DK
    n_experts = moe_router.shape[2]
    assert ret_v == ret_heads * RET_DV and hg_v == hg_heads * HG_DV and hg_k == hg_v
    assert n_experts <= ROUTER_LANES

    pad = RET_CHUNK - n_meta
    tp = pad + n_meta + seq
    assert tp % RET_CHUNK == 0 and tp % HG_CHUNK == 0
    mp = batch * tp
    alpha = (2 * depth) ** 0.25

    off = {}
    acc = 0
    for nm, wd in (("rq", ret_qk), ("rk", ret_qk), ("rv", ret_v), ("rg", ret_v), ("hq", hg_k),
                   ("hf", hg_k), ("hi", hg_v), ("hg", hg_v), ("ga", d), ("gb", d)):
        off[nm] = acc
        acc += wd

    meta = jnp.broadcast_to(meta_tokens[None].astype(F32), (batch, n_meta, d))
    h = jnp.concatenate([jnp.zeros((batch, pad, d), F32), meta, x.astype(F32)], axis=1)
    h = h.reshape(mp, d)
    hb = h.astype(BF16)

    pos = jnp.arange(tp, dtype=F32) - float(pad)
    inv = 1.0 / (ROPE_BASE ** jnp.linspace(0.0, 1.0, RET_DK // 2, dtype=F32))
    ang = pos[:, None] * inv[None, :]
    cos, sin = jnp.cos(ang), jnp.sin(ang)
    log_gamma = jnp.log1p(-jnp.exp2(-5.0 - jnp.arange(ret_heads, dtype=F32)))

    ret_gn3 = ret_gn_g.astype(F32).reshape(depth, 1, ret_v)
    hg_gn3 = hg_norm_g.astype(F32).reshape(depth, 1, hg_v)
    ln1_g3, ln1_b3 = ln1_g.reshape(depth, 1, d), ln1_b.reshape(depth, 1, d)
    ln2_g3, ln2_b3 = ln2_g.reshape(depth, 1, d), ln2_b.reshape(depth, 1, d)
    n_moe = moe_router.shape[0]
    d_ffe = moe_w_gate.shape[3]
    router_w = jnp.pad(moe_router.astype(F32), ((0, 0), (0, 0), (0, ROUTER_LANES - n_experts)))
    moe_wg = moe_w_gate.reshape(n_moe * n_experts, d, d_ffe)
    moe_wu = moe_w_up.reshape(n_moe * n_experts, d, d_ffe)
    moe_wd = moe_w_down.reshape(n_moe * n_experts, d_ffe, d)

    bm = _pick(mp, (1056, 768, 512, 384, 256, 128, 64, 32, 16))
    bm_half = _pick(mp, (528, 384, 256, 128, 64, 32, 16))
    tile = lambda n, cands: _pick(n, cands)

    for l in range(depth):
        proj = _matmul(hb, [w_in], l, bm=bm, bn=tile(p_in, (1024, 512, 256, 128)),
                       out_dtype=BF16, name="in_proj")
        yr = _retention(proj, cos, sin, log_gamma, ret_gn3, l, batch=batch, tp=tp,
                        heads=ret_heads, off_q=off["rq"], off_k=off["rk"], off_v=off["rv"],
                        off_g=off["rg"])
        yh = _hgrn2(proj, hg_lb_logits.astype(F32), hg_gn3, l, batch=batch, tp=tp,
                    heads=hg_heads, off_q=off["hq"], off_f=off["hf"], off_i=off["hi"],
                    off_g=off["hg"])
        bn_o = tile(d, (512, 256, 128))
        t_r = _matmul(yr, [w_ret_out], l, bm=bm_half, bn=bn_o, out_dtype=F32,
                      gate=(proj, off["ga"]), name="ret_out_proj")
        merged = _matmul(yh, [w_hg_out], l, bm=bm, bn=bn_o, out_dtype=BF16,
                         gate=(proj, off["gb"]), add=t_r, name="hg_out_proj")
        m = _matmul(merged, [w_o], l, bm=bm, bn=bn_o, out_dtype=F32, name="mix_out_proj")
        h, hb = _add_ln(h, m, ln1_g3, ln1_b3, l, alpha=alpha, tp=tp, pad=pad)
        if l % 2 == 0:
            d_ff = ffn_w_gate.shape[2]
            act = _matmul(hb, [ffn_w_gate, ffn_w_up], l // 2, bm=bm,
                          bn=tile(d_ff, (512, 256, 128)), out_dtype=BF16, swiglu=True,
                          name="ffn_up")
            f = _matmul(act, [ffn_w_down], l // 2, bm=bm_half, bn=tile(d, (256, 128)),
                        out_dtype=F32, name="ffn_down")
            h, hb = _add_ln(h, f, ln2_g3, ln2_b3, l, alpha=alpha, tp=tp, pad=pad)
        else:
            idx, wts = _router(h, router_w, l // 2, n_experts, tp)
            src, pos, tile_expert, tile_valid, n_rows = _route(idx, n_experts=n_experts, tp=tp,
                                                               pad=pad)
            xs = _gather_rows(h, src, tile_valid, n_rows=n_rows)
            wbase = (l // 2) * n_experts
            act = _grouped_matmul(xs, [moe_wg, moe_wu], tile_expert, tile_valid, wbase,
                                  bn=tile(d_ffe, (256, 128)), out_dtype=BF16, swiglu=True,
                                  name="moe_up")
            ys = _grouped_matmul(act, [moe_wd], tile_expert, tile_valid, wbase,
                                 bn=tile(d, (512, 256, 128)), out_dtype=F32, swiglu=False,
                                 name="moe_down")
            h, hb = _combine_ln(h, ys, pos, wts, ln2_g3, ln2_b3, l, alpha=alpha, tp=tp, pad=pad)

    out = h.reshape(batch, tp, d)[:, pad + n_meta:, :]
    return out.astype(x.dtype)
```

```python
import functools

import jax
import jax.numpy as jnp
from jax import lax
from jax.experimental import pallas as pl
from jax.experimental.pallas import tpu as pltpu

F32 = jnp.float32
BF16 = jnp.bfloat16

RET_DK = 256
RET_DV = 512
RET_CHUNK = 128
HG_DK = 128
HG_DV = 128
HG_CHUNK = 64
TOP_K = 2
ROPE_BASE = 10000.0
LN_EPS = 1e-5
ROUTER_LANES = 128
VMEM_LIMIT_CAP_MB = 56


def _pick(n, candidates):
    for c in candidates:
        if n % c == 0:
            return c
    raise ValueError(f"no tile in {candidates} divides {n}")


def _params(n_axes, vmem_mb):
    return pltpu.CompilerParams(
        dimension_semantics=("arbitrary",) * n_axes,
        vmem_limit_bytes=min(vmem_mb, VMEM_LIMIT_CAP_MB) * 1024 * 1024)


def _mm_body(*refs, n_w, swiglu, has_gate, has_add):
    it = iter(refs)
    x_ref = next(it)
    w_refs = [next(it) for _ in range(n_w)]
    gate_ref = next(it) if has_gate else None
    add_ref = next(it) if has_add else None
    o_ref = next(it)
    wb_refs = [next(it) for _ in range(n_w)]

    @pl.when(pl.program_id(1) == 0)
    def _():
        for w_ref, wb_ref in zip(w_refs, wb_refs):
            wb_ref[...] = w_ref[...].astype(BF16)

    x = x_ref[...]
    y = jnp.dot(x, wb_refs[0][...], preferred_element_type=F32)
    if swiglu:
        u = jnp.dot(x, wb_refs[1][...], preferred_element_type=F32)
        y = (y * jax.nn.sigmoid(y)) * u
    if has_gate:
        y = jax.nn.sigmoid(gate_ref[...].astype(F32)) * y
    if has_add:
        y = add_ref[...].astype(F32) + y
    o_ref[...] = y.astype(o_ref.dtype)


def _matmul(x, ws, widx, *, bm, bn, out_dtype, swiglu=False, gate=None, add=None, name):
    m, k = x.shape
    n = ws[0].shape[2]
    assert m % bm == 0 and n % bn == 0, (m, bm, n, bn)
    in_specs = [pl.BlockSpec((bm, k), lambda j, i: (i, 0))]
    args = [x]
    for w in ws:
        in_specs.append(pl.BlockSpec((None, k, bn), lambda j, i: (widx, 0, j)))
        args.append(w)
    if gate is not None:
        garr, goff = gate
        assert goff % bn == 0
        gblk = goff // bn
        in_specs.append(pl.BlockSpec((bm, bn), lambda j, i: (i, gblk + j)))
        args.append(garr)
    if add is not None:
        in_specs.append(pl.BlockSpec((bm, bn), lambda j, i: (i, j)))
        args.append(add)
    out_bytes = jnp.dtype(out_dtype).itemsize
    vmem = (2 * bm * k * 2 + len(ws) * k * bn * (2 * 4 + 2) + 2 * bm * bn * out_bytes
            + (2 * bm * bn * 2 if gate is not None else 0)
            + (2 * bm * bn * 4 if add is not None else 0)
            + 6 * bm * bn * 4)
    body = functools.partial(_mm_body, n_w=len(ws), swiglu=swiglu, has_gate=gate is not None,
                             has_add=add is not None)
    return pl.pallas_call(
        body,
        grid=(n // bn, m // bm),
        in_specs=in_specs,
        out_specs=pl.BlockSpec((bm, bn), lambda j, i: (i, j)),
        out_shape=jax.ShapeDtypeStruct((m, n), out_dtype),
        scratch_shapes=[pltpu.VMEM((k, bn), BF16) for _ in ws],
        compiler_params=_params(2, vmem // (1024 * 1024) + 4),
        name=name,
    )(*args)


def _ln_body(h_ref, m_ref, g_ref, b_ref, o_ref, ob_ref, *, alpha, rows, tp, pad):
    y = _layernorm_rows(alpha * h_ref[...] + m_ref[...], g_ref[...], b_ref[...],
                        rows=rows, tp=tp, pad=pad)
    o_ref[...] = y
    ob_ref[...] = y.astype(BF16)


def _layernorm_rows(z, g, b, *, rows, tp, pad):
    mu = jnp.mean(z, axis=-1, keepdims=True)
    zc = z - mu
    var = jnp.mean(zc * zc, axis=-1, keepdims=True)
    y = zc * lax.rsqrt(var + LN_EPS) * g + b
    blk = lax.rem(pl.program_id(0), tp // rows)
    r = blk * rows + lax.broadcasted_iota(jnp.int32, (rows, 1), 0)
    return jnp.where(r >= pad, y, 0.0)


def _mm_ln_body(x_ref, w_ref, h_ref, g_ref, b_ref, o_ref, ob_ref, wb_ref, *, alpha, rows, tp, pad):
    @pl.when(pl.program_id(0) == 0)
    def _():
        wb_ref[...] = w_ref[...].astype(BF16)

    m = jnp.dot(x_ref[...], wb_ref[...], preferred_element_type=F32)
    y = _layernorm_rows(alpha * h_ref[...] + m, g_ref[...], b_ref[...], rows=rows, tp=tp, pad=pad)
    o_ref[...] = y
    ob_ref[...] = y.astype(BF16)


def _matmul_add_ln(x, w, layer, h, g, b, *, alpha, tp, pad):
    mp, k = x.shape
    d = w.shape[2]
    rows = _pick(tp, (384, 256, 128, 64, 32, 16))
    body = functools.partial(_mm_ln_body, alpha=alpha, rows=rows, tp=tp, pad=pad)
    row_spec = pl.BlockSpec((rows, d), lambda i: (i, 0))
    par_spec = pl.BlockSpec((None, 1, d), lambda i: (layer, 0, 0))
    vmem = k * d * (4 + 2) + 2 * rows * k * 2 + rows * d * (2 * 4 + 2 * 4 + 2 * 2 + 3 * 4)
    return pl.pallas_call(
        body,
        grid=(mp // rows,),
        in_specs=[pl.BlockSpec((rows, k), lambda i: (i, 0)),
                  pl.BlockSpec((None, k, d), lambda i: (layer, 0, 0),
                               pipeline_mode=pl.Buffered(1)),
                  row_spec, par_spec, par_spec],
        out_specs=[row_spec, row_spec],
        out_shape=[jax.ShapeDtypeStruct((mp, d), F32), jax.ShapeDtypeStruct((mp, d), BF16)],
        scratch_shapes=[pltpu.VMEM((k, d), BF16)],
        compiler_params=_params(1, vmem // (1024 * 1024) + 4),
        name="mix_out_proj_layernorm",
    )(x, w, h, g, b)


def _add_ln(h, m, g, b, layer, *, alpha, tp, pad):
    mp, d = h.shape
    rows = _pick(tp, (528, 384, 256, 128, 64, 32, 16))
    body = functools.partial(_ln_body, alpha=alpha, rows=rows, tp=tp, pad=pad)
    row_spec = pl.BlockSpec((rows, d), lambda i: (i, 0))
    par_spec = pl.BlockSpec((None, 1, d), lambda i: (layer, 0, 0))
    return pl.pallas_call(
        body,
        grid=(mp // rows,),
        in_specs=[row_spec, row_spec, par_spec, par_spec],
        out_specs=[row_spec, row_spec],
        out_shape=[jax.ShapeDtypeStruct((mp, d), F32), jax.ShapeDtypeStruct((mp, d), BF16)],
        compiler_params=_params(1, 40),
        name="add_layernorm",
    )(h, m, g, b)


def _ret_body(lg_ref, q_ref, k_ref, v_ref, g_ref, cos_ref, sin_ref, gn_ref, o_ref, r_ref,
              *, n_chunks, hpg):
    c_len = RET_CHUNK
    half = RET_DK // 2
    hs = range(hpg)
    ksl = [slice(i * RET_DK, (i + 1) * RET_DK) for i in hs]
    vsl = [slice(i * RET_DV, (i + 1) * RET_DV) for i in hs]

    @pl.when(pl.program_id(2) == 0)
    def _():
        r_ref[...] = jnp.zeros_like(r_ref)

    t_col = lax.broadcasted_iota(jnp.int32, (c_len, 1), 0).astype(F32)
    rel = (lax.broadcasted_iota(jnp.int32, (c_len, c_len), 0)
           - lax.broadcasted_iota(jnp.int32, (c_len, c_len), 1))
    rel_f = jnp.maximum(rel, 0).astype(F32)
    lg = [lg_ref[pl.program_id(1) * hpg + i] for i in hs]
    intra = [jnp.where(rel >= 0, jnp.exp(rel_f * x), 0.0) for x in lg]
    q_decay = [jnp.exp((t_col + 1.0) * x) * (RET_DK ** -0.5) for x in lg]
    k_decay = [jnp.exp((c_len - 1.0 - t_col) * x) for x in lg]
    c_decay = [jnp.exp(jnp.full((1, 1), float(c_len), F32) * x) for x in lg]
    gn = gn_ref[...]

    def rot(x, cos, sin):
        x1, x2 = x[:, :half], x[:, half:]
        return jnp.concatenate([x1 * cos - x2 * sin, x1 * sin + x2 * cos], axis=-1)

    def chunk(c, carry):
        rows = pl.ds(pl.multiple_of(c * c_len, c_len), c_len)
        cos = cos_ref[rows, :]
        sin = sin_ref[rows, :]
        q = [rot(q_ref[rows, ksl[i]].astype(F32), cos, sin) for i in hs]
        k = [rot(k_ref[rows, ksl[i]].astype(F32), cos, sin) for i in hs]
        v = [v_ref[rows, vsl[i]] for i in hs]
        s = [lax.dot_general((q[i] * (RET_DK ** -0.5)).astype(BF16), k[i].astype(BF16),
                             (((1,), (1,)), ((), ())), preferred_element_type=F32) * intra[i]
             for i in hs]
        r_old = [r_ref[i] for i in hs]
        o = [jnp.dot(s[i].astype(BF16), v[i], preferred_element_type=F32)
             + jnp.dot((q[i] * q_decay[i]).astype(BF16), r_old[i].astype(BF16),
                       preferred_element_type=F32) for i in hs]
        for i in hs:
            r_ref[i] = r_old[i] * c_decay[i] + lax.dot_general(
                (k[i] * k_decay[i]).astype(BF16), v[i], (((0,), (0,)), ((), ())),
                preferred_element_type=F32)
        for i in hs:
            mu = jnp.mean(o[i], axis=-1, keepdims=True)
            oc = o[i] - mu
            var = jnp.mean(oc * oc, axis=-1, keepdims=True)
            gate = g_ref[rows, vsl[i]].astype(F32)
            y = oc * lax.rsqrt(var + LN_EPS) * gn[:, vsl[i]] * (gate * jax.nn.sigmoid(gate))
            o_ref[rows, vsl[i]] = y.astype(o_ref.dtype)
        return carry

    lax.fori_loop(0, n_chunks, chunk, 0)


def _retention(proj, cos, sin, log_gamma, gn, layer, *, batch, tp, heads, off_q, off_k, off_v,
               off_g):
    mp = proj.shape[0]
    n_chunks_total = tp // RET_CHUNK
    hpg = _pick(heads, (4, 2, 1))
    cpb = _pick(n_chunks_total, (3, 2, 1))
    tb = cpb * RET_CHUNK
    nblk = tp // tb
    gk, gv = hpg * RET_DK, hpg * RET_DV
    assert off_q % gk == 0 and off_k % gk == 0 and off_v % gv == 0 and off_g % gv == 0
    qb, kb, vb, gb = off_q // gk, off_k // gk, off_v // gv, off_g // gv
    row = lambda b, h, j: b * nblk + j
    body = functools.partial(_ret_body, n_chunks=cpb, hpg=hpg)
    return pl.pallas_call(
        body,
        grid=(batch, heads // hpg, nblk),
        in_specs=[
            pl.BlockSpec(memory_space=pltpu.SMEM),
            pl.BlockSpec((tb, gk), lambda b, h, j: (row(b, h, j), qb + h)),
            pl.BlockSpec((tb, gk), lambda b, h, j: (row(b, h, j), kb + h)),
            pl.BlockSpec((tb, gv), lambda b, h, j: (row(b, h, j), vb + h)),
            pl.BlockSpec((tb, gv), lambda b, h, j: (row(b, h, j), gb + h)),
            pl.BlockSpec((tb, RET_DK // 2), lambda b, h, j: (j, 0)),
            pl.BlockSpec((tb, RET_DK // 2), lambda b, h, j: (j, 0)),
            pl.BlockSpec((None, 1, gv), lambda b, h, j: (layer, 0, h)),
        ],
        out_specs=pl.BlockSpec((tb, gv), lambda b, h, j: (row(b, h, j), h)),
        out_shape=jax.ShapeDtypeStruct((mp, heads * RET_DV), BF16),
        scratch_shapes=[pltpu.VMEM((hpg, RET_DK, RET_DV), F32)],
        compiler_params=_params(3, 40),
        name="retention",
    )(log_gamma, proj, proj, proj, proj, cos, sin, gn)


HG_SAFE_RANGE = 85.0


def _hgrn_body(lbl_ref, gn_ref, q_ref, f_ref, i_ref, g_ref, o_ref, s_ref, acc_ref, bx_ref,
               qx_ref, kx_ref, vx_ref, *, layer, n_chunks, heads):
    c_len = HG_CHUNK

    @pl.when(pl.program_id(1) == 0)
    def _():
        s_ref[...] = jnp.zeros_like(s_ref)

    logits = lbl_ref[...]
    e = jnp.exp(logits - jnp.max(logits, axis=0, keepdims=True))
    sm = e / jnp.sum(e, axis=0, keepdims=True)
    lb = jnp.zeros_like(sm[0:1, :])
    for j in range(1, layer + 1):
        lb = lb + sm[j:j + 1, :]
    gn = gn_ref[...]
    tri = (lax.broadcasted_iota(jnp.int32, (c_len, c_len), 0)
           >= lax.broadcasted_iota(jnp.int32, (c_len, c_len), 1)).astype(BF16)
    t_idx = lax.broadcasted_iota(jnp.int32, (c_len, 1), 0)
    causal = (lax.broadcasted_iota(jnp.int32, (c_len, c_len), 0)
              >= lax.broadcasted_iota(jnp.int32, (c_len, c_len), 1))

    def chunk(c, carry):
        rows = pl.ds(pl.multiple_of(c * c_len, c_len), c_len)

        def head_terms(h):
            ks = slice(h * HG_DK, (h + 1) * HG_DK)
            hq = q_ref[rows, ks].astype(F32)
            q = hq * jax.nn.sigmoid(hq)
            f = lb[:, ks] + (1.0 - lb[:, ks]) * jax.nn.sigmoid(f_ref[rows, ks].astype(F32))
            logf = jnp.log(f)
            p0 = logf.astype(BF16)
            r1 = logf - p0.astype(F32)
            p1 = r1.astype(BF16)
            p2 = (r1 - p1.astype(F32)).astype(BF16)
            b = (jnp.dot(tri, p0, preferred_element_type=F32)
                 + jnp.dot(tri, p1, preferred_element_type=F32)
                 + jnp.dot(tri, p2, preferred_element_type=F32))
            return q, 1.0 - f, b

        def write_out(h, o):
            vs = slice(h * HG_DV, (h + 1) * HG_DV)
            gate = g_ref[rows, vs].astype(F32)
            gate = gate * jax.nn.sigmoid(gate)
            y = o * lax.rsqrt(jnp.mean(o * o, axis=-1, keepdims=True) + LN_EPS)
            o_ref[rows, vs] = (y * gn[:, vs] * gate).astype(o_ref.dtype)

        hs = range(heads)
        vsl = [slice(h * HG_DV, (h + 1) * HG_DV) for h in hs]
        terms = [head_terms(h) for h in hs]
        b_last = [t[2][c_len - 1:c_len, :] for t in terms]
        q_in = [t[0] * jnp.exp(t[2]) for t in terms]
        k_out = [t[1] * jnp.exp(bl - t[2]) for t, bl in zip(terms, b_last)]
        un_mid = [jnp.exp(-0.5 * bl) for bl in b_last]
        q_mid = [jnp.where(bl < -HG_SAFE_RANGE, 0.0, x * u).astype(BF16)
                 for x, u, bl in zip(q_in, un_mid, b_last)]
        k_mid = [jnp.where(bl < -HG_SAFE_RANGE, 0.0, x * u).astype(BF16)
                 for x, u, bl in zip(k_out, un_mid, b_last)]
        sc = [lax.dot_general(qm, km, (((1,), (1,)), ((), ())), preferred_element_type=F32)
              for qm, km in zip(q_mid, k_mid)]
        a = [jnp.where(causal, x, 0.0).astype(BF16) for x in sc]
        vh = [i_ref[rows, vs] for vs in vsl]
        st = [s_ref[h] for h in hs]
        o = [jnp.dot(a[h], vh[h], preferred_element_type=F32)
             + lax.dot_general(q_in[h].astype(BF16), st[h].astype(BF16), (((1,), (1,)), ((), ())),
                               preferred_element_type=F32) for h in hs]
        for h in hs:
            s_ref[h] = st[h] * jnp.exp(b_last[h]) + lax.dot_general(
                vh[h], k_out[h].astype(BF16), (((0,), (0,)), ((), ())),
                preferred_element_type=F32)
        b_min = b_last[0]
        for h in hs:
            acc_ref[:, vsl[h]] = o[h]
            write_out(h, o[h])
            b_min = jnp.minimum(b_min, b_last[h])

        @pl.when(jnp.min(b_min) < -HG_SAFE_RANGE)
        def _():
            for h in range(heads):
                ks = slice(h * HG_DK, (h + 1) * HG_DK)
                vs = slice(h * HG_DV, (h + 1) * HG_DV)
                q, k, b = head_terms(h)
                bx_ref[:, ks] = b
                qx_ref[:, ks] = jnp.where(b[c_len - 1:c_len, :] < -HG_SAFE_RANGE, q, 0.0)
                kx_ref[:, ks] = k
                vx_ref[:, vs] = i_ref[rows, vs].astype(F32)

            def pair(s, cc):
                bs = bx_ref[pl.ds(s, 1), :]
                d = jnp.where(t_idx >= s, jnp.exp(jnp.minimum(bx_ref[...] - bs, 0.0)), 0.0)
                p = qx_ref[...] * d * kx_ref[pl.ds(s, 1), :]
                v_s = vx_ref[pl.ds(s, 1), :]
                for h in range(heads):
                    ks = slice(h * HG_DK, (h + 1) * HG_DK)
                    vs = slice(h * HG_DV, (h + 1) * HG_DV)
                    acc_ref[:, vs] += jnp.sum(p[:, ks], axis=-1, keepdims=True) * v_s[:, vs]
                return cc
            lax.fori_loop(0, c_len, pair, 0)
            for h in range(heads):
                write_out(h, acc_ref[:, h * HG_DV:(h + 1) * HG_DV])
        return carry

    lax.fori_loop(0, n_chunks, chunk, 0)


def _hgrn2(proj, lb_logits, gn, layer, *, batch, tp, heads, off_q, off_f, off_i, off_g):
    mp = proj.shape[0]
    hk, hv = heads * HG_DK, heads * HG_DV
    rb = _pick(tp, (704, 384, 256, 128, 64))
    nblk = tp // rb
    depth = lb_logits.shape[0]
    body = functools.partial(_hgrn_body, layer=layer, n_chunks=rb // HG_CHUNK, heads=heads)
    slab = pltpu.VMEM((HG_CHUNK, hv), F32)
    row = lambda b, j: b * nblk + j
    return pl.pallas_call(
        body,
        grid=(batch, nblk),
        in_specs=[
            pl.BlockSpec((depth, hk), lambda b, j: (0, 0)),
            pl.BlockSpec((None, 1, hv), lambda b, j: (layer, 0, 0)),
            pl.BlockSpec((rb, hk), lambda b, j: (row(b, j), off_q // hk)),
            pl.BlockSpec((rb, hk), lambda b, j: (row(b, j), off_f // hk)),
            pl.BlockSpec((rb, hv), lambda b, j: (row(b, j), off_i // hv)),
            pl.BlockSpec((rb, hv), lambda b, j: (row(b, j), off_g // hv)),
        ],
        out_specs=pl.BlockSpec((rb, hv), lambda b, j: (row(b, j), 0)),
        out_shape=jax.ShapeDtypeStruct((mp, hv), BF16),
        scratch_shapes=[pltpu.VMEM((heads, HG_DV, HG_DK), F32), slab, slab, slab, slab, slab],
        compiler_params=_params(2, 48),
        name="hgrn2",
    )(lb_logits, gn, proj, proj, proj, proj)


def _router_body(x_ref, w_ref, idx_ref, wts_ref, *, n_experts):
    x = x_ref[...]
    w = w_ref[...]
    xh = x.astype(BF16)
    xl = (x - xh.astype(F32)).astype(BF16)
    wh = w.astype(BF16)
    wl = (w - wh.astype(F32)).astype(BF16)
    logits = (jnp.dot(xh, wh, preferred_element_type=F32)
              + jnp.dot(xh, wl, preferred_element_type=F32)
              + jnp.dot(xl, wh, preferred_element_type=F32))
    lane = lax.broadcasted_iota(jnp.int32, logits.shape, 1)
    neg = jnp.float32(-jnp.inf)
    logits = jnp.where(lane < n_experts, logits, neg)
    v1 = jnp.max(logits, axis=-1, keepdims=True)
    i1 = jnp.min(jnp.where(logits == v1, lane, ROUTER_LANES), axis=-1, keepdims=True)
    rest = jnp.where(lane == i1, neg, logits)
    v2 = jnp.max(rest, axis=-1, keepdims=True)
    i2 = jnp.min(jnp.where(rest == v2, lane, ROUTER_LANES), axis=-1, keepdims=True)
    e2 = jnp.exp(v2 - v1)
    w1 = 1.0 / (1.0 + e2)
    w2 = e2 / (1.0 + e2)
    idx_ref[...] = jnp.where(lane == 0, i1, jnp.where(lane == 1, i2, 0))
    wts_ref[...] = jnp.where(lane == 0, w1, jnp.where(lane == 1, w2, 0.0))


def _router(h, w_router_padded, layer_idx, n_experts, tp):
    mp, d = h.shape
    rows = _pick(tp, (528, 384, 256, 128, 64, 32, 16))
    body = functools.partial(_router_body, n_experts=n_experts)
    out_spec = pl.BlockSpec((rows, ROUTER_LANES), lambda i: (i, 0))
    return pl.pallas_call(
        body,
        grid=(mp // rows,),
        in_specs=[pl.BlockSpec((rows, d), lambda i: (i, 0)),
                  pl.BlockSpec((None, d, ROUTER_LANES), lambda i: (layer_idx, 0, 0))],
        out_specs=[out_spec, out_spec],
        out_shape=[jax.ShapeDtypeStruct((mp, ROUTER_LANES), jnp.int32),
                   jax.ShapeDtypeStruct((mp, ROUTER_LANES), F32)],
        compiler_params=_params(1, 40),
        name="moe_router",
    )(h, w_router_padded)


MOE_TM = 512
GATHER_ROWS = 256


def _gather_body(src_ref, tv_ref, x_hbm, o_ref, buf, sem, *, rows, steps_per_tile):
    step = pl.program_id(0)

    def start_rows(st, slot):
        @pl.when(tv_ref[st // steps_per_tile] == 1)
        def _():
            def issue(r, c):
                t = src_ref[st * rows + r]
                pltpu.make_async_copy(x_hbm.at[pl.ds(t, 1), :], buf.at[slot, pl.ds(r, 1), :],
                                      sem.at[slot]).start()
                return c
            lax.fori_loop(0, rows, issue, 0, unroll=8)

    @pl.when(step == 0)
    def _():
        start_rows(step, 0)

    @pl.when(step + 1 < pl.num_programs(0))
    def _():
        start_rows(step + 1, lax.rem(step + 1, 2))

    slot = lax.rem(step, 2)
    valid = tv_ref[step // steps_per_tile] == 1

    @pl.when(valid)
    def _():
        pltpu.make_async_copy(x_hbm.at[pl.ds(0, rows), :], buf.at[slot], sem.at[slot]).wait()
        o_ref[...] = buf[slot].astype(o_ref.dtype)

    @pl.when(jnp.logical_not(valid))
    def _():
        o_ref[...] = jnp.zeros_like(o_ref)


def _gather_rows(h, src, tile_valid, *, n_rows):
    mp, d = h.shape
    rows = GATHER_ROWS
    assert n_rows % rows == 0 and MOE_TM % rows == 0
    body = functools.partial(_gather_body, rows=rows, steps_per_tile=MOE_TM // rows)
    return pl.pallas_call(
        body,
        grid_spec=pltpu.PrefetchScalarGridSpec(
            num_scalar_prefetch=2,
            grid=(n_rows // rows,),
            in_specs=[pl.BlockSpec(memory_space=pl.ANY)],
            out_specs=pl.BlockSpec((rows, d), lambda i, src, tv: (i, 0)),
            scratch_shapes=[pltpu.VMEM((2, rows, d), F32), pltpu.SemaphoreType.DMA((2,))],
        ),
        out_shape=jax.ShapeDtypeStruct((n_rows, d), BF16),
        compiler_params=_params(1, 32),
        name="moe_gather",
    )(src, tile_valid, h)


def _gmm_body(te_ref, tv_ref, *refs, n_w, swiglu):
    it = iter(refs)
    x_ref = next(it)
    w_refs = [next(it) for _ in range(n_w)]
    o_ref = next(it)
    wb_refs = [next(it) for _ in range(n_w)]
    i = pl.program_id(1)
    fresh = jnp.logical_or(i == 0, te_ref[i] != te_ref[jnp.maximum(i - 1, 0)])

    @pl.when(fresh)
    def _():
        for w_ref, wb_ref in zip(w_refs, wb_refs):
            wb_ref[...] = w_ref[...].astype(BF16)

    @pl.when(tv_ref[i] == 1)
    def _():
        x = x_ref[...]
        y = jnp.dot(x, wb_refs[0][...], preferred_element_type=F32)
        if swiglu:
            u = jnp.dot(x, wb_refs[1][...], preferred_element_type=F32)
            y = (y * jax.nn.sigmoid(y)) * u
        o_ref[...] = y.astype(o_ref.dtype)

    @pl.when(tv_ref[i] == 0)
    def _():
        o_ref[...] = jnp.zeros_like(o_ref)


def _grouped_matmul(x, ws, tile_expert, tile_valid, wbase, *, bn, out_dtype, swiglu, name):
    m, k = x.shape
    n = ws[0].shape[2]
    tm = MOE_TM
    assert m % tm == 0 and n % bn == 0
    in_specs = [pl.BlockSpec((tm, k), lambda j, i, te, tv: (i, 0))]
    for _ in ws:
        in_specs.append(pl.BlockSpec((None, k, bn), lambda j, i, te, tv: (wbase + te[i], 0, j),
                                     pipeline_mode=pl.Buffered(1)))
    out_bytes = jnp.dtype(out_dtype).itemsize
    vmem = (2 * tm * k * 2 + len(ws) * k * bn * (4 + 2) + 2 * tm * bn * out_bytes
            + 4 * tm * bn * 4)
    body = functools.partial(_gmm_body, n_w=len(ws), swiglu=swiglu)
    return pl.pallas_call(
        body,
        grid_spec=pltpu.PrefetchScalarGridSpec(
            num_scalar_prefetch=2,
            grid=(n // bn, m // tm),
            in_specs=in_specs,
            out_specs=pl.BlockSpec((tm, bn), lambda j, i, te, tv: (i, j)),
            scratch_shapes=[pltpu.VMEM((k, bn), BF16) for _ in ws],
        ),
        out_shape=jax.ShapeDtypeStruct((m, n), out_dtype),
        compiler_params=_params(2, vmem // (1024 * 1024) + 4),
        name=name,
    )(tile_expert, tile_valid, x, *ws)


def _combine_ln_body(pos_ref, h_ref, w_ref, g_ref, b_ref, y_hbm, o_ref, ob_ref, buf, sem,
                     *, alpha, rows, tp, pad):
    step = pl.program_id(0)

    def start_rows(st, slot):
        def issue(r, c):
            for s in range(TOP_K):
                p = pos_ref[(st * rows + r) * TOP_K + s]
                pltpu.make_async_copy(y_hbm.at[pl.ds(p, 1), :], buf.at[slot, s, pl.ds(r, 1), :],
                                      sem.at[slot]).start()
            return c
        lax.fori_loop(0, rows, issue, 0, unroll=4)

    @pl.when(step == 0)
    def _():
        start_rows(step, 0)

    @pl.when(step + 1 < pl.num_programs(0))
    def _():
        start_rows(step + 1, lax.rem(step + 1, 2))

    slot = lax.rem(step, 2)
    for s in range(TOP_K):
        pltpu.make_async_copy(y_hbm.at[pl.ds(0, rows), :], buf.at[slot, s], sem.at[slot]).wait()

    w = w_ref[...]
    lane = lax.broadcasted_iota(jnp.int32, w.shape, 1)
    f = None
    for s in range(TOP_K):
        ws = jnp.sum(jnp.where(lane == s, w, 0.0), axis=-1, keepdims=True)
        term = ws * buf[slot, s]
        f = term if f is None else f + term
    y = _layernorm_rows(alpha * h_ref[...] + f, g_ref[...], b_ref[...], rows=rows, tp=tp, pad=pad)
    o_ref[...] = y
    ob_ref[...] = y.astype(BF16)


def _combine_ln(h, ys, pos, wts, g, b, layer, *, alpha, tp, pad):
    mp, d = h.shape
    rows = _pick(tp, (384, 256, 128, 64, 32, 16))
    body = functools.partial(_combine_ln_body, alpha=alpha, rows=rows, tp=tp, pad=pad)
    row_spec = pl.BlockSpec((rows, d), lambda i, p: (i, 0))
    par_spec = pl.BlockSpec((None, 1, d), lambda i, p: (layer, 0, 0))
    return pl.pallas_call(
        body,
        grid_spec=pltpu.PrefetchScalarGridSpec(
            num_scalar_prefetch=1,
            grid=(mp // rows,),
            in_specs=[row_spec, pl.BlockSpec((rows, ROUTER_LANES), lambda i, p: (i, 0)),
                      par_spec, par_spec, pl.BlockSpec(memory_space=pl.ANY)],
            out_specs=[row_spec, row_spec],
            scratch_shapes=[pltpu.VMEM((2, TOP_K, rows, d), F32), pltpu.SemaphoreType.DMA((2,))],
        ),
        out_shape=[jax.ShapeDtypeStruct((mp, d), F32), jax.ShapeDtypeStruct((mp, d), BF16)],
        compiler_params=_params(1, 40),
        name="moe_combine_layernorm",
    )(pos, h, wts, g, b, ys)


def _route(idx, *, n_experts, tp, pad):
    mp = idx.shape[0]
    tm = MOE_TM
    n_assign = mp * TOP_K
    n_tiles = -(-n_assign // tm) + n_experts
    n_rows = n_tiles * tm
    token = jnp.arange(mp, dtype=jnp.int32)
    real = (token % tp) >= pad
    e = jnp.where(real[:, None], idx[:, :TOP_K], n_experts).reshape(n_assign)
    onehot = (e[:, None] == jnp.arange(n_experts, dtype=jnp.int32)[None, :]).astype(jnp.int32)
    csum = jnp.cumsum(onehot, axis=0)
    counts = csum[-1]
    rank = jnp.sum(onehot * csum, axis=1) - 1
    gsz = ((counts + tm - 1) // tm) * tm
    gend = jnp.cumsum(gsz)
    gstart = gend - gsz
    routed = e < n_experts
    dst = jnp.where(routed, gstart[jnp.minimum(e, n_experts - 1)] + rank, n_rows)
    assign_token = jnp.arange(n_assign, dtype=jnp.int32) // TOP_K
    src = jnp.zeros((n_rows,), jnp.int32).at[dst].set(assign_token, mode="drop")
    pos = jnp.where(routed, dst, 0).astype(jnp.int32)
    tile_start = jnp.arange(n_tiles, dtype=jnp.int32) * tm
    tile_expert = jnp.minimum(jnp.sum((tile_start[:, None] >= gend[None, :]).astype(jnp.int32), axis=1),
                              n_experts - 1)
    tile_valid = (tile_start < gend[-1]).astype(jnp.int32)
    return src, pos, tile_expert, tile_valid, n_rows


def kernel(x, meta_tokens, w_in, ret_gn_g, hg_norm_g, hg_lb_logits, w_ret_out, w_hg_out, w_o,
           ln1_g, ln1_b, ln2_g, ln2_b, ffn_w_gate, ffn_w_up, ffn_w_down,
           moe_router, moe_w_gate, moe_w_up, moe_w_down):
    batch, seq, d = x.shape
    n_meta = meta_tokens.shape[0]
    depth, _, p_in = w_in.shape
    ret_v = w_ret_out.shape[1]
    hg_v = w_hg_out.shape[1]
    hg_k = hg_lb_logits.shape[1]
    ret_qk = (p_in - 2 * ret_v - 2 * hg_k - 2 * hg_v - 2 * d) // 2
    ret_heads = ret_qk // RET_DK
    hg_heads = hg_k // HG_DK
    n_experts = moe_router.shape[2]
    assert ret_v == ret_heads * RET_DV and hg_v == hg_heads * HG_DV and hg_k == hg_v
    assert n_experts <= ROUTER_LANES

    pad = RET_CHUNK - n_meta
    tp = pad + n_meta + seq
    assert tp % RET_CHUNK == 0 and tp % HG_CHUNK == 0
    mp = batch * tp
    alpha = (2 * depth) ** 0.25

    off = {}
    acc = 0
    for nm, wd in (("rq", ret_qk), ("rk", ret_qk), ("rv", ret_v), ("rg", ret_v), ("hq", hg_k),
                   ("hf", hg_k), ("hi", hg_v), ("hg", hg_v), ("ga", d), ("gb", d)):
        off[nm] = acc
        acc += wd

    meta = jnp.broadcast_to(meta_tokens[None].astype(F32), (batch, n_meta, d))
    h = jnp.concatenate([jnp.zeros((batch, pad, d), F32), meta, x.astype(F32)], axis=1)
    h = h.reshape(mp, d)
    hb = h.astype(BF16)

    pos = jnp.arange(tp, dtype=F32) - float(pad)
    inv = 1.0 / (ROPE_BASE ** jnp.linspace(0.0, 1.0, RET_DK // 2, dtype=F32))
    ang = pos[:, None] * inv[None, :]
    cos, sin = jnp.cos(ang), jnp.sin(ang)
    log_gamma = jnp.log1p(-jnp.exp2(-5.0 - jnp.arange(ret_heads, dtype=F32)))

    ret_gn3 = ret_gn_g.astype(F32).reshape(depth, 1, ret_v)
    hg_gn3 = hg_norm_g.astype(F32).reshape(depth, 1, hg_v)
    ln1_g3, ln1_b3 = ln1_g.reshape(depth, 1, d), ln1_b.reshape(depth, 1, d)
    ln2_g3, ln2_b3 = ln2_g.reshape(depth, 1, d), ln2_b.reshape(depth, 1, d)
    n_moe = moe_router.shape[0]
    d_ffe = moe_w_gate.shape[3]
    router_w = jnp.pad(moe_router.astype(F32), ((0, 0), (0, 0), (0, ROUTER_LANES - n_experts)))
    moe_wg = moe_w_gate.reshape(n_moe * n_experts, d, d_ffe)
    moe_wu = moe_w_up.reshape(n_moe * n_experts, d, d_ffe)
    moe_wd = moe_w_down.reshape(n_moe * n_experts, d_ffe, d)

    bm = _pick(mp, (1056, 768, 512, 384, 256, 128, 64, 32, 16))
    bm_half = _pick(mp, (528, 384, 256, 128, 64, 32, 16))
    tile = lambda n, cands: _pick(n, cands)

    for l in range(depth):
        proj = _matmul(hb, [w_in], l, bm=bm, bn=tile(p_in, (1024, 512, 256, 128)),
                       out_dtype=BF16, name="in_proj")
        yr = _retention(proj, cos, sin, log_gamma, ret_gn3, l, batch=batch, tp=tp,
                        heads=ret_heads, off_q=off["rq"], off_k=off["rk"], off_v=off["rv"],
                        off_g=off["rg"])
        yh = _hgrn2(proj, hg_lb_logits.astype(F32), hg_gn3, l, batch=batch, tp=tp,
                    heads=hg_heads, off_q=off["hq"], off_f=off["hf"], off_i=off["hi"],
                    off_g=off["hg"])
        bn_o = tile(d, (512, 256, 128))
        t_r = _matmul(yr, [w_ret_out], l, bm=bm, bn=bn_o, out_dtype=F32,
                      gate=(proj, off["ga"]), name="ret_out_proj")
        merged = _matmul(yh, [w_hg_out], l, bm=bm, bn=bn_o, out_dtype=BF16,
                         gate=(proj, off["gb"]), add=t_r, name="hg_out_proj")
        h, hb = _matmul_add_ln(merged, w_o, l, h, ln1_g3, ln1_b3, alpha=alpha, tp=tp, pad=pad)
        if l % 2 == 0:
            d_ff = ffn_w_gate.shape[2]
            act = _matmul(hb, [ffn_w_gate, ffn_w_up], l // 2, bm=bm,
                          bn=tile(d_ff, (512, 256, 128)), out_dtype=BF16, swiglu=True,
                          name="ffn_up")
            f = _matmul(act, [ffn_w_down], l // 2, bm=bm_half, bn=tile(d, (512, 256, 128)),
                        out_dtype=F32, name="ffn_down")
            h, hb = _add_ln(h, f, ln2_g3, ln2_b3, l, alpha=alpha, tp=tp, pad=pad)
        else:
            idx, wts = _router(h, router_w, l // 2, n_experts, tp)
            src, pos, tile_expert, tile_valid, n_rows = _route(idx, n_experts=n_experts, tp=tp,
                                                               pad=pad)
            xs = _gather_rows(h, src, tile_valid, n_rows=n_rows)
            wbase = (l // 2) * n_experts
            act = _grouped_matmul(xs, [moe_wg, moe_wu], tile_expert, tile_valid, wbase,
                                  bn=tile(d_ffe, (1408, 256, 128)), out_dtype=BF16, swiglu=True,
                                  name="moe_up")
            ys = _grouped_matmul(act, [moe_wd], tile_expert, tile_valid, wbase,
                                 bn=tile(d, (1024, 512, 256, 128)), out_dtype=F32, swiglu=False,
                                 name="moe_down")
            h, hb = _combine_ln(h, ys, pos, wts, ln2_g3, ln2_b3, l, alpha=alpha, tp=tp, pad=pad)

    out = h.reshape(batch, tp, d)[:, pad + n_meta:, :]
    return out.astype(x.dtype)
```

```python
import functools

import jax
import jax.numpy as jnp
from jax import lax
from jax.experimental import pallas as pl
from jax.experimental.pallas import tpu as pltpu

F32 = jnp.float32
BF16 = jnp.bfloat16

RET_DK = 256
RET_DV = 512
RET_CHUNK = 128
HG_DK = 128
HG_DV = 128
HG_CHUNK = 64
TOP_K = 2
ROPE_BASE = 10000.0
LN_EPS = 1e-5
ROUTER_LANES = 128
VMEM_LIMIT_CAP_MB = 56


def _pick(n, candidates):
    for c in candidates:
        if n % c == 0:
            return c
    raise ValueError(f"no tile in {candidates} divides {n}")


def _params(n_axes, vmem_mb):
    return pltpu.CompilerParams(
        dimension_semantics=("arbitrary",) * n_axes,
        vmem_limit_bytes=min(vmem_mb, VMEM_LIMIT_CAP_MB) * 1024 * 1024)


def _mm_body(*refs, n_w, swiglu, has_gate, has_add):
    it = iter(refs)
    x_ref = next(it)
    w_refs = [next(it) for _ in range(n_w)]
    gate_ref = next(it) if has_gate else None
    add_ref = next(it) if has_add else None
    o_ref = next(it)
    wb_refs = [next(it) for _ in range(n_w)]

    @pl.when(pl.program_id(1) == 0)
    def _():
        for w_ref, wb_ref in zip(w_refs, wb_refs):
            wb_ref[...] = w_ref[...].astype(BF16)

    x = x_ref[...]
    y = jnp.dot(x, wb_refs[0][...], preferred_element_type=F32)
    if swiglu:
        u = jnp.dot(x, wb_refs[1][...], preferred_element_type=F32)
        y = (y * jax.nn.sigmoid(y)) * u
    if has_gate:
        y = jax.nn.sigmoid(gate_ref[...].astype(F32)) * y
    if has_add:
        y = add_ref[...].astype(F32) + y
    o_ref[...] = y.astype(o_ref.dtype)


def _matmul(x, ws, widx, *, bm, bn, out_dtype, swiglu=False, gate=None, add=None, name):
    m, k = x.shape
    n = ws[0].shape[2]
    assert m % bm == 0 and n % bn == 0, (m, bm, n, bn)
    in_specs = [pl.BlockSpec((bm, k), lambda j, i: (i, 0))]
    args = [x]
    for w in ws:
        in_specs.append(pl.BlockSpec((None, k, bn), lambda j, i: (widx, 0, j)))
        args.append(w)
    if gate is not None:
        garr, goff = gate
        assert goff % bn == 0
        gblk = goff // bn
        in_specs.append(pl.BlockSpec((bm, bn), lambda j, i: (i, gblk + j)))
        args.append(garr)
    if add is not None:
        in_specs.append(pl.BlockSpec((bm, bn), lambda j, i: (i, j)))
        args.append(add)
    out_bytes = jnp.dtype(out_dtype).itemsize
    vmem = (2 * bm * k * 2 + len(ws) * k * bn * (2 * 4 + 2) + 2 * bm * bn * out_bytes
            + (2 * bm * bn * 2 if gate is not None else 0)
            + (2 * bm * bn * 4 if add is not None else 0)
            + 6 * bm * bn * 4)
    body = functools.partial(_mm_body, n_w=len(ws), swiglu=swiglu, has_gate=gate is not None,
                             has_add=add is not None)
    return pl.pallas_call(
        body,
        grid=(n // bn, m // bm),
        in_specs=in_specs,
        out_specs=pl.BlockSpec((bm, bn), lambda j, i: (i, j)),
        out_shape=jax.ShapeDtypeStruct((m, n), out_dtype),
        scratch_shapes=[pltpu.VMEM((k, bn), BF16) for _ in ws],
        compiler_params=_params(2, vmem // (1024 * 1024) + 4),
        name=name,
    )(*args)


def _ln_body(h_ref, m_ref, g_ref, b_ref, o_ref, ob_ref, *, alpha, rows, tp, pad):
    y = _layernorm_rows(alpha * h_ref[...] + m_ref[...], g_ref[...], b_ref[...],
                        rows=rows, tp=tp, pad=pad)
    o_ref[...] = y
    ob_ref[...] = y.astype(BF16)


def _layernorm_rows(z, g, b, *, rows, tp, pad):
    mu = jnp.mean(z, axis=-1, keepdims=True)
    zc = z - mu
    var = jnp.mean(zc * zc, axis=-1, keepdims=True)
    y = zc * lax.rsqrt(var + LN_EPS) * g + b
    blk = lax.rem(pl.program_id(0), tp // rows)
    r = blk * rows + lax.broadcasted_iota(jnp.int32, (rows, 1), 0)
    return jnp.where(r >= pad, y, 0.0)


def _mm_ln_body(x_ref, w_ref, h_ref, g_ref, b_ref, o_ref, ob_ref, wb_ref, *, alpha, rows, tp, pad):
    @pl.when(pl.program_id(0) == 0)
    def _():
        wb_ref[...] = w_ref[...].astype(BF16)

    m = jnp.dot(x_ref[...], wb_ref[...], preferred_element_type=F32)
    y = _layernorm_rows(alpha * h_ref[...] + m, g_ref[...], b_ref[...], rows=rows, tp=tp, pad=pad)
    o_ref[...] = y
    ob_ref[...] = y.astype(BF16)


def _matmul_add_ln(x, w, layer, h, g, b, *, alpha, tp, pad):
    mp, k = x.shape
    d = w.shape[2]
    rows = _pick(tp, (384, 256, 128, 64, 32, 16))
    body = functools.partial(_mm_ln_body, alpha=alpha, rows=rows, tp=tp, pad=pad)
    row_spec = pl.BlockSpec((rows, d), lambda i: (i, 0))
    par_spec = pl.BlockSpec((None, 1, d), lambda i: (layer, 0, 0))
    vmem = k * d * (4 + 2) + 2 * rows * k * 2 + rows * d * (2 * 4 + 2 * 4 + 2 * 2 + 3 * 4)
    return pl.pallas_call(
        body,
        grid=(mp // rows,),
        in_specs=[pl.BlockSpec((rows, k), lambda i: (i, 0)),
                  pl.BlockSpec((None, k, d), lambda i: (layer, 0, 0),
                               pipeline_mode=pl.Buffered(1)),
                  row_spec, par_spec, par_spec],
        out_specs=[row_spec, row_spec],
        out_shape=[jax.ShapeDtypeStruct((mp, d), F32), jax.ShapeDtypeStruct((mp, d), BF16)],
        scratch_shapes=[pltpu.VMEM((k, d), BF16)],
        compiler_params=_params(1, vmem // (1024 * 1024) + 4),
        name="mix_out_proj_layernorm",
    )(x, w, h, g, b)


def _add_ln(h, m, g, b, layer, *, alpha, tp, pad):
    mp, d = h.shape
    rows = _pick(tp, (528, 384, 256, 128, 64, 32, 16))
    body = functools.partial(_ln_body, alpha=alpha, rows=rows, tp=tp, pad=pad)
    row_spec = pl.BlockSpec((rows, d), lambda i: (i, 0))
    par_spec = pl.BlockSpec((None, 1, d), lambda i: (layer, 0, 0))
    return pl.pallas_call(
        body,
        grid=(mp // rows,),
        in_specs=[row_spec, row_spec, par_spec, par_spec],
        out_specs=[row_spec, row_spec],
        out_shape=[jax.ShapeDtypeStruct((mp, d), F32), jax.ShapeDtypeStruct((mp, d), BF16)],
        compiler_params=_params(1, 40),
        name="add_layernorm",
    )(h, m, g, b)


def _ret_body(lg_ref, q_ref, k_ref, v_ref, g_ref, cos_ref, sin_ref, gn_ref, o_ref, r_ref,
              *, n_chunks, hpg):
    c_len = RET_CHUNK
    half = RET_DK // 2
    hs = range(hpg)
    ksl = [slice(i * RET_DK, (i + 1) * RET_DK) for i in hs]
    vsl = [slice(i * RET_DV, (i + 1) * RET_DV) for i in hs]

    @pl.when(pl.program_id(2) == 0)
    def _():
        r_ref[...] = jnp.zeros_like(r_ref)

    t_col = lax.broadcasted_iota(jnp.int32, (c_len, 1), 0).astype(F32)
    rel = (lax.broadcasted_iota(jnp.int32, (c_len, c_len), 0)
           - lax.broadcasted_iota(jnp.int32, (c_len, c_len), 1))
    rel_f = jnp.maximum(rel, 0).astype(F32)
    lg = [lg_ref[pl.program_id(1) * hpg + i] for i in hs]
    intra = [jnp.where(rel >= 0, jnp.exp(rel_f * x), 0.0) for x in lg]
    q_decay = [jnp.exp((t_col + 1.0) * x) * (RET_DK ** -0.5) for x in lg]
    k_decay = [jnp.exp((c_len - 1.0 - t_col) * x) for x in lg]
    c_decay = [jnp.exp(jnp.full((1, 1), float(c_len), F32) * x) for x in lg]
    gn = gn_ref[...]

    def rot(x, cos, sin):
        x1, x2 = x[:, :half], x[:, half:]
        return jnp.concatenate([x1 * cos - x2 * sin, x1 * sin + x2 * cos], axis=-1)

    def chunk(c, carry):
        rows = pl.ds(pl.multiple_of(c * c_len, c_len), c_len)
        cos = cos_ref[rows, :]
        sin = sin_ref[rows, :]
        q = [rot(q_ref[rows, ksl[i]].astype(F32), cos, sin) for i in hs]
        k = [rot(k_ref[rows, ksl[i]].astype(F32), cos, sin) for i in hs]
        v = [v_ref[rows, vsl[i]] for i in hs]
        s = [lax.dot_general((q[i] * (RET_DK ** -0.5)).astype(BF16), k[i].astype(BF16),
                             (((1,), (1,)), ((), ())), preferred_element_type=F32) * intra[i]
             for i in hs]
        r_old = [r_ref[i] for i in hs]
        o = [jnp.dot(s[i].astype(BF16), v[i], preferred_element_type=F32)
             + jnp.dot((q[i] * q_decay[i]).astype(BF16), r_old[i].astype(BF16),
                       preferred_element_type=F32) for i in hs]
        for i in hs:
            r_ref[i] = r_old[i] * c_decay[i] + lax.dot_general(
                (k[i] * k_decay[i]).astype(BF16), v[i], (((0,), (0,)), ((), ())),
                preferred_element_type=F32)
        for i in hs:
            mu = jnp.mean(o[i], axis=-1, keepdims=True)
            oc = o[i] - mu
            var = jnp.mean(oc * oc, axis=-1, keepdims=True)
            gate = g_ref[rows, vsl[i]].astype(F32)
            y = oc * lax.rsqrt(var + LN_EPS) * gn[:, vsl[i]] * (gate * jax.nn.sigmoid(gate))
            o_ref[rows, vsl[i]] = y.astype(o_ref.dtype)
        return carry

    lax.fori_loop(0, n_chunks, chunk, 0)


def _retention(proj, cos, sin, log_gamma, gn, layer, *, batch, tp, heads, off_q, off_k, off_v,
               off_g):
    mp = proj.shape[0]
    n_chunks_total = tp // RET_CHUNK
    hpg = _pick(heads, (4, 2, 1))
    cpb = _pick(n_chunks_total, (3, 2, 1))
    tb = cpb * RET_CHUNK
    nblk = tp // tb
    gk, gv = hpg * RET_DK, hpg * RET_DV
    assert off_q % gk == 0 and off_k % gk == 0 and off_v % gv == 0 and off_g % gv == 0
    qb, kb, vb, gb = off_q // gk, off_k // gk, off_v // gv, off_g // gv
    row = lambda b, h, j: b * nblk + j
    body = functools.partial(_ret_body, n_chunks=cpb, hpg=hpg)
    return pl.pallas_call(
        body,
        grid=(batch, heads // hpg, nblk),
        in_specs=[
            pl.BlockSpec(memory_space=pltpu.SMEM),
            pl.BlockSpec((tb, gk), lambda b, h, j: (row(b, h, j), qb + h)),
            pl.BlockSpec((tb, gk), lambda b, h, j: (row(b, h, j), kb + h)),
            pl.BlockSpec((tb, gv), lambda b, h, j: (row(b, h, j), vb + h)),
            pl.BlockSpec((tb, gv), lambda b, h, j: (row(b, h, j), gb + h)),
            pl.BlockSpec((tb, RET_DK // 2), lambda b, h, j: (j, 0)),
            pl.BlockSpec((tb, RET_DK // 2), lambda b, h, j: (j, 0)),
            pl.BlockSpec((None, 1, gv), lambda b, h, j: (layer, 0, h)),
        ],
        out_specs=pl.BlockSpec((tb, gv), lambda b, h, j: (row(b, h, j), h)),
        out_shape=jax.ShapeDtypeStruct((mp, heads * RET_DV), BF16),
        scratch_shapes=[pltpu.VMEM((hpg, RET_DK, RET_DV), F32)],
        compiler_params=_params(3, 40),
        name="retention",
    )(log_gamma, proj, proj, proj, proj, cos, sin, gn)


HG_SAFE_RANGE = 85.0


def _hgrn_body(lbl_ref, gn_ref, q_ref, f_ref, i_ref, g_ref, o_ref, s_ref, acc_ref, bx_ref,
               qx_ref, kx_ref, vx_ref, *, layer, n_chunks, heads):
    c_len = HG_CHUNK

    @pl.when(pl.program_id(1) == 0)
    def _():
        s_ref[...] = jnp.zeros_like(s_ref)

    logits = lbl_ref[...]
    e = jnp.exp(logits - jnp.max(logits, axis=0, keepdims=True))
    sm = e / jnp.sum(e, axis=0, keepdims=True)
    lb = jnp.zeros_like(sm[0:1, :])
    for j in range(1, layer + 1):
        lb = lb + sm[j:j + 1, :]
    gn = gn_ref[...]
    tri = (lax.broadcasted_iota(jnp.int32, (c_len, c_len), 0)
           >= lax.broadcasted_iota(jnp.int32, (c_len, c_len), 1)).astype(BF16)
    t_idx = lax.broadcasted_iota(jnp.int32, (c_len, 1), 0)
    causal = (lax.broadcasted_iota(jnp.int32, (c_len, c_len), 0)
              >= lax.broadcasted_iota(jnp.int32, (c_len, c_len), 1))

    def chunk(c, carry):
        rows = pl.ds(pl.multiple_of(c * c_len, c_len), c_len)

        def head_terms(h):
            ks = slice(h * HG_DK, (h + 1) * HG_DK)
            hq = q_ref[rows, ks].astype(F32)
            q = hq * jax.nn.sigmoid(hq)
            f = lb[:, ks] + (1.0 - lb[:, ks]) * jax.nn.sigmoid(f_ref[rows, ks].astype(F32))
            logf = jnp.log(f)
            p0 = logf.astype(BF16)
            r1 = logf - p0.astype(F32)
            p1 = r1.astype(BF16)
            p2 = (r1 - p1.astype(F32)).astype(BF16)
            b = (jnp.dot(tri, p0, preferred_element_type=F32)
                 + jnp.dot(tri, p1, preferred_element_type=F32)
                 + jnp.dot(tri, p2, preferred_element_type=F32))
            return q, 1.0 - f, b

        def write_out(h, o):
            vs = slice(h * HG_DV, (h + 1) * HG_DV)
            gate = g_ref[rows, vs].astype(F32)
            gate = gate * jax.nn.sigmoid(gate)
            y = o * lax.rsqrt(jnp.mean(o * o, axis=-1, keepdims=True) + LN_EPS)
            o_ref[rows, vs] = (y * gn[:, vs] * gate).astype(o_ref.dtype)

        hs = range(heads)
        vsl = [slice(h * HG_DV, (h + 1) * HG_DV) for h in hs]
        terms = [head_terms(h) for h in hs]
        b_last = [t[2][c_len - 1:c_len, :] for t in terms]
        q_in = [t[0] * jnp.exp(t[2]) for t in terms]
        k_out = [t[1] * jnp.exp(bl - t[2]) for t, bl in zip(terms, b_last)]
        un_mid = [jnp.exp(-0.5 * bl) for bl in b_last]
        q_mid = [jnp.where(bl < -HG_SAFE_RANGE, 0.0, x * u).astype(BF16)
                 for x, u, bl in zip(q_in, un_mid, b_last)]
        k_mid = [jnp.where(bl < -HG_SAFE_RANGE, 0.0, x * u).astype(BF16)
                 for x, u, bl in zip(k_out, un_mid, b_last)]
        sc = [lax.dot_general(qm, km, (((1,), (1,)), ((), ())), preferred_element_type=F32)
              for qm, km in zip(q_mid, k_mid)]
        a = [jnp.where(causal, x, 0.0).astype(BF16) for x in sc]
        vh = [i_ref[rows, vs] for vs in vsl]
        st = [s_ref[h] for h in hs]
        o = [jnp.dot(a[h], vh[h], preferred_element_type=F32)
             + lax.dot_general(q_in[h].astype(BF16), st[h].astype(BF16), (((1,), (1,)), ((), ())),
                               preferred_element_type=F32) for h in hs]
        for h in hs:
            s_ref[h] = st[h] * jnp.exp(b_last[h]) + lax.dot_general(
                vh[h], k_out[h].astype(BF16), (((0,), (0,)), ((), ())),
                preferred_element_type=F32)
        b_min = b_last[0]
        for h in hs:
            acc_ref[:, vsl[h]] = o[h]
            write_out(h, o[h])
            b_min = jnp.minimum(b_min, b_last[h])

        @pl.when(jnp.min(b_min) < -HG_SAFE_RANGE)
        def _():
            for h in range(heads):
                ks = slice(h * HG_DK, (h + 1) * HG_DK)
                vs = slice(h * HG_DV, (h + 1) * HG_DV)
                q, k, b = head_terms(h)
                bx_ref[:, ks] = b
                qx_ref[:, ks] = jnp.where(b[c_len - 1:c_len, :] < -HG_SAFE_RANGE, q, 0.0)
                kx_ref[:, ks] = k
                vx_ref[:, vs] = i_ref[rows, vs].astype(F32)

            def pair(s, cc):
                bs = bx_ref[pl.ds(s, 1), :]
                d = jnp.where(t_idx >= s, jnp.exp(jnp.minimum(bx_ref[...] - bs, 0.0)), 0.0)
                p = qx_ref[...] * d * kx_ref[pl.ds(s, 1), :]
                v_s = vx_ref[pl.ds(s, 1), :]
                for h in range(heads):
                    ks = slice(h * HG_DK, (h + 1) * HG_DK)
                    vs = slice(h * HG_DV, (h + 1) * HG_DV)
                    acc_ref[:, vs] += jnp.sum(p[:, ks], axis=-1, keepdims=True) * v_s[:, vs]
                return cc
            lax.fori_loop(0, c_len, pair, 0)
            for h in range(heads):
                write_out(h, acc_ref[:, h * HG_DV:(h + 1) * HG_DV])
        return carry

    lax.fori_loop(0, n_chunks, chunk, 0)


def _hgrn2(proj, lb_logits, gn, layer, *, batch, tp, heads, off_q, off_f, off_i, off_g):
    mp = proj.shape[0]
    hk, hv = heads * HG_DK, heads * HG_DV
    rb = _pick(tp, (704, 384, 256, 128, 64))
    nblk = tp // rb
    depth = lb_logits.shape[0]
    body = functools.partial(_hgrn_body, layer=layer, n_chunks=rb // HG_CHUNK, heads=heads)
    slab = pltpu.VMEM((HG_CHUNK, hv), F32)
    row = lambda b, j: b * nblk + j
    return pl.pallas_call(
        body,
        grid=(batch, nblk),
        in_specs=[
            pl.BlockSpec((depth, hk), lambda b, j: (0, 0)),
            pl.BlockSpec((None, 1, hv), lambda b, j: (layer, 0, 0)),
            pl.BlockSpec((rb, hk), lambda b, j: (row(b, j), off_q // hk)),
            pl.BlockSpec((rb, hk), lambda b, j: (row(b, j), off_f // hk)),
            pl.BlockSpec((rb, hv), lambda b, j: (row(b, j), off_i // hv)),
            pl.BlockSpec((rb, hv), lambda b, j: (row(b, j), off_g // hv)),
        ],
        out_specs=pl.BlockSpec((rb, hv), lambda b, j: (row(b, j), 0)),
        out_shape=jax.ShapeDtypeStruct((mp, hv), BF16),
        scratch_shapes=[pltpu.VMEM((heads, HG_DV, HG_DK), F32), slab, slab, slab, slab, slab],
        compiler_params=_params(2, 48),
        name="hgrn2",
    )(lb_logits, gn, proj, proj, proj, proj)


def _router_body(x_ref, w_ref, idx_ref, wts_ref, *, n_experts):
    x = x_ref[...]
    w = w_ref[...]
    xh = x.astype(BF16)
    xl = (x - xh.astype(F32)).astype(BF16)
    wh = w.astype(BF16)
    wl = (w - wh.astype(F32)).astype(BF16)
    logits = (jnp.dot(xh, wh, preferred_element_type=F32)
              + jnp.dot(xh, wl, preferred_element_type=F32)
              + jnp.dot(xl, wh, preferred_element_type=F32))
    lane = lax.broadcasted_iota(jnp.int32, logits.shape, 1)
    neg = jnp.float32(-jnp.inf)
    logits = jnp.where(lane < n_experts, logits, neg)
    v1 = jnp.max(logits, axis=-1, keepdims=True)
    i1 = jnp.min(jnp.where(logits == v1, lane, ROUTER_LANES), axis=-1, keepdims=True)
    rest = jnp.where(lane == i1, neg, logits)
    v2 = jnp.max(rest, axis=-1, keepdims=True)
    i2 = jnp.min(jnp.where(rest == v2, lane, ROUTER_LANES), axis=-1, keepdims=True)
    e2 = jnp.exp(v2 - v1)
    w1 = 1.0 / (1.0 + e2)
    w2 = e2 / (1.0 + e2)
    idx_ref[...] = jnp.where(lane == 0, i1, jnp.where(lane == 1, i2, 0))
    wts_ref[...] = jnp.where(lane == 0, w1, jnp.where(lane == 1, w2, 0.0))


def _router(h, w_router_padded, layer_idx, n_experts, tp):
    mp, d = h.shape
    rows = _pick(tp, (528, 384, 256, 128, 64, 32, 16))
    body = functools.partial(_router_body, n_experts=n_experts)
    out_spec = pl.BlockSpec((rows, ROUTER_LANES), lambda i: (i, 0))
    return pl.pallas_call(
        body,
        grid=(mp // rows,),
        in_specs=[pl.BlockSpec((rows, d), lambda i: (i, 0)),
                  pl.BlockSpec((None, d, ROUTER_LANES), lambda i: (layer_idx, 0, 0))],
        out_specs=[out_spec, out_spec],
        out_shape=[jax.ShapeDtypeStruct((mp, ROUTER_LANES), jnp.int32),
                   jax.ShapeDtypeStruct((mp, ROUTER_LANES), F32)],
        compiler_params=_params(1, 40),
        name="moe_router",
    )(h, w_router_padded)


MOE_TM = 512
GATHER_ROWS = 256


def _gather_body(src_ref, tv_ref, x_hbm, o_ref, buf, sem, *, rows, steps_per_tile):
    step = pl.program_id(0)

    def start_rows(st, slot):
        @pl.when(tv_ref[st // steps_per_tile] == 1)
        def _():
            def issue(r, c):
                t = src_ref[st * rows + r]
                pltpu.make_async_copy(x_hbm.at[pl.ds(t, 1), :], buf.at[slot, pl.ds(r, 1), :],
                                      sem.at[slot]).start()
                return c
            lax.fori_loop(0, rows, issue, 0, unroll=8)

    @pl.when(step == 0)
    def _():
        start_rows(step, 0)

    @pl.when(step + 1 < pl.num_programs(0))
    def _():
        start_rows(step + 1, lax.rem(step + 1, 2))

    slot = lax.rem(step, 2)
    valid = tv_ref[step // steps_per_tile] == 1

    @pl.when(valid)
    def _():
        pltpu.make_async_copy(x_hbm.at[pl.ds(0, rows), :], buf.at[slot], sem.at[slot]).wait()
        o_ref[...] = buf[slot].astype(o_ref.dtype)

    @pl.when(jnp.logical_not(valid))
    def _():
        o_ref[...] = jnp.zeros_like(o_ref)


def _gather_rows(h, src, tile_valid, *, n_rows):
    mp, d = h.shape
    rows = GATHER_ROWS
    assert n_rows % rows == 0 and MOE_TM % rows == 0
    body = functools.partial(_gather_body, rows=rows, steps_per_tile=MOE_TM // rows)
    return pl.pallas_call(
        body,
        grid_spec=pltpu.PrefetchScalarGridSpec(
            num_scalar_prefetch=2,
            grid=(n_rows // rows,),
            in_specs=[pl.BlockSpec(memory_space=pl.ANY)],
            out_specs=pl.BlockSpec((rows, d), lambda i, src, tv: (i, 0)),
            scratch_shapes=[pltpu.VMEM((2, rows, d), F32), pltpu.SemaphoreType.DMA((2,))],
        ),
        out_shape=jax.ShapeDtypeStruct((n_rows, d), BF16),
        compiler_params=_params(1, 32),
        name="moe_gather",
    )(src, tile_valid, h)


def _gmm_body(te_ref, tv_ref, nx_ref, *refs, n_w, swiglu, wbase, bn):
    it = iter(refs)
    x_ref = next(it)
    w_hbm = [next(it) for _ in range(n_w)]
    o_ref = next(it)
    wb_refs = [next(it) for _ in range(n_w)]
    wf_refs = [next(it) for _ in range(n_w)]
    sem = next(it)
    j = pl.program_id(0)
    i = pl.program_id(1)
    fresh = jnp.logical_or(i == 0, te_ref[i] != te_ref[jnp.maximum(i - 1, 0)])

    def weight_copy(w, expert, col_tile):
        cols = pl.ds(pl.multiple_of(col_tile * bn, bn), bn)
        return pltpu.make_async_copy(w_hbm[w].at[wbase + expert, :, cols], wf_refs[w], sem.at[w])

    @pl.when(fresh)
    def _():
        @pl.when(jnp.logical_and(i == 0, j == 0))
        def _():
            for w in range(n_w):
                weight_copy(w, te_ref[0], 0).start()
        for w in range(n_w):
            weight_copy(w, te_ref[i], j).wait()
            wb_refs[w][...] = wf_refs[w][...].astype(BF16)
        nxt = nx_ref[i]

        @pl.when(nxt >= 0)
        def _():
            for w in range(n_w):
                weight_copy(w, nxt, j).start()

        @pl.when(jnp.logical_and(nxt < 0, j + 1 < pl.num_programs(0)))
        def _():
            for w in range(n_w):
                weight_copy(w, te_ref[0], j + 1).start()

    @pl.when(tv_ref[i] == 1)
    def _():
        x = x_ref[...]
        y = jnp.dot(x, wb_refs[0][...], preferred_element_type=F32)
        if swiglu:
            u = jnp.dot(x, wb_refs[1][...], preferred_element_type=F32)
            y = (y * jax.nn.sigmoid(y)) * u
        o_ref[...] = y.astype(o_ref.dtype)

    @pl.when(tv_ref[i] == 0)
    def _():
        o_ref[...] = jnp.zeros_like(o_ref)


def _grouped_matmul(x, ws, tile_expert, tile_valid, next_expert, wbase, *, bn, out_dtype, swiglu,
                    name):
    m, k = x.shape
    n = ws[0].shape[2]
    tm = MOE_TM
    assert m % tm == 0 and n % bn == 0
    in_specs = [pl.BlockSpec((tm, k), lambda j, i, te, tv, nx: (i, 0))]
    in_specs += [pl.BlockSpec(memory_space=pl.ANY) for _ in ws]
    out_bytes = jnp.dtype(out_dtype).itemsize
    vmem = (2 * tm * k * 2 + len(ws) * k * bn * (4 + 2) + 2 * tm * bn * out_bytes
            + 4 * tm * bn * 4)
    body = functools.partial(_gmm_body, n_w=len(ws), swiglu=swiglu, wbase=wbase, bn=bn)
    return pl.pallas_call(
        body,
        grid_spec=pltpu.PrefetchScalarGridSpec(
            num_scalar_prefetch=3,
            grid=(n // bn, m // tm),
            in_specs=in_specs,
            out_specs=pl.BlockSpec((tm, bn), lambda j, i, te, tv, nx: (i, j)),
            scratch_shapes=([pltpu.VMEM((k, bn), BF16) for _ in ws]
                            + [pltpu.VMEM((k, bn), F32) for _ in ws]
                            + [pltpu.SemaphoreType.DMA((len(ws),))]),
        ),
        out_shape=jax.ShapeDtypeStruct((m, n), out_dtype),
        compiler_params=_params(2, vmem // (1024 * 1024) + 4),
        name=name,
    )(tile_expert, tile_valid, next_expert, x, *ws)


def _combine_ln_body(pos_ref, h_ref, w_ref, g_ref, b_ref, y_hbm, o_ref, ob_ref, buf, sem,
                     *, alpha, rows, tp, pad):
    step = pl.program_id(0)

    def start_rows(st, slot):
        def issue(r, c):
            for s in range(TOP_K):
                p = pos_ref[(st * rows + r) * TOP_K + s]
                pltpu.make_async_copy(y_hbm.at[pl.ds(p, 1), :], buf.at[slot, s, pl.ds(r, 1), :],
                                      sem.at[slot]).start()
            return c
        lax.fori_loop(0, rows, issue, 0, unroll=4)

    @pl.when(step == 0)
    def _():
        start_rows(step, 0)

    @pl.when(step + 1 < pl.num_programs(0))
    def _():
        start_rows(step + 1, lax.rem(step + 1, 2))

    slot = lax.rem(step, 2)
    for s in range(TOP_K):
        pltpu.make_async_copy(y_hbm.at[pl.ds(0, rows), :], buf.at[slot, s], sem.at[slot]).wait()

    w = w_ref[...]
    lane = lax.broadcasted_iota(jnp.int32, w.shape, 1)
    f = None
    for s in range(TOP_K):
        ws = jnp.sum(jnp.where(lane == s, w, 0.0), axis=-1, keepdims=True)
        term = ws * buf[slot, s]
        f = term if f is None else f + term
    y = _layernorm_rows(alpha * h_ref[...] + f, g_ref[...], b_ref[...], rows=rows, tp=tp, pad=pad)
    o_ref[...] = y
    ob_ref[...] = y.astype(BF16)


def _combine_ln(h, ys, pos, wts, g, b, layer, *, alpha, tp, pad):
    mp, d = h.shape
    rows = _pick(tp, (384, 256, 128, 64, 32, 16))
    body = functools.partial(_combine_ln_body, alpha=alpha, rows=rows, tp=tp, pad=pad)
    row_spec = pl.BlockSpec((rows, d), lambda i, p: (i, 0))
    par_spec = pl.BlockSpec((None, 1, d), lambda i, p: (layer, 0, 0))
    return pl.pallas_call(
        body,
        grid_spec=pltpu.PrefetchScalarGridSpec(
            num_scalar_prefetch=1,
            grid=(mp // rows,),
            in_specs=[row_spec, pl.BlockSpec((rows, ROUTER_LANES), lambda i, p: (i, 0)),
                      par_spec, par_spec, pl.BlockSpec(memory_space=pl.ANY)],
            out_specs=[row_spec, row_spec],
            scratch_shapes=[pltpu.VMEM((2, TOP_K, rows, d), F32), pltpu.SemaphoreType.DMA((2,))],
        ),
        out_shape=[jax.ShapeDtypeStruct((mp, d), F32), jax.ShapeDtypeStruct((mp, d), BF16)],
        compiler_params=_params(1, 40),
        name="moe_combine_layernorm",
    )(pos, h, wts, g, b, ys)


def _route(idx, *, n_experts, tp, pad):
    mp = idx.shape[0]
    tm = MOE_TM
    n_assign = mp * TOP_K
    n_tiles = -(-n_assign // tm) + n_experts
    n_rows = n_tiles * tm
    token = jnp.arange(mp, dtype=jnp.int32)
    real = (token % tp) >= pad
    e = jnp.where(real[:, None], idx[:, :TOP_K], n_experts).reshape(n_assign)
    onehot = (e[:, None] == jnp.arange(n_experts, dtype=jnp.int32)[None, :]).astype(jnp.int32)
    csum = jnp.cumsum(onehot, axis=0)
    counts = csum[-1]
    rank = jnp.sum(onehot * csum, axis=1) - 1
    gsz = ((counts + tm - 1) // tm) * tm
    gend = jnp.cumsum(gsz)
    gstart = gend - gsz
    routed = e < n_experts
    dst = jnp.where(routed, gstart[jnp.minimum(e, n_experts - 1)] + rank, n_rows)
    assign_token = jnp.arange(n_assign, dtype=jnp.int32) // TOP_K
    src = jnp.zeros((n_rows,), jnp.int32).at[dst].set(assign_token, mode="drop")
    pos = jnp.where(routed, dst, 0).astype(jnp.int32)
    tile_start = jnp.arange(n_tiles, dtype=jnp.int32) * tm
    tile_valid = tile_start < gend[-1]
    tile_expert = jnp.sum((tile_start[:, None] >= gend[None, :]).astype(jnp.int32), axis=1)
    last_expert = jnp.max(jnp.where(tile_valid, tile_expert, 0))
    tile_expert = jnp.where(tile_valid, tile_expert, last_expert)
    later = jnp.arange(n_tiles)[None, :] > jnp.arange(n_tiles)[:, None]
    differs = jnp.logical_and(later, tile_expert[None, :] != tile_expert[:, None])
    first = jnp.argmax(differs, axis=1)
    next_expert = jnp.where(jnp.any(differs, axis=1), tile_expert[first], -1).astype(jnp.int32)
    return src, pos, tile_expert, tile_valid.astype(jnp.int32), next_expert, n_rows


def kernel(x, meta_tokens, w_in, ret_gn_g, hg_norm_g, hg_lb_logits, w_ret_out, w_hg_out, w_o,
           ln1_g, ln1_b, ln2_g, ln2_b, ffn_w_gate, ffn_w_up, ffn_w_down,
           moe_router, moe_w_gate, moe_w_up, moe_w_down):
    batch, seq, d = x.shape
    n_meta = meta_tokens.shape[0]
    depth, _, p_in = w_in.shape
    ret_v = w_ret_out.shape[1]
    hg_v = w_hg_out.shape[1]
    hg_k = hg_lb_logits.shape[1]
    ret_qk = (p_in - 2 * ret_v - 2 * hg_k - 2 * hg_v - 2 * d) // 2
    ret_heads = ret_qk // RET_DK
    hg_heads = hg_k // HG_DK
    n_experts = moe_router.shape[2]
    assert ret_v == ret_heads * RET_DV and hg_v == hg_heads * HG_DV and hg_k == hg_v
    assert n_experts <= ROUTER_LANES

    pad = RET_CHUNK - n_meta
    tp = pad + n_meta + seq
    assert tp % RET_CHUNK == 0 and tp % HG_CHUNK == 0
    mp = batch * tp
    alpha = (2 * depth) ** 0.25

    off = {}
    acc = 0
    for nm, wd in (("rq", ret_qk), ("rk", ret_qk), ("rv", ret_v), ("rg", ret_v), ("hq", hg_k),
                   ("hf", hg_k), ("hi", hg_v), ("hg", hg_v), ("ga", d), ("gb", d)):
        off[nm] = acc
        acc += wd

    meta = jnp.broadcast_to(meta_tokens[None].astype(F32), (batch, n_meta, d))
    h = jnp.concatenate([jnp.zeros((batch, pad, d), F32), meta, x.astype(F32)], axis=1)
    h = h.reshape(mp, d)
    hb = h.astype(BF16)

    pos = jnp.arange(tp, dtype=F32) - float(pad)
    inv = 1.0 / (ROPE_BASE ** jnp.linspace(0.0, 1.0, RET_DK // 2, dtype=F32))
    ang = pos[:, None] * inv[None, :]
    cos, sin = jnp.cos(ang), jnp.sin(ang)
    log_gamma = jnp.log1p(-jnp.exp2(-5.0 - jnp.arange(ret_heads, dtype=F32)))

    ret_gn3 = ret_gn_g.astype(F32).reshape(depth, 1, ret_v)
    hg_gn3 = hg_norm_g.astype(F32).reshape(depth, 1, hg_v)
    ln1_g3, ln1_b3 = ln1_g.reshape(depth, 1, d), ln1_b.reshape(depth, 1, d)
    ln2_g3, ln2_b3 = ln2_g.reshape(depth, 1, d), ln2_b.reshape(depth, 1, d)
    n_moe = moe_router.shape[0]
    d_ffe = moe_w_gate.shape[3]
    router_w = jnp.pad(moe_router.astype(F32), ((0, 0), (0, 0), (0, ROUTER_LANES - n_experts)))
    moe_wg = moe_w_gate.reshape(n_moe * n_experts, d, d_ffe)
    moe_wu = moe_w_up.reshape(n_moe * n_experts, d, d_ffe)
    moe_wd = moe_w_down.reshape(n_moe * n_experts, d_ffe, d)

    bm = _pick(mp, (1056, 768, 512, 384, 256, 128, 64, 32, 16))
    bm_half = _pick(mp, (528, 384, 256, 128, 64, 32, 16))
    tile = lambda n, cands: _pick(n, cands)

    for l in range(depth):
        proj = _matmul(hb, [w_in], l, bm=bm, bn=tile(p_in, (1024, 512, 256, 128)),
                       out_dtype=BF16, name="in_proj")
        yr = _retention(proj, cos, sin, log_gamma, ret_gn3, l, batch=batch, tp=tp,
                        heads=ret_heads, off_q=off["rq"], off_k=off["rk"], off_v=off["rv"],
                        off_g=off["rg"])
        yh = _hgrn2(proj, hg_lb_logits.astype(F32), hg_gn3, l, batch=batch, tp=tp,
                    heads=hg_heads, off_q=off["hq"], off_f=off["hf"], off_i=off["hi"],
                    off_g=off["hg"])
        bn_o = tile(d, (512, 256, 128))
        t_r = _matmul(yr, [w_ret_out], l, bm=bm, bn=bn_o, out_dtype=F32,
                      gate=(proj, off["ga"]), name="ret_out_proj")
        merged = _matmul(yh, [w_hg_out], l, bm=bm, bn=bn_o, out_dtype=BF16,
                         gate=(proj, off["gb"]), add=t_r, name="hg_out_proj")
        h, hb = _matmul_add_ln(merged, w_o, l, h, ln1_g3, ln1_b3, alpha=alpha, tp=tp, pad=pad)
        if l % 2 == 0:
            d_ff = ffn_w_gate.shape[2]
            act = _matmul(hb, [ffn_w_gate, ffn_w_up], l // 2, bm=bm,
                          bn=tile(d_ff, (512, 256, 128)), out_dtype=BF16, swiglu=True,
                          name="ffn_up")
            f = _matmul(act, [ffn_w_down], l // 2, bm=bm_half, bn=tile(d, (512, 256, 128)),
                        out_dtype=F32, name="ffn_down")
            h, hb = _add_ln(h, f, ln2_g3, ln2_b3, l, alpha=alpha, tp=tp, pad=pad)
        else:
            idx, wts = _router(h, router_w, l // 2, n_experts, tp)
            src, pos, tile_expert, tile_valid, next_expert, n_rows = _route(
                idx, n_experts=n_experts, tp=tp, pad=pad)
            xs = _gather_rows(h, src, tile_valid, n_rows=n_rows)
            wbase = (l // 2) * n_experts
            act = _grouped_matmul(xs, [moe_wg, moe_wu], tile_expert, tile_valid, next_expert, wbase,
                                  bn=tile(d_ffe, (1408, 256, 128)), out_dtype=BF16, swiglu=True,
                                  name="moe_up")
            ys = _grouped_matmul(act, [moe_wd], tile_expert, tile_valid, next_expert, wbase,
                                 bn=tile(d, (1024, 512, 256, 128)), out_dtype=F32, swiglu=False,
                                 name="moe_down")
            h, hb = _combine_ln(h, ys, pos, wts, ln2_g3, ln2_b3, l, alpha=alpha, tp=tp, pad=pad)

    out = h.reshape(batch, tp, d)[:, pad + n_meta:, :]
    return out.astype(x.dtype)
```

```python
import functools

import jax
import jax.numpy as jnp
from jax import lax
from jax.experimental import pallas as pl
from jax.experimental.pallas import tpu as pltpu

F32 = jnp.float32
BF16 = jnp.bfloat16

RET_DK = 256
RET_DV = 512
RET_CHUNK = 128
HG_DK = 128
HG_DV = 128
HG_CHUNK = 64
TOP_K = 2
ROPE_BASE = 10000.0
LN_EPS = 1e-5
ROUTER_LANES = 128
VMEM_LIMIT_CAP_MB = 56


def _pick(n, candidates):
    for c in candidates:
        if n % c == 0:
            return c
    raise ValueError(f"no tile in {candidates} divides {n}")


def _params(n_axes, vmem_mb):
    return pltpu.CompilerParams(
        dimension_semantics=("arbitrary",) * n_axes,
        vmem_limit_bytes=min(vmem_mb, VMEM_LIMIT_CAP_MB) * 1024 * 1024)


def _mm_body(*refs, n_w, swiglu, has_gate):
    it = iter(refs)
    x_ref = next(it)
    w_refs = [next(it) for _ in range(n_w)]
    gate_ref = next(it) if has_gate else None
    o_ref = next(it)
    wb_refs = [next(it) for _ in range(n_w)]

    @pl.when(pl.program_id(1) == 0)
    def _():
        for w_ref, wb_ref in zip(w_refs, wb_refs):
            wb_ref[...] = w_ref[...].astype(BF16)

    x = x_ref[...]
    y = jnp.dot(x, wb_refs[0][...], preferred_element_type=F32)
    if swiglu:
        u = jnp.dot(x, wb_refs[1][...], preferred_element_type=F32)
        y = (y * jax.nn.sigmoid(y)) * u
    if has_gate:
        y = jax.nn.sigmoid(gate_ref[...].astype(F32)) * y
    o_ref[...] = y.astype(o_ref.dtype)


def _matmul(x, ws, widx, *, bm, bn, out_dtype, swiglu=False, gate=None, name):
    m, k = x.shape
    n = ws[0].shape[2]
    assert m % bm == 0 and n % bn == 0, (m, bm, n, bn)
    in_specs = [pl.BlockSpec((bm, k), lambda j, i: (i, 0))]
    args = [x]
    for w in ws:
        in_specs.append(pl.BlockSpec((None, k, bn), lambda j, i: (widx, 0, j)))
        args.append(w)
    if gate is not None:
        garr, goff = gate
        assert goff % bn == 0
        gblk = goff // bn
        in_specs.append(pl.BlockSpec((bm, bn), lambda j, i: (i, gblk + j)))
        args.append(garr)
    out_bytes = jnp.dtype(out_dtype).itemsize
    vmem = (2 * bm * k * 2 + len(ws) * k * bn * (2 * 4 + 2) + 2 * bm * bn * out_bytes
            + (2 * bm * bn * 2 if gate is not None else 0)
            + 6 * bm * bn * 4)
    body = functools.partial(_mm_body, n_w=len(ws), swiglu=swiglu, has_gate=gate is not None)
    return pl.pallas_call(
        body,
        grid=(n // bn, m // bm),
        in_specs=in_specs,
        out_specs=pl.BlockSpec((bm, bn), lambda j, i: (i, j)),
        out_shape=jax.ShapeDtypeStruct((m, n), out_dtype),
        scratch_shapes=[pltpu.VMEM((k, bn), BF16) for _ in ws],
        compiler_params=_params(2, vmem // (1024 * 1024) + 4),
        name=name,
    )(*args)


def _mm_xres_body(x_ref, w_ref, o_ref, *, sub):
    wb = w_ref[...].astype(BF16)
    for r0 in range(0, x_ref.shape[0], sub):
        rs = slice(r0, r0 + sub)
        o_ref[rs, :] = jnp.dot(x_ref[rs, :], wb, preferred_element_type=F32).astype(o_ref.dtype)


def _matmul_rows_resident(x, w, widx, *, rows, bn, out_dtype, name):
    m, k = x.shape
    n = w.shape[2]
    assert m % rows == 0 and n % bn == 0
    sub = _pick(rows, (1056, 768, 512, 384, 256, 128, 64, 32, 16))
    out_bytes = jnp.dtype(out_dtype).itemsize
    vmem = rows * k * 2 + k * bn * (2 * 4 + 2) + 2 * rows * bn * out_bytes + 3 * sub * bn * 4
    return pl.pallas_call(
        functools.partial(_mm_xres_body, sub=sub),
        grid=(m // rows, n // bn),
        in_specs=[pl.BlockSpec((rows, k), lambda b, j: (b, 0), pipeline_mode=pl.Buffered(1)),
                  pl.BlockSpec((None, k, bn), lambda b, j: (widx, 0, j))],
        out_specs=pl.BlockSpec((rows, bn), lambda b, j: (b, j)),
        out_shape=jax.ShapeDtypeStruct((m, n), out_dtype),
        compiler_params=_params(2, vmem // (1024 * 1024) + 6),
        name=name,
    )(x, w)


def _ln_body(h_ref, m_ref, g_ref, b_ref, o_ref, ob_ref, *, alpha, rows, tp, pad):
    y = _layernorm_rows(alpha * h_ref[...] + m_ref[...], g_ref[...], b_ref[...],
                        rows=rows, tp=tp, pad=pad)
    o_ref[...] = y
    ob_ref[...] = y.astype(BF16)


def _layernorm_rows(z, g, b, *, rows, tp, pad, row0=0):
    mu = jnp.mean(z, axis=-1, keepdims=True)
    zc = z - mu
    var = jnp.mean(zc * zc, axis=-1, keepdims=True)
    y = zc * lax.rsqrt(var + LN_EPS) * g + b
    blk = lax.rem(pl.program_id(0), tp // rows)
    r = blk * rows + row0 + lax.broadcasted_iota(jnp.int32, (z.shape[0], 1), 0)
    return jnp.where(r >= pad, y, 0.0)


RESIDENT_ROWS = 384
RESIDENT_SUB = 128


def _mm_ln_body(x_ref, w_ref, h_ref, g_ref, b_ref, o_ref, ob_ref, wb_ref, *, alpha, rows, sub, tp,
                pad):
    @pl.when(pl.program_id(0) == 0)
    def _():
        wb_ref[...] = w_ref[...].astype(BF16)

    for r0 in range(0, rows, sub):
        rs = slice(r0, r0 + sub)
        m = jnp.dot(x_ref[rs, :], wb_ref[...], preferred_element_type=F32)
        y = _layernorm_rows(alpha * h_ref[rs, :] + m, g_ref[...], b_ref[...], rows=rows, tp=tp,
                            pad=pad, row0=r0)
        o_ref[rs, :] = y
        ob_ref[rs, :] = y.astype(BF16)


def _mm_gate_add_body(x_ref, w_ref, gate_ref, add_ref, o_ref, wb_ref, *, rows, sub):
    @pl.when(pl.program_id(0) == 0)
    def _():
        wb_ref[...] = w_ref[...].astype(BF16)

    for r0 in range(0, rows, sub):
        rs = slice(r0, r0 + sub)
        m = jnp.dot(x_ref[rs, :], wb_ref[...], preferred_element_type=F32)
        y = add_ref[rs, :] + jax.nn.sigmoid(gate_ref[rs, :].astype(F32)) * m
        o_ref[rs, :] = y.astype(o_ref.dtype)


def _resident_specs(x, w, layer, tp):
    mp, k = x.shape
    d = w.shape[2]
    rows = _pick(tp, (RESIDENT_ROWS, 256, 128, 64, 32, 16))
    sub = _pick(rows, (RESIDENT_SUB, 64, 32, 16))
    x_spec = pl.BlockSpec((rows, k), lambda i: (i, 0))
    w_spec = pl.BlockSpec((None, k, d), lambda i: (layer, 0, 0), pipeline_mode=pl.Buffered(1))
    return mp, k, d, rows, sub, x_spec, w_spec


def _matmul_add_ln(x, w, layer, h, g, b, *, alpha, tp, pad):
    mp, k, d, rows, sub, x_spec, w_spec = _resident_specs(x, w, layer, tp)
    body = functools.partial(_mm_ln_body, alpha=alpha, rows=rows, sub=sub, tp=tp, pad=pad)
    row_spec = pl.BlockSpec((rows, d), lambda i: (i, 0))
    par_spec = pl.BlockSpec((None, 1, d), lambda i: (layer, 0, 0))
    vmem = k * d * (4 + 2) + 2 * rows * k * 2 + rows * d * (2 * 4 + 2 * 4 + 2 * 2 + 3 * 4)
    return pl.pallas_call(
        body,
        grid=(mp // rows,),
        in_specs=[x_spec, w_spec, row_spec, par_spec, par_spec],
        out_specs=[row_spec, row_spec],
        out_shape=[jax.ShapeDtypeStruct((mp, d), F32), jax.ShapeDtypeStruct((mp, d), BF16)],
        scratch_shapes=[pltpu.VMEM((k, d), BF16)],
        compiler_params=_params(1, vmem // (1024 * 1024) + 4),
        name="mix_out_proj_layernorm",
    )(x, w, h, g, b)


def _matmul_gate_add(x, w, layer, gate, add, *, tp, name):
    mp, k, d, rows, sub, x_spec, w_spec = _resident_specs(x, w, layer, tp)
    garr, goff = gate
    assert goff % d == 0
    gblk = goff // d
    body = functools.partial(_mm_gate_add_body, rows=rows, sub=sub)
    row_spec = pl.BlockSpec((rows, d), lambda i: (i, 0))
    vmem = k * d * (4 + 2) + 2 * rows * k * 2 + rows * d * (2 * 2 + 2 * 4 + 2 * 2 + 3 * 4)
    return pl.pallas_call(
        body,
        grid=(mp // rows,),
        in_specs=[x_spec, w_spec, pl.BlockSpec((rows, d), lambda i: (i, gblk)), row_spec],
        out_specs=row_spec,
        out_shape=jax.ShapeDtypeStruct((mp, d), BF16),
        scratch_shapes=[pltpu.VMEM((k, d), BF16)],
        compiler_params=_params(1, vmem // (1024 * 1024) + 4),
        name=name,
    )(x, w, garr, add)


def _add_ln(h, m, g, b, layer, *, alpha, tp, pad):
    mp, d = h.shape
    rows = _pick(tp, (528, 384, 256, 128, 64, 32, 16))
    body = functools.partial(_ln_body, alpha=alpha, rows=rows, tp=tp, pad=pad)
    row_spec = pl.BlockSpec((rows, d), lambda i: (i, 0))
    par_spec = pl.BlockSpec((None, 1, d), lambda i: (layer, 0, 0))
    return pl.pallas_call(
        body,
        grid=(mp // rows,),
        in_specs=[row_spec, row_spec, par_spec, par_spec],
        out_specs=[row_spec, row_spec],
        out_shape=[jax.ShapeDtypeStruct((mp, d), F32), jax.ShapeDtypeStruct((mp, d), BF16)],
        compiler_params=_params(1, 40),
        name="add_layernorm",
    )(h, m, g, b)


def _ret_body(lg_ref, q_ref, k_ref, v_ref, g_ref, cos_ref, sin_ref, gn_ref, o_ref, r_ref,
              *, n_chunks, hpg):
    c_len = RET_CHUNK
    half = RET_DK // 2
    hs = range(hpg)
    ksl = [slice(i * RET_DK, (i + 1) * RET_DK) for i in hs]
    vsl = [slice(i * RET_DV, (i + 1) * RET_DV) for i in hs]

    @pl.when(pl.program_id(2) == 0)
    def _():
        r_ref[...] = jnp.zeros_like(r_ref)

    t_col = lax.broadcasted_iota(jnp.int32, (c_len, 1), 0).astype(F32)
    rel = (lax.broadcasted_iota(jnp.int32, (c_len, c_len), 0)
           - lax.broadcasted_iota(jnp.int32, (c_len, c_len), 1))
    rel_f = jnp.maximum(rel, 0).astype(F32)
    lg = [lg_ref[pl.program_id(1) * hpg + i] for i in hs]
    intra = [jnp.where(rel >= 0, jnp.exp(rel_f * x), 0.0) for x in lg]
    q_decay = [jnp.exp((t_col + 1.0) * x) * (RET_DK ** -0.5) for x in lg]
    k_decay = [jnp.exp((c_len - 1.0 - t_col) * x) for x in lg]
    c_decay = [jnp.exp(jnp.full((1, 1), float(c_len), F32) * x) for x in lg]
    gn = gn_ref[...]

    def rot(x, cos, sin):
        x1, x2 = x[:, :half], x[:, half:]
        return jnp.concatenate([x1 * cos - x2 * sin, x1 * sin + x2 * cos], axis=-1)

    def chunk(c, carry):
        rows = pl.ds(pl.multiple_of(c * c_len, c_len), c_len)
        cos = cos_ref[rows, :]
        sin = sin_ref[rows, :]
        q = [rot(q_ref[rows, ksl[i]].astype(F32), cos, sin) for i in hs]
        k = [rot(k_ref[rows, ksl[i]].astype(F32), cos, sin) for i in hs]
        v = [v_ref[rows, vsl[i]] for i in hs]
        s = [lax.dot_general((q[i] * (RET_DK ** -0.5)).astype(BF16), k[i].astype(BF16),
                             (((1,), (1,)), ((), ())), preferred_element_type=F32) * intra[i]
             for i in hs]
        r_old = [r_ref[i] for i in hs]
        o = [jnp.dot(s[i].astype(BF16), v[i], preferred_element_type=F32)
             + jnp.dot((q[i] * q_decay[i]).astype(BF16), r_old[i].astype(BF16),
                       preferred_element_type=F32) for i in hs]
        for i in hs:
            r_ref[i] = r_old[i] * c_decay[i] + lax.dot_general(
                (k[i] * k_decay[i]).astype(BF16), v[i], (((0,), (0,)), ((), ())),
                preferred_element_type=F32)
        for i in hs:
            mu = jnp.mean(o[i], axis=-1, keepdims=True)
            oc = o[i] - mu
            var = jnp.mean(oc * oc, axis=-1, keepdims=True)
            gate = g_ref[rows, vsl[i]].astype(F32)
            y = oc * lax.rsqrt(var + LN_EPS) * gn[:, vsl[i]] * (gate * jax.nn.sigmoid(gate))
            o_ref[rows, vsl[i]] = y.astype(o_ref.dtype)
        return carry

    lax.fori_loop(0, n_chunks, chunk, 0)


def _retention(proj, cos, sin, log_gamma, gn, layer, *, batch, tp, heads, off_q, off_k, off_v,
               off_g):
    mp = proj.shape[0]
    n_chunks_total = tp // RET_CHUNK
    hpg = _pick(heads, (4, 2, 1))
    cpb = _pick(n_chunks_total, (3, 2, 1))
    tb = cpb * RET_CHUNK
    nblk = tp // tb
    gk, gv = hpg * RET_DK, hpg * RET_DV
    assert off_q % gk == 0 and off_k % gk == 0 and off_v % gv == 0 and off_g % gv == 0
    qb, kb, vb, gb = off_q // gk, off_k // gk, off_v // gv, off_g // gv
    row = lambda b, h, j: b * nblk + j
    body = functools.partial(_ret_body, n_chunks=cpb, hpg=hpg)
    return pl.pallas_call(
        body,
        grid=(batch, heads // hpg, nblk),
        in_specs=[
            pl.BlockSpec(memory_space=pltpu.SMEM),
            pl.BlockSpec((tb, gk), lambda b, h, j: (row(b, h, j), qb + h)),
            pl.BlockSpec((tb, gk), lambda b, h, j: (row(b, h, j), kb + h)),
            pl.BlockSpec((tb, gv), lambda b, h, j: (row(b, h, j), vb + h)),
            pl.BlockSpec((tb, gv), lambda b, h, j: (row(b, h, j), gb + h)),
            pl.BlockSpec((tb, RET_DK // 2), lambda b, h, j: (j, 0)),
            pl.BlockSpec((tb, RET_DK // 2), lambda b, h, j: (j, 0)),
            pl.BlockSpec((None, 1, gv), lambda b, h, j: (layer, 0, h)),
        ],
        out_specs=pl.BlockSpec((tb, gv), lambda b, h, j: (row(b, h, j), h)),
        out_shape=jax.ShapeDtypeStruct((mp, heads * RET_DV), BF16),
        scratch_shapes=[pltpu.VMEM((hpg, RET_DK, RET_DV), F32)],
        compiler_params=_params(3, 40),
        name="retention",
    )(log_gamma, proj, proj, proj, proj, cos, sin, gn)


HG_SAFE_RANGE = 85.0


def _hgrn_body(lbl_ref, gn_ref, q_ref, f_ref, i_ref, g_ref, o_ref, s_ref, acc_ref, bx_ref,
               qx_ref, kx_ref, vx_ref, *, layer, n_chunks, heads):
    c_len = HG_CHUNK

    @pl.when(pl.program_id(1) == 0)
    def _():
        s_ref[...] = jnp.zeros_like(s_ref)

    logits = lbl_ref[...]
    e = jnp.exp(logits - jnp.max(logits, axis=0, keepdims=True))
    sm = e / jnp.sum(e, axis=0, keepdims=True)
    lb = jnp.zeros_like(sm[0:1, :])
    for j in range(1, layer + 1):
        lb = lb + sm[j:j + 1, :]
    gn = gn_ref[...]
    tri = (lax.broadcasted_iota(jnp.int32, (c_len, c_len), 0)
           >= lax.broadcasted_iota(jnp.int32, (c_len, c_len), 1)).astype(BF16)
    t_idx = lax.broadcasted_iota(jnp.int32, (c_len, 1), 0)
    causal = (lax.broadcasted_iota(jnp.int32, (c_len, c_len), 0)
              >= lax.broadcasted_iota(jnp.int32, (c_len, c_len), 1))

    def chunk(c, carry):
        rows = pl.ds(pl.multiple_of(c * c_len, c_len), c_len)

        def head_terms(h):
            ks = slice(h * HG_DK, (h + 1) * HG_DK)
            hq = q_ref[rows, ks].astype(F32)
            q = hq * jax.nn.sigmoid(hq)
            f = lb[:, ks] + (1.0 - lb[:, ks]) * jax.nn.sigmoid(f_ref[rows, ks].astype(F32))
            logf = jnp.log(f)
            p0 = logf.astype(BF16)
            r1 = logf - p0.astype(F32)
            p1 = r1.astype(BF16)
            p2 = (r1 - p1.astype(F32)).astype(BF16)
            b = (jnp.dot(tri, p0, preferred_element_type=F32)
                 + jnp.dot(tri, p1, preferred_element_type=F32)
                 + jnp.dot(tri, p2, preferred_element_type=F32))
            return q, 1.0 - f, b

        def write_out(h, o):
            vs = slice(h * HG_DV, (h + 1) * HG_DV)
            gate = g_ref[rows, vs].astype(F32)
            gate = gate * jax.nn.sigmoid(gate)
            y = o * lax.rsqrt(jnp.mean(o * o, axis=-1, keepdims=True) + LN_EPS)
            o_ref[rows, vs] = (y * gn[:, vs] * gate).astype(o_ref.dtype)

        hs = range(heads)
        vsl = [slice(h * HG_DV, (h + 1) * HG_DV) for h in hs]
        terms = [head_terms(h) for h in hs]
        b_last = [t[2][c_len - 1:c_len, :] for t in terms]
        q_in = [t[0] * jnp.exp(t[2]) for t in terms]
        k_out = [t[1] * jnp.exp(bl - t[2]) for t, bl in zip(terms, b_last)]
        un_mid = [jnp.where(bl < -HG_SAFE_RANGE, 0.0, jnp.exp(-0.5 * bl)) for bl in b_last]
        q_mid = [(x * u).astype(BF16) for x, u in zip(q_in, un_mid)]
        k_mid = [(x * u).astype(BF16) for x, u in zip(k_out, un_mid)]
        sc = [lax.dot_general(qm, km, (((1,), (1,)), ((), ())), preferred_element_type=F32)
              for qm, km in zip(q_mid, k_mid)]
        a = [jnp.where(causal, x, 0.0).astype(BF16) for x in sc]
        vh = [i_ref[rows, vs] for vs in vsl]
        st = [s_ref[h] for h in hs]
        o = [jnp.dot(a[h], vh[h], preferred_element_type=F32)
             + lax.dot_general(q_in[h].astype(BF16), st[h].astype(BF16), (((1,), (1,)), ((), ())),
                               preferred_element_type=F32) for h in hs]
        for h in hs:
            s_ref[h] = st[h] * jnp.exp(b_last[h]) + lax.dot_general(
                vh[h], k_out[h].astype(BF16), (((0,), (0,)), ((), ())),
                preferred_element_type=F32)
        b_min = b_last[0]
        for h in hs:
            acc_ref[:, vsl[h]] = o[h]
            write_out(h, o[h])
            b_min = jnp.minimum(b_min, b_last[h])

        @pl.when(jnp.min(b_min) < -HG_SAFE_RANGE)
        def _():
            for h in range(heads):
                ks = slice(h * HG_DK, (h + 1) * HG_DK)
                vs = slice(h * HG_DV, (h + 1) * HG_DV)
                q, k, b = head_terms(h)
                bx_ref[:, ks] = b
                qx_ref[:, ks] = jnp.where(b[c_len - 1:c_len, :] < -HG_SAFE_RANGE, q, 0.0)
                kx_ref[:, ks] = k
                vx_ref[:, vs] = i_ref[rows, vs].astype(F32)

            def pair(s, cc):
                bs = bx_ref[pl.ds(s, 1), :]
                d = jnp.where(t_idx >= s, jnp.exp(jnp.minimum(bx_ref[...] - bs, 0.0)), 0.0)
                p = qx_ref[...] * d * kx_ref[pl.ds(s, 1), :]
                v_s = vx_ref[pl.ds(s, 1), :]
                for h in range(heads):
                    ks = slice(h * HG_DK, (h + 1) * HG_DK)
                    vs = slice(h * HG_DV, (h + 1) * HG_DV)
                    acc_ref[:, vs] += jnp.sum(p[:, ks], axis=-1, keepdims=True) * v_s[:, vs]
                return cc
            lax.fori_loop(0, c_len, pair, 0)
            for h in range(heads):
                write_out(h, acc_ref[:, h * HG_DV:(h + 1) * HG_DV])
        return carry

    lax.fori_loop(0, n_chunks, chunk, 0)


def _hgrn2(proj, lb_logits, gn, layer, *, batch, tp, heads, off_q, off_f, off_i, off_g):
    mp = proj.shape[0]
    hk, hv = heads * HG_DK, heads * HG_DV
    rb = _pick(tp, (704, 384, 256, 128, 64))
    nblk = tp // rb
    depth = lb_logits.shape[0]
    body = functools.partial(_hgrn_body, layer=layer, n_chunks=rb // HG_CHUNK, heads=heads)
    slab = pltpu.VMEM((HG_CHUNK, hv), F32)
    row = lambda b, j: b * nblk + j
    return pl.pallas_call(
        body,
        grid=(batch, nblk),
        in_specs=[
            pl.BlockSpec((depth, hk), lambda b, j: (0, 0)),
            pl.BlockSpec((None, 1, hv), lambda b, j: (layer, 0, 0)),
            pl.BlockSpec((rb, hk), lambda b, j: (row(b, j), off_q // hk)),
            pl.BlockSpec((rb, hk), lambda b, j: (row(b, j), off_f // hk)),
            pl.BlockSpec((rb, hv), lambda b, j: (row(b, j), off_i // hv)),
            pl.BlockSpec((rb, hv), lambda b, j: (row(b, j), off_g // hv)),
        ],
        out_specs=pl.BlockSpec((rb, hv), lambda b, j: (row(b, j), 0)),
        out_shape=jax.ShapeDtypeStruct((mp, hv), BF16),
        scratch_shapes=[pltpu.VMEM((heads, HG_DV, HG_DK), F32), slab, slab, slab, slab, slab],
        compiler_params=_params(2, 48),
        name="hgrn2",
    )(lb_logits, gn, proj, proj, proj, proj)


def _router_body(x_ref, w_ref, idx_ref, wts_ref, *, n_experts):
    x = x_ref[...]
    w = w_ref[...]
    xh = x.astype(BF16)
    xl = (x - xh.astype(F32)).astype(BF16)
    wh = w.astype(BF16)
    wl = (w - wh.astype(F32)).astype(BF16)
    logits = (jnp.dot(xh, wh, preferred_element_type=F32)
              + jnp.dot(xh, wl, preferred_element_type=F32)
              + jnp.dot(xl, wh, preferred_element_type=F32))
    lane = lax.broadcasted_iota(jnp.int32, logits.shape, 1)
    neg = jnp.float32(-jnp.inf)
    logits = jnp.where(lane < n_experts, logits, neg)
    v1 = jnp.max(logits, axis=-1, keepdims=True)
    i1 = jnp.min(jnp.where(logits == v1, lane, ROUTER_LANES), axis=-1, keepdims=True)
    rest = jnp.where(lane == i1, neg, logits)
    v2 = jnp.max(rest, axis=-1, keepdims=True)
    i2 = jnp.min(jnp.where(rest == v2, lane, ROUTER_LANES), axis=-1, keepdims=True)
    e2 = jnp.exp(v2 - v1)
    w1 = 1.0 / (1.0 + e2)
    w2 = e2 / (1.0 + e2)
    idx_ref[...] = jnp.where(lane == 0, i1, jnp.where(lane == 1, i2, 0))
    wts_ref[...] = jnp.where(lane == 0, w1, jnp.where(lane == 1, w2, 0.0))


def _router(h, w_router_padded, layer_idx, n_experts, tp):
    mp, d = h.shape
    rows = _pick(tp, (528, 384, 256, 128, 64, 32, 16))
    body = functools.partial(_router_body, n_experts=n_experts)
    out_spec = pl.BlockSpec((rows, ROUTER_LANES), lambda i: (i, 0))
    return pl.pallas_call(
        body,
        grid=(mp // rows,),
        in_specs=[pl.BlockSpec((rows, d), lambda i: (i, 0)),
                  pl.BlockSpec((None, d, ROUTER_LANES), lambda i: (layer_idx, 0, 0))],
        out_specs=[out_spec, out_spec],
        out_shape=[jax.ShapeDtypeStruct((mp, ROUTER_LANES), jnp.int32),
                   jax.ShapeDtypeStruct((mp, ROUTER_LANES), F32)],
        compiler_params=_params(1, 40),
        name="moe_router",
    )(h, w_router_padded)


MOE_TM = 512
GATHER_ROWS = 256


def _gather_body(src_ref, tv_ref, x_hbm, o_ref, buf, sem, *, rows, steps_per_tile):
    step = pl.program_id(0)

    def start_rows(st, slot):
        @pl.when(tv_ref[st // steps_per_tile] == 1)
        def _():
            def issue(r, c):
                t = src_ref[st * rows + r]
                pltpu.make_async_copy(x_hbm.at[pl.ds(t, 1), :], buf.at[slot, pl.ds(r, 1), :],
                                      sem.at[slot]).start()
                return c
            lax.fori_loop(0, rows, issue, 0, unroll=8)

    @pl.when(step == 0)
    def _():
        start_rows(step, 0)

    @pl.when(step + 1 < pl.num_programs(0))
    def _():
        start_rows(step + 1, lax.rem(step + 1, 2))

    slot = lax.rem(step, 2)
    valid = tv_ref[step // steps_per_tile] == 1

    @pl.when(valid)
    def _():
        pltpu.make_async_copy(x_hbm.at[pl.ds(0, rows), :], buf.at[slot], sem.at[slot]).wait()
        o_ref[...] = buf[slot].astype(o_ref.dtype)

    @pl.when(jnp.logical_not(valid))
    def _():
        o_ref[...] = jnp.zeros_like(o_ref)


def _gather_rows(h, src, tile_valid, *, n_rows):
    mp, d = h.shape
    rows = GATHER_ROWS
    assert n_rows % rows == 0 and MOE_TM % rows == 0
    body = functools.partial(_gather_body, rows=rows, steps_per_tile=MOE_TM // rows)
    return pl.pallas_call(
        body,
        grid_spec=pltpu.PrefetchScalarGridSpec(
            num_scalar_prefetch=2,
            grid=(n_rows // rows,),
            in_specs=[pl.BlockSpec(memory_space=pl.ANY)],
            out_specs=pl.BlockSpec((rows, d), lambda i, src, tv: (i, 0)),
            scratch_shapes=[pltpu.VMEM((2, rows, d), F32), pltpu.SemaphoreType.DMA((2,))],
        ),
        out_shape=jax.ShapeDtypeStruct((n_rows, d), BF16),
        compiler_params=_params(1, 32),
        name="moe_gather",
    )(src, tile_valid, h)


def _gmm_body(te_ref, tv_ref, nx_ref, *refs, n_w, swiglu, wbase, bn):
    it = iter(refs)
    x_ref = next(it)
    w_hbm = [next(it) for _ in range(n_w)]
    o_ref = next(it)
    wb_refs = [next(it) for _ in range(n_w)]
    wf_refs = [next(it) for _ in range(n_w)]
    sem = next(it)
    j = pl.program_id(0)
    i = pl.program_id(1)
    fresh = jnp.logical_or(i == 0, te_ref[i] != te_ref[jnp.maximum(i - 1, 0)])

    def weight_copy(w, expert, col_tile):
        cols = pl.ds(pl.multiple_of(col_tile * bn, bn), bn)
        return pltpu.make_async_copy(w_hbm[w].at[wbase + expert, :, cols], wf_refs[w], sem.at[w])

    @pl.when(fresh)
    def _():
        @pl.when(jnp.logical_and(i == 0, j == 0))
        def _():
            for w in range(n_w):
                weight_copy(w, te_ref[0], 0).start()
        for w in range(n_w):
            weight_copy(w, te_ref[i], j).wait()
            wb_refs[w][...] = wf_refs[w][...].astype(BF16)
        nxt = nx_ref[i]

        @pl.when(nxt >= 0)
        def _():
            for w in range(n_w):
                weight_copy(w, nxt, j).start()

        @pl.when(jnp.logical_and(nxt < 0, j + 1 < pl.num_programs(0)))
        def _():
            for w in range(n_w):
                weight_copy(w, te_ref[0], j + 1).start()

    @pl.when(tv_ref[i] == 1)
    def _():
        x = x_ref[...]
        y = jnp.dot(x, wb_refs[0][...], preferred_element_type=F32)
        if swiglu:
            u = jnp.dot(x, wb_refs[1][...], preferred_element_type=F32)
            y = (y * jax.nn.sigmoid(y)) * u
        o_ref[...] = y.astype(o_ref.dtype)

    @pl.when(tv_ref[i] == 0)
    def _():
        o_ref[...] = jnp.zeros_like(o_ref)


def _grouped_matmul(x, ws, tile_expert, tile_valid, next_expert, wbase, *, bn, out_dtype, swiglu,
                    name):
    m, k = x.shape
    n = ws[0].shape[2]
    tm = MOE_TM
    assert m % tm == 0 and n % bn == 0
    in_specs = [pl.BlockSpec((tm, k), lambda j, i, te, tv, nx: (i, 0))]
    in_specs += [pl.BlockSpec(memory_space=pl.ANY) for _ in ws]
    out_bytes = jnp.dtype(out_dtype).itemsize
    vmem = (2 * tm * k * 2 + len(ws) * k * bn * (4 + 2) + 2 * tm * bn * out_bytes
            + 4 * tm * bn * 4)
    body = functools.partial(_gmm_body, n_w=len(ws), swiglu=swiglu, wbase=wbase, bn=bn)
    return pl.pallas_call(
        body,
        grid_spec=pltpu.PrefetchScalarGridSpec(
            num_scalar_prefetch=3,
            grid=(n // bn, m // tm),
            in_specs=in_specs,
            out_specs=pl.BlockSpec((tm, bn), lambda j, i, te, tv, nx: (i, j)),
            scratch_shapes=([pltpu.VMEM((k, bn), BF16) for _ in ws]
                            + [pltpu.VMEM((k, bn), F32) for _ in ws]
                            + [pltpu.SemaphoreType.DMA((len(ws),))]),
        ),
        out_shape=jax.ShapeDtypeStruct((m, n), out_dtype),
        compiler_params=_params(2, vmem // (1024 * 1024) + 4),
        name=name,
    )(tile_expert, tile_valid, next_expert, x, *ws)


def _combine_ln_body(pos_ref, h_ref, w_ref, g_ref, b_ref, y_hbm, o_ref, ob_ref, buf, sem,
                     *, alpha, rows, tp, pad):
    step = pl.program_id(0)

    def start_rows(st, slot):
        def issue(r, c):
            for s in range(TOP_K):
                p = pos_ref[(st * rows + r) * TOP_K + s]
                pltpu.make_async_copy(y_hbm.at[pl.ds(p, 1), :], buf.at[slot, s, pl.ds(r, 1), :],
                                      sem.at[slot]).start()
            return c
        lax.fori_loop(0, rows, issue, 0, unroll=4)

    @pl.when(step == 0)
    def _():
        start_rows(step, 0)

    @pl.when(step + 1 < pl.num_programs(0))
    def _():
        start_rows(step + 1, lax.rem(step + 1, 2))

    slot = lax.rem(step, 2)
    for s in range(TOP_K):
        pltpu.make_async_copy(y_hbm.at[pl.ds(0, rows), :], buf.at[slot, s], sem.at[slot]).wait()

    w = w_ref[...]
    lane = lax.broadcasted_iota(jnp.int32, w.shape, 1)
    f = None
    for s in range(TOP_K):
        ws = jnp.sum(jnp.where(lane == s, w, 0.0), axis=-1, keepdims=True)
        term = ws * buf[slot, s]
        f = term if f is None else f + term
    y = _layernorm_rows(alpha * h_ref[...] + f, g_ref[...], b_ref[...], rows=rows, tp=tp, pad=pad)
    o_ref[...] = y
    ob_ref[...] = y.astype(BF16)


def _combine_ln(h, ys, pos, wts, g, b, layer, *, alpha, tp, pad):
    mp, d = h.shape
    rows = _pick(tp, (384, 256, 128, 64, 32, 16))
    body = functools.partial(_combine_ln_body, alpha=alpha, rows=rows, tp=tp, pad=pad)
    row_spec = pl.BlockSpec((rows, d), lambda i, p: (i, 0))
    par_spec = pl.BlockSpec((None, 1, d), lambda i, p: (layer, 0, 0))
    return pl.pallas_call(
        body,
        grid_spec=pltpu.PrefetchScalarGridSpec(
            num_scalar_prefetch=1,
            grid=(mp // rows,),
            in_specs=[row_spec, pl.BlockSpec((rows, ROUTER_LANES), lambda i, p: (i, 0)),
                      par_spec, par_spec, pl.BlockSpec(memory_space=pl.ANY)],
            out_specs=[row_spec, row_spec],
            scratch_shapes=[pltpu.VMEM((2, TOP_K, rows, d), F32), pltpu.SemaphoreType.DMA((2,))],
        ),
        out_shape=[jax.ShapeDtypeStruct((mp, d), F32), jax.ShapeDtypeStruct((mp, d), BF16)],
        compiler_params=_params(1, 40),
        name="moe_combine_layernorm",
    )(pos, h, wts, g, b, ys)


def _route(idx, *, n_experts, tp, pad):
    mp = idx.shape[0]
    tm = MOE_TM
    n_assign = mp * TOP_K
    n_tiles = -(-n_assign // tm) + n_experts
    n_rows = n_tiles * tm
    token = jnp.arange(mp, dtype=jnp.int32)
    real = (token % tp) >= pad
    e = jnp.where(real[:, None], idx[:, :TOP_K], n_experts).reshape(n_assign)
    onehot = (e[:, None] == jnp.arange(n_experts, dtype=jnp.int32)[None, :]).astype(jnp.int32)
    csum = jnp.cumsum(onehot, axis=0)
    counts = csum[-1]
    rank = jnp.sum(onehot * csum, axis=1) - 1
    gsz = ((counts + tm - 1) // tm) * tm
    gend = jnp.cumsum(gsz)
    gstart = gend - gsz
    routed = e < n_experts
    dst = jnp.where(routed, gstart[jnp.minimum(e, n_experts - 1)] + rank, n_rows)
    assign_token = jnp.arange(n_assign, dtype=jnp.int32) // TOP_K
    src = jnp.zeros((n_rows,), jnp.int32).at[dst].set(assign_token, mode="drop")
    pos = jnp.where(routed, dst, 0).astype(jnp.int32)
    tile_start = jnp.arange(n_tiles, dtype=jnp.int32) * tm
    tile_valid = tile_start < gend[-1]
    tile_expert = jnp.sum((tile_start[:, None] >= gend[None, :]).astype(jnp.int32), axis=1)
    last_expert = jnp.max(jnp.where(tile_valid, tile_expert, 0))
    tile_expert = jnp.where(tile_valid, tile_expert, last_expert)
    later = jnp.arange(n_tiles)[None, :] > jnp.arange(n_tiles)[:, None]
    differs = jnp.logical_and(later, tile_expert[None, :] != tile_expert[:, None])
    first = jnp.argmax(differs, axis=1)
    next_expert = jnp.where(jnp.any(differs, axis=1), tile_expert[first], -1).astype(jnp.int32)
    return src, pos, tile_expert, tile_valid.astype(jnp.int32), next_expert, n_rows


def kernel(x, meta_tokens, w_in, ret_gn_g, hg_norm_g, hg_lb_logits, w_ret_out, w_hg_out, w_o,
           ln1_g, ln1_b, ln2_g, ln2_b, ffn_w_gate, ffn_w_up, ffn_w_down,
           moe_router, moe_w_gate, moe_w_up, moe_w_down):
    batch, seq, d = x.shape
    n_meta = meta_tokens.shape[0]
    depth, _, p_in = w_in.shape
    ret_v = w_ret_out.shape[1]
    hg_v = w_hg_out.shape[1]
    hg_k = hg_lb_logits.shape[1]
    ret_qk = (p_in - 2 * ret_v - 2 * hg_k - 2 * hg_v - 2 * d) // 2
    ret_heads = ret_qk // RET_DK
    hg_heads = hg_k // HG_DK
    n_experts = moe_router.shape[2]
    assert ret_v == ret_heads * RET_DV and hg_v == hg_heads * HG_DV and hg_k == hg_v
    assert n_experts <= ROUTER_LANES

    pad = RET_CHUNK - n_meta
    tp = pad + n_meta + seq
    assert tp % RET_CHUNK == 0 and tp % HG_CHUNK == 0
    mp = batch * tp
    alpha = (2 * depth) ** 0.25

    off = {}
    acc = 0
    for nm, wd in (("rq", ret_qk), ("rk", ret_qk), ("rv", ret_v), ("rg", ret_v), ("hq", hg_k),
                   ("hf", hg_k), ("hi", hg_v), ("hg", hg_v), ("ga", d), ("gb", d)):
        off[nm] = acc
        acc += wd

    meta = jnp.broadcast_to(meta_tokens[None].astype(F32), (batch, n_meta, d))
    h = jnp.concatenate([jnp.zeros((batch, pad, d), F32), meta, x.astype(F32)], axis=1)
    h = h.reshape(mp, d)
    hb = h.astype(BF16)

    pos = jnp.arange(tp, dtype=F32) - float(pad)
    inv = 1.0 / (ROPE_BASE ** jnp.linspace(0.0, 1.0, RET_DK // 2, dtype=F32))
    ang = pos[:, None] * inv[None, :]
    cos, sin = jnp.cos(ang), jnp.sin(ang)
    log_gamma = jnp.log1p(-jnp.exp2(-5.0 - jnp.arange(ret_heads, dtype=F32)))

    ret_gn3 = ret_gn_g.astype(F32).reshape(depth, 1, ret_v)
    hg_gn3 = hg_norm_g.astype(F32).reshape(depth, 1, hg_v)
    ln1_g3, ln1_b3 = ln1_g.reshape(depth, 1, d), ln1_b.reshape(depth, 1, d)
    ln2_g3, ln2_b3 = ln2_g.reshape(depth, 1, d), ln2_b.reshape(depth, 1, d)
    n_moe = moe_router.shape[0]
    d_ffe = moe_w_gate.shape[3]
    router_w = jnp.pad(moe_router.astype(F32), ((0, 0), (0, 0), (0, ROUTER_LANES - n_experts)))
    moe_wg = moe_w_gate.reshape(n_moe * n_experts, d, d_ffe)
    moe_wu = moe_w_up.reshape(n_moe * n_experts, d, d_ffe)
    moe_wd = moe_w_down.reshape(n_moe * n_experts, d_ffe, d)

    bm = _pick(mp, (1056, 768, 512, 384, 256, 128, 64, 32, 16))
    bm_half = _pick(mp, (528, 384, 256, 128, 64, 32, 16))
    tile = lambda n, cands: _pick(n, cands)

    for l in range(depth):
        proj = _matmul_rows_resident(hb, w_in, l, rows=tp, bn=tile(p_in, (512, 256, 128)),
                                     out_dtype=BF16, name="in_proj")
        yr = _retention(proj, cos, sin, log_gamma, ret_gn3, l, batch=batch, tp=tp,
                        heads=ret_heads, off_q=off["rq"], off_k=off["rk"], off_v=off["rv"],
                        off_g=off["rg"])
        yh = _hgrn2(proj, hg_lb_logits.astype(F32), hg_gn3, l, batch=batch, tp=tp,
                    heads=hg_heads, off_q=off["hq"], off_f=off["hf"], off_i=off["hi"],
                    off_g=off["hg"])
        bn_o = tile(d, (512, 256, 128))
        t_r = _matmul(yr, [w_ret_out], l, bm=bm, bn=bn_o, out_dtype=F32,
                      gate=(proj, off["ga"]), name="ret_out_proj")
        merged = _matmul_gate_add(yh, w_hg_out, l, (proj, off["gb"]), t_r, tp=tp,
                                  name="hg_out_proj")
        h, hb = _matmul_add_ln(merged, w_o, l, h, ln1_g3, ln1_b3, alpha=alpha, tp=tp, pad=pad)
        if l % 2 == 0:
            d_ff = ffn_w_gate.shape[2]
            act = _matmul(hb, [ffn_w_gate, ffn_w_up], l // 2, bm=bm,
                          bn=tile(d_ff, (512, 256, 128)), out_dtype=BF16, swiglu=True,
                          name="ffn_up")
            f = _matmul(act, [ffn_w_down], l // 2, bm=bm_half, bn=tile(d, (512, 256, 128)),
                        out_dtype=F32, name="ffn_down")
            h, hb = _add_ln(h, f, ln2_g3, ln2_b3, l, alpha=alpha, tp=tp, pad=pad)
        else:
            idx, wts = _router(h, router_w, l // 2, n_experts, tp)
            src, pos, tile_expert, tile_valid, next_expert, n_rows = _route(
                idx, n_experts=n_experts, tp=tp, pad=pad)
            xs = _gather_rows(h, src, tile_valid, n_rows=n_rows)
            wbase = (l // 2) * n_experts
            act = _grouped_matmul(xs, [moe_wg, moe_wu], tile_expert, tile_valid, next_expert, wbase,
                                  bn=tile(d_ffe, (1408, 256, 128)), out_dtype=BF16, swiglu=True,
                                  name="moe_up")
            ys = _grouped_matmul(act, [moe_wd], tile_expert, tile_valid, next_expert, wbase,
                                 bn=tile(d, (1024, 512, 256, 128)), out_dtype=F32, swiglu=False,
                                 name="moe_down")
            h, hb = _combine_ln(h, ys, pos, wts, ln2_g3, ln2_b3, l, alpha=alpha, tp=tp, pad=pad)

    out = h.reshape(batch, tp, d)[:, pad + n_meta:, :]
    return out.astype(x.dtype)
```

```python
import functools

import jax
import jax.numpy as jnp
from jax import lax
from jax.experimental import pallas as pl
from jax.experimental.pallas import tpu as pltpu

F32 = jnp.float32
BF16 = jnp.bfloat16

RET_DK = 256
RET_DV = 512
RET_CHUNK = 128
HG_DK = 128
HG_DV = 128
HG_CHUNK = 64
TOP_K = 2
ROPE_BASE = 10000.0
LN_EPS = 1e-5
ROUTER_LANES = 128
VMEM_LIMIT_CAP_MB = 56


def _pick(n, candidates):
    for c in candidates:
        if n % c == 0:
            return c
    raise ValueError(f"no tile in {candidates} divides {n}")


def _params(n_axes, vmem_mb):
    return pltpu.CompilerParams(
        dimension_semantics=("arbitrary",) * n_axes,
        vmem_limit_bytes=min(vmem_mb, VMEM_LIMIT_CAP_MB) * 1024 * 1024)


def _mm_body(x_ref, w_ref, o_ref, wb_ref):
    @pl.when(pl.program_id(1) == 0)
    def _():
        wb_ref[...] = w_ref[...].astype(BF16)

    o_ref[...] = jnp.dot(x_ref[...], wb_ref[...], preferred_element_type=F32).astype(o_ref.dtype)


def _matmul(x, w, widx, *, bm, bn, out_dtype, name):
    m, k = x.shape
    n = w.shape[2]
    assert m % bm == 0 and n % bn == 0, (m, bm, n, bn)
    out_bytes = jnp.dtype(out_dtype).itemsize
    vmem = 2 * bm * k * 2 + k * bn * (2 * 4 + 2) + 2 * bm * bn * out_bytes + 6 * bm * bn * 4
    return pl.pallas_call(
        _mm_body,
        grid=(n // bn, m // bm),
        in_specs=[pl.BlockSpec((bm, k), lambda j, i: (i, 0)),
                  pl.BlockSpec((None, k, bn), lambda j, i: (widx, 0, j))],
        out_specs=pl.BlockSpec((bm, bn), lambda j, i: (i, j)),
        out_shape=jax.ShapeDtypeStruct((m, n), out_dtype),
        scratch_shapes=[pltpu.VMEM((k, bn), BF16)],
        compiler_params=_params(2, vmem // (1024 * 1024) + 4),
        name=name,
    )(x, w)


def _mm_xres_body(*refs, n_w, sub):
    x_ref, w_refs, o_ref = refs[0], refs[1:1 + n_w], refs[1 + n_w]
    wbs = [w_ref[...].astype(BF16) for w_ref in w_refs]
    for r0 in range(0, x_ref.shape[0], sub):
        rs = slice(r0, r0 + sub)
        y = jnp.dot(x_ref[rs, :], wbs[0], preferred_element_type=F32)
        if n_w == 2:
            u = jnp.dot(x_ref[rs, :], wbs[1], preferred_element_type=F32)
            y = (y * jax.nn.sigmoid(y)) * u
        o_ref[rs, :] = y.astype(o_ref.dtype)


def _matmul_rows_resident(x, ws, widx, *, rows, bn, out_dtype, name):
    m, k = x.shape
    n = ws[0].shape[2]
    assert m % rows == 0 and n % bn == 0
    sub = _pick(rows, (1056, 768, 512, 384, 256, 128, 64, 32, 16))
    out_bytes = jnp.dtype(out_dtype).itemsize
    vmem = (rows * k * 2 + len(ws) * k * bn * (2 * 4 + 2) + 2 * rows * bn * out_bytes
            + (2 + len(ws)) * sub * bn * 4)
    return pl.pallas_call(
        functools.partial(_mm_xres_body, n_w=len(ws), sub=sub),
        grid=(m // rows, n // bn),
        in_specs=([pl.BlockSpec((rows, k), lambda b, j: (b, 0), pipeline_mode=pl.Buffered(1))]
                  + [pl.BlockSpec((None, k, bn), lambda b, j: (widx, 0, j)) for _ in ws]),
        out_specs=pl.BlockSpec((rows, bn), lambda b, j: (b, j)),
        out_shape=jax.ShapeDtypeStruct((m, n), out_dtype),
        compiler_params=_params(2, vmem // (1024 * 1024) + 6),
        name=name,
    )(x, *ws)


def _ln_body(h_ref, m_ref, g_ref, b_ref, o_ref, ob_ref, *, alpha, rows, tp, pad):
    y = _layernorm_rows(alpha * h_ref[...] + m_ref[...], g_ref[...], b_ref[...],
                        rows=rows, tp=tp, pad=pad)
    o_ref[...] = y
    ob_ref[...] = y.astype(BF16)


def _layernorm_rows(z, g, b, *, rows, tp, pad, row0=0):
    mu = jnp.mean(z, axis=-1, keepdims=True)
    zc = z - mu
    var = jnp.mean(zc * zc, axis=-1, keepdims=True)
    y = zc * lax.rsqrt(var + LN_EPS) * g + b
    blk = lax.rem(pl.program_id(0), tp // rows)
    r = blk * rows + row0 + lax.broadcasted_iota(jnp.int32, (z.shape[0], 1), 0)
    return jnp.where(r >= pad, y, 0.0)


RESIDENT_ROWS = 384
RESIDENT_SUB = 128


def _mm_ln_body(x_ref, w_ref, h_ref, g_ref, b_ref, o_ref, ob_ref, wb_ref, *, alpha, rows, sub, tp,
                pad):
    @pl.when(pl.program_id(0) == 0)
    def _():
        wb_ref[...] = w_ref[...].astype(BF16)

    for r0 in range(0, rows, sub):
        rs = slice(r0, r0 + sub)
        m = jnp.dot(x_ref[rs, :], wb_ref[...], preferred_element_type=F32)
        y = _layernorm_rows(alpha * h_ref[rs, :] + m, g_ref[...], b_ref[...], rows=rows, tp=tp,
                            pad=pad, row0=r0)
        o_ref[rs, :] = y
        ob_ref[rs, :] = y.astype(BF16)


def _mm_gate_add_body(x_ref, w_ref, gate_ref, add_ref, o_ref, wb_ref, *, rows, sub):
    @pl.when(pl.program_id(0) == 0)
    def _():
        wb_ref[...] = w_ref[...].astype(BF16)

    for r0 in range(0, rows, sub):
        rs = slice(r0, r0 + sub)
        m = jnp.dot(x_ref[rs, :], wb_ref[...], preferred_element_type=F32)
        y = add_ref[rs, :] + jax.nn.sigmoid(gate_ref[rs, :].astype(F32)) * m
        o_ref[rs, :] = y.astype(o_ref.dtype)


def _resident_specs(x, w, layer, tp):
    mp, k = x.shape
    d = w.shape[2]
    rows = _pick(tp, (RESIDENT_ROWS, 256, 128, 64, 32, 16))
    sub = _pick(rows, (RESIDENT_SUB, 64, 32, 16))
    x_spec = pl.BlockSpec((rows, k), lambda i: (i, 0))
    w_spec = pl.BlockSpec((None, k, d), lambda i: (layer, 0, 0), pipeline_mode=pl.Buffered(1))
    return mp, k, d, rows, sub, x_spec, w_spec


def _matmul_add_ln(x, w, layer, h, g, b, *, alpha, tp, pad):
    mp, k, d, rows, sub, x_spec, w_spec = _resident_specs(x, w, layer, tp)
    body = functools.partial(_mm_ln_body, alpha=alpha, rows=rows, sub=sub, tp=tp, pad=pad)
    row_spec = pl.BlockSpec((rows, d), lambda i: (i, 0))
    par_spec = pl.BlockSpec((None, 1, d), lambda i: (layer, 0, 0))
    vmem = k * d * (4 + 2) + 2 * rows * k * 2 + rows * d * (2 * 4 + 2 * 4 + 2 * 2 + 3 * 4)
    return pl.pallas_call(
        body,
        grid=(mp // rows,),
        in_specs=[x_spec, w_spec, row_spec, par_spec, par_spec],
        out_specs=[row_spec, row_spec],
        out_shape=[jax.ShapeDtypeStruct((mp, d), F32), jax.ShapeDtypeStruct((mp, d), BF16)],
        scratch_shapes=[pltpu.VMEM((k, d), BF16)],
        compiler_params=_params(1, vmem // (1024 * 1024) + 4),
        name="mix_out_proj_layernorm",
    )(x, w, h, g, b)


def _matmul_gate_add(x, w, layer, gate, add, *, tp, name):
    mp, k, d, rows, sub, x_spec, w_spec = _resident_specs(x, w, layer, tp)
    garr, goff = gate
    assert goff % d == 0
    gblk = goff // d
    body = functools.partial(_mm_gate_add_body, rows=rows, sub=sub)
    row_spec = pl.BlockSpec((rows, d), lambda i: (i, 0))
    vmem = k * d * (4 + 2) + 2 * rows * k * 2 + rows * d * (2 * 2 + 2 * 4 + 2 * 2 + 3 * 4)
    return pl.pallas_call(
        body,
        grid=(mp // rows,),
        in_specs=[x_spec, w_spec, pl.BlockSpec((rows, d), lambda i: (i, gblk)), row_spec],
        out_specs=row_spec,
        out_shape=jax.ShapeDtypeStruct((mp, d), BF16),
        scratch_shapes=[pltpu.VMEM((k, d), BF16)],
        compiler_params=_params(1, vmem // (1024 * 1024) + 4),
        name=name,
    )(x, w, garr, add)


def _add_ln(h, m, g, b, layer, *, alpha, tp, pad):
    mp, d = h.shape
    rows = _pick(tp, (528, 384, 256, 128, 64, 32, 16))
    body = functools.partial(_ln_body, alpha=alpha, rows=rows, tp=tp, pad=pad)
    row_spec = pl.BlockSpec((rows, d), lambda i: (i, 0))
    par_spec = pl.BlockSpec((None, 1, d), lambda i: (layer, 0, 0))
    return pl.pallas_call(
        body,
        grid=(mp // rows,),
        in_specs=[row_spec, row_spec, par_spec, par_spec],
        out_specs=[row_spec, row_spec],
        out_shape=[jax.ShapeDtypeStruct((mp, d), F32), jax.ShapeDtypeStruct((mp, d), BF16)],
        compiler_params=_params(1, 40),
        name="add_layernorm",
    )(h, m, g, b)


def _ret_body(lg_ref, q_ref, k_ref, v_ref, g_ref, cos_ref, sin_ref, gn_ref, o_ref, r_ref,
              *, n_chunks, hpg):
    c_len = RET_CHUNK
    half = RET_DK // 2
    hs = range(hpg)
    ksl = [slice(i * RET_DK, (i + 1) * RET_DK) for i in hs]
    vsl = [slice(i * RET_DV, (i + 1) * RET_DV) for i in hs]

    @pl.when(pl.program_id(2) == 0)
    def _():
        r_ref[...] = jnp.zeros_like(r_ref)

    t_col = lax.broadcasted_iota(jnp.int32, (c_len, 1), 0).astype(F32)
    rel = (lax.broadcasted_iota(jnp.int32, (c_len, c_len), 0)
           - lax.broadcasted_iota(jnp.int32, (c_len, c_len), 1))
    rel_f = jnp.maximum(rel, 0).astype(F32)
    lg = [lg_ref[pl.program_id(1) * hpg + i] for i in hs]
    intra = [jnp.where(rel >= 0, jnp.exp(rel_f * x), 0.0) for x in lg]
    q_decay = [jnp.exp((t_col + 1.0) * x) * (RET_DK ** -0.5) for x in lg]
    k_decay = [jnp.exp((c_len - 1.0 - t_col) * x) for x in lg]
    c_decay = [jnp.exp(jnp.full((1, 1), float(c_len), F32) * x) for x in lg]
    gn = gn_ref[...]

    def rot(x, cos, sin):
        x1, x2 = x[:, :half], x[:, half:]
        return jnp.concatenate([x1 * cos - x2 * sin, x1 * sin + x2 * cos], axis=-1)

    def chunk(c, carry):
        rows = pl.ds(pl.multiple_of(c * c_len, c_len), c_len)
        cos = cos_ref[rows, :]
        sin = sin_ref[rows, :]
        q = [rot(q_ref[rows, ksl[i]].astype(F32), cos, sin) for i in hs]
        k = [rot(k_ref[rows, ksl[i]].astype(F32), cos, sin) for i in hs]
        v = [v_ref[rows, vsl[i]] for i in hs]
        s = [lax.dot_general((q[i] * (RET_DK ** -0.5)).astype(BF16), k[i].astype(BF16),
                             (((1,), (1,)), ((), ())), preferred_element_type=F32) * intra[i]
             for i in hs]
        r_old = [r_ref[i] for i in hs]
        o = [jnp.dot(s[i].astype(BF16), v[i], preferred_element_type=F32)
             + jnp.dot((q[i] * q_decay[i]).astype(BF16), r_old[i].astype(BF16),
                       preferred_element_type=F32) for i in hs]
        for i in hs:
            r_ref[i] = r_old[i] * c_decay[i] + lax.dot_general(
                (k[i] * k_decay[i]).astype(BF16), v[i], (((0,), (0,)), ((), ())),
                preferred_element_type=F32)
        for i in hs:
            mu = jnp.mean(o[i], axis=-1, keepdims=True)
            oc = o[i] - mu
            var = jnp.mean(oc * oc, axis=-1, keepdims=True)
            gate = g_ref[rows, vsl[i]].astype(F32)
            y = oc * lax.rsqrt(var + LN_EPS) * gn[:, vsl[i]] * (gate * jax.nn.sigmoid(gate))
            o_ref[rows, vsl[i]] = y.astype(o_ref.dtype)
        return carry

    lax.fori_loop(0, n_chunks, chunk, 0)


def _retention(proj, cos, sin, log_gamma, gn, layer, *, batch, tp, heads, off_q, off_k, off_v,
               off_g):
    mp = proj.shape[0]
    n_chunks_total = tp // RET_CHUNK
    hpg = _pick(heads, (4, 2, 1))
    cpb = _pick(n_chunks_total, (3, 2, 1))
    tb = cpb * RET_CHUNK
    nblk = tp // tb
    gk, gv = hpg * RET_DK, hpg * RET_DV
    assert off_q % gk == 0 and off_k % gk == 0 and off_v % gv == 0 and off_g % gv == 0
    qb, kb, vb, gb = off_q // gk, off_k // gk, off_v // gv, off_g // gv
    row = lambda b, h, j: b * nblk + j
    body = functools.partial(_ret_body, n_chunks=cpb, hpg=hpg)
    return pl.pallas_call(
        body,
        grid=(batch, heads // hpg, nblk),
        in_specs=[
            pl.BlockSpec(memory_space=pltpu.SMEM),
            pl.BlockSpec((tb, gk), lambda b, h, j: (row(b, h, j), qb + h)),
            pl.BlockSpec((tb, gk), lambda b, h, j: (row(b, h, j), kb + h)),
            pl.BlockSpec((tb, gv), lambda b, h, j: (row(b, h, j), vb + h)),
            pl.BlockSpec((tb, gv), lambda b, h, j: (row(b, h, j), gb + h)),
            pl.BlockSpec((tb, RET_DK // 2), lambda b, h, j: (j, 0)),
            pl.BlockSpec((tb, RET_DK // 2), lambda b, h, j: (j, 0)),
            pl.BlockSpec((None, 1, gv), lambda b, h, j: (layer, 0, h)),
        ],
        out_specs=pl.BlockSpec((tb, gv), lambda b, h, j: (row(b, h, j), h)),
        out_shape=jax.ShapeDtypeStruct((mp, heads * RET_DV), BF16),
        scratch_shapes=[pltpu.VMEM((hpg, RET_DK, RET_DV), F32)],
        compiler_params=_params(3, 40),
        name="retention",
    )(log_gamma, proj, proj, proj, proj, cos, sin, gn)


HG_SAFE_RANGE = 85.0


FUSED_ROWS = 192
STAGE_ROWS = 512


def _hgrn_retout_body(lbl_ref, gn_ref, q_ref, f_ref, i_ref, g_ref, yr_ref, ga_ref, w_hbm,
                      o_ref, t_ref, s_ref, acc_ref, bx_ref, qx_ref, kx_ref, vx_ref, wb_ref,
                      wf_ref, sem, *, layer, n_chunks, heads):
    c_len = HG_CHUNK
    first = jnp.logical_and(pl.program_id(0) == 0, pl.program_id(1) == 0)

    @pl.when(first)
    def _():
        n_stage = wb_ref.shape[0] // STAGE_ROWS

        def stage_copy(c, slot):
            return pltpu.make_async_copy(
                w_hbm.at[layer, pl.ds(c * STAGE_ROWS, STAGE_ROWS), :], wf_ref.at[slot],
                sem.at[slot])
        stage_copy(0, 0).start()
        for c in range(n_stage):
            if c + 1 < n_stage:
                stage_copy(c + 1, (c + 1) % 2).start()
            stage_copy(c, c % 2).wait()
            wb_ref[c * STAGE_ROWS:(c + 1) * STAGE_ROWS, :] = wf_ref[c % 2].astype(BF16)

    @pl.when(pl.program_id(1) == 0)
    def _():
        s_ref[...] = jnp.zeros_like(s_ref)

    logits = lbl_ref[...]
    e = jnp.exp(logits - jnp.max(logits, axis=0, keepdims=True))
    sm = e / jnp.sum(e, axis=0, keepdims=True)
    lb = jnp.zeros_like(sm[0:1, :])
    for j in range(1, layer + 1):
        lb = lb + sm[j:j + 1, :]
    gn = gn_ref[...]
    tri = (lax.broadcasted_iota(jnp.int32, (c_len, c_len), 0)
           >= lax.broadcasted_iota(jnp.int32, (c_len, c_len), 1)).astype(BF16)
    t_idx = lax.broadcasted_iota(jnp.int32, (c_len, 1), 0)
    causal = (lax.broadcasted_iota(jnp.int32, (c_len, c_len), 0)
              >= lax.broadcasted_iota(jnp.int32, (c_len, c_len), 1))
    hs = range(heads)
    ksl = [slice(h * HG_DK, (h + 1) * HG_DK) for h in hs]
    vsl = [slice(h * HG_DV, (h + 1) * HG_DV) for h in hs]

    def head_terms(rows, h):
        hq = q_ref[rows, ksl[h]].astype(F32)
        q = hq * jax.nn.sigmoid(hq)
        f = lb[:, ksl[h]] + (1.0 - lb[:, ksl[h]]) * jax.nn.sigmoid(f_ref[rows, ksl[h]].astype(F32))
        logf = jnp.log(f)
        p0 = logf.astype(BF16)
        r1 = logf - p0.astype(F32)
        p1 = r1.astype(BF16)
        p2 = (r1 - p1.astype(F32)).astype(BF16)
        b = (jnp.dot(tri, p0, preferred_element_type=F32)
             + jnp.dot(tri, p1, preferred_element_type=F32)
             + jnp.dot(tri, p2, preferred_element_type=F32))
        return q, 1.0 - f, b

    def write_out(rows, h, o):
        gate = g_ref[rows, vsl[h]].astype(F32)
        gate = gate * jax.nn.sigmoid(gate)
        y = o * lax.rsqrt(jnp.mean(o * o, axis=-1, keepdims=True) + LN_EPS)
        o_ref[rows, vsl[h]] = (y * gn[:, vsl[h]] * gate).astype(o_ref.dtype)

    n_rows = n_chunks * c_len
    col_tile = 256
    pending = [slice(c0, c0 + col_tile) for c0 in range(0, t_ref.shape[1], col_tile)]

    def project_some(n=1):
        for _ in range(n):
            if pending:
                cs = pending.pop(0)
                m = jnp.dot(yr_ref[...], wb_ref[:, cs], preferred_element_type=F32)
                t_ref[:, cs] = jax.nn.sigmoid(ga_ref[:, cs].astype(F32)) * m

    def chunk_main(rows):
        terms = []
        for h in hs:
            terms.append(head_terms(rows, h))
            if h % 8 == 7:
                project_some()
        b_last = [t[2][c_len - 1:c_len, :] for t in terms]
        q_in = [t[0] * jnp.exp(t[2]) for t in terms]
        k_out = [t[1] * jnp.exp(bl - t[2]) for t, bl in zip(terms, b_last)]
        un_mid = [jnp.where(bl < -HG_SAFE_RANGE, 0.0, jnp.exp(-0.5 * bl)) for bl in b_last]
        q_mid = [(x * u).astype(BF16) for x, u in zip(q_in, un_mid)]
        k_mid = [(x * u).astype(BF16) for x, u in zip(k_out, un_mid)]
        sc = [lax.dot_general(qm, km, (((1,), (1,)), ((), ())), preferred_element_type=F32)
              for qm, km in zip(q_mid, k_mid)]
        a = [jnp.where(causal, x, 0.0).astype(BF16) for x in sc]
        project_some()
        vh = [i_ref[rows, vs] for vs in vsl]
        st = [s_ref[h] for h in hs]
        o = [jnp.dot(a[h], vh[h], preferred_element_type=F32)
             + lax.dot_general(q_in[h].astype(BF16), st[h].astype(BF16), (((1,), (1,)), ((), ())),
                               preferred_element_type=F32) for h in hs]
        for h in hs:
            s_ref[h] = st[h] * jnp.exp(b_last[h]) + lax.dot_general(
                vh[h], k_out[h].astype(BF16), (((0,), (0,)), ((), ())),
                preferred_element_type=F32)
        b_min = b_last[0]
        for h in hs:
            acc_ref[rows, vsl[h]] = o[h]
            write_out(rows, h, o[h])
            b_min = jnp.minimum(b_min, b_last[h])
        return b_min

    step_min = None
    for c in range(n_chunks):
        b_min = chunk_main(slice(c * c_len, (c + 1) * c_len))
        step_min = b_min if step_min is None else jnp.minimum(step_min, b_min)
    project_some(len(pending))

    @pl.when(jnp.min(step_min) < -HG_SAFE_RANGE)
    def _():
        def chunk_exact(c, carry):
            rows = pl.ds(pl.multiple_of(c * c_len, c_len), c_len)
            for h in hs:
                q, k, b = head_terms(rows, h)
                bx_ref[:, ksl[h]] = b
                qx_ref[:, ksl[h]] = jnp.where(b[c_len - 1:c_len, :] < -HG_SAFE_RANGE, q, 0.0)
                kx_ref[:, ksl[h]] = k
                vx_ref[:, vsl[h]] = i_ref[rows, vsl[h]].astype(F32)

            def pair(s, cc):
                bs = bx_ref[pl.ds(s, 1), :]
                d = jnp.where(t_idx >= s, jnp.exp(jnp.minimum(bx_ref[...] - bs, 0.0)), 0.0)
                p = qx_ref[...] * d * kx_ref[pl.ds(s, 1), :]
                v_s = vx_ref[pl.ds(s, 1), :]
                for h in hs:
                    acc_ref[rows, vsl[h]] += (jnp.sum(p[:, ksl[h]], axis=-1, keepdims=True)
                                              * v_s[:, vsl[h]])
                return cc
            lax.fori_loop(0, c_len, pair, 0)
            for h in hs:
                write_out(rows, h, acc_ref[rows, vsl[h]])
            return carry
        lax.fori_loop(0, n_chunks, chunk_exact, 0)


def _hgrn2_with_ret_out(proj, lb_logits, gn, yr, w_ret, layer, *, batch, tp, heads, off_q, off_f,
                        off_i, off_g, off_ga):
    mp = proj.shape[0]
    hk, hv = heads * HG_DK, heads * HG_DV
    kr, d = w_ret.shape[1], w_ret.shape[2]
    rb = _pick(tp, (FUSED_ROWS, 128, 64))
    nblk = tp // rb
    depth = lb_logits.shape[0]
    assert kr % STAGE_ROWS == 0 and off_ga % d == 0
    body = functools.partial(_hgrn_retout_body, layer=layer, n_chunks=rb // HG_CHUNK, heads=heads)
    slab = pltpu.VMEM((HG_CHUNK, hv), F32)
    row = lambda b, j: b * nblk + j
    vmem = (kr * d * 2 + 2 * STAGE_ROWS * d * 4 + rb * hv * 4 + 4 * HG_CHUNK * hv * 4
            + heads * HG_DV * HG_DK * 4
            + 2 * rb * (2 * hk * 2 + 2 * hv * 2 + kr * 2 + d * 2 + hv * 2 + d * 4) + 3 * rb * d * 4)
    return pl.pallas_call(
        body,
        grid=(batch, nblk),
        in_specs=[
            pl.BlockSpec((depth, hk), lambda b, j: (0, 0)),
            pl.BlockSpec((None, 1, hv), lambda b, j: (layer, 0, 0)),
            pl.BlockSpec((rb, hk), lambda b, j: (row(b, j), off_q // hk)),
            pl.BlockSpec((rb, hk), lambda b, j: (row(b, j), off_f // hk)),
            pl.BlockSpec((rb, hv), lambda b, j: (row(b, j), off_i // hv)),
            pl.BlockSpec((rb, hv), lambda b, j: (row(b, j), off_g // hv)),
            pl.BlockSpec((rb, kr), lambda b, j: (row(b, j), 0)),
            pl.BlockSpec((rb, d), lambda b, j: (row(b, j), off_ga // d)),
            pl.BlockSpec(memory_space=pl.ANY),
        ],
        out_specs=[pl.BlockSpec((rb, hv), lambda b, j: (row(b, j), 0)),
                   pl.BlockSpec((rb, d), lambda b, j: (row(b, j), 0))],
        out_shape=[jax.ShapeDtypeStruct((mp, hv), BF16), jax.ShapeDtypeStruct((mp, d), F32)],
        scratch_shapes=[pltpu.VMEM((heads, HG_DV, HG_DK), F32), pltpu.VMEM((rb, hv), F32),
                        slab, slab, slab, slab, pltpu.VMEM((kr, d), BF16),
                        pltpu.VMEM((2, STAGE_ROWS, d), F32), pltpu.SemaphoreType.DMA((2,))],
        compiler_params=_params(2, vmem // (1024 * 1024) + 4),
        name="hgrn2_ret_out_proj",
    )(lb_logits, gn, proj, proj, proj, proj, yr, proj, w_ret)


def _router_body(x_ref, w_ref, idx_ref, wts_ref, *, n_experts):
    x = x_ref[...]
    w = w_ref[...]
    xh = x.astype(BF16)
    xl = (x - xh.astype(F32)).astype(BF16)
    wh = w.astype(BF16)
    wl = (w - wh.astype(F32)).astype(BF16)
    logits = (jnp.dot(xh, wh, preferred_element_type=F32)
              + jnp.dot(xh, wl, preferred_element_type=F32)
              + jnp.dot(xl, wh, preferred_element_type=F32))
    lane = lax.broadcasted_iota(jnp.int32, logits.shape, 1)
    neg = jnp.float32(-jnp.inf)
    logits = jnp.where(lane < n_experts, logits, neg)
    v1 = jnp.max(logits, axis=-1, keepdims=True)
    i1 = jnp.min(jnp.where(logits == v1, lane, ROUTER_LANES), axis=-1, keepdims=True)
    rest = jnp.where(lane == i1, neg, logits)
    v2 = jnp.max(rest, axis=-1, keepdims=True)
    i2 = jnp.min(jnp.where(rest == v2, lane, ROUTER_LANES), axis=-1, keepdims=True)
    e2 = jnp.exp(v2 - v1)
    w1 = 1.0 / (1.0 + e2)
    w2 = e2 / (1.0 + e2)
    idx_ref[...] = jnp.where(lane == 0, i1, jnp.where(lane == 1, i2, 0))
    wts_ref[...] = jnp.where(lane == 0, w1, jnp.where(lane == 1, w2, 0.0))


def _router(h, w_router_padded, layer_idx, n_experts, tp):
    mp, d = h.shape
    rows = _pick(tp, (528, 384, 256, 128, 64, 32, 16))
    body = functools.partial(_router_body, n_experts=n_experts)
    out_spec = pl.BlockSpec((rows, ROUTER_LANES), lambda i: (i, 0))
    return pl.pallas_call(
        body,
        grid=(mp // rows,),
        in_specs=[pl.BlockSpec((rows, d), lambda i: (i, 0)),
                  pl.BlockSpec((None, d, ROUTER_LANES), lambda i: (layer_idx, 0, 0))],
        out_specs=[out_spec, out_spec],
        out_shape=[jax.ShapeDtypeStruct((mp, ROUTER_LANES), jnp.int32),
                   jax.ShapeDtypeStruct((mp, ROUTER_LANES), F32)],
        compiler_params=_params(1, 40),
        name="moe_router",
    )(h, w_router_padded)


MOE_TM = 512
GATHER_ROWS = 256


def _gather_body(src_ref, tv_ref, x_hbm, o_ref, buf, sem, *, rows, steps_per_tile):
    step = pl.program_id(0)

    def start_rows(st, slot):
        @pl.when(tv_ref[st // steps_per_tile] == 1)
        def _():
            def issue(r, c):
                t = src_ref[st * rows + r]
                pltpu.make_async_copy(x_hbm.at[pl.ds(t, 1), :], buf.at[slot, pl.ds(r, 1), :],
                                      sem.at[slot]).start()
                return c
            lax.fori_loop(0, rows, issue, 0, unroll=8)

    @pl.when(step == 0)
    def _():
        start_rows(step, 0)

    @pl.when(step + 1 < pl.num_programs(0))
    def _():
        start_rows(step + 1, lax.rem(step + 1, 2))

    slot = lax.rem(step, 2)
    valid = tv_ref[step // steps_per_tile] == 1

    @pl.when(valid)
    def _():
        pltpu.make_async_copy(x_hbm.at[pl.ds(0, rows), :], buf.at[slot], sem.at[slot]).wait()
        o_ref[...] = buf[slot].astype(o_ref.dtype)

    @pl.when(jnp.logical_not(valid))
    def _():
        o_ref[...] = jnp.zeros_like(o_ref)


def _gather_rows(h, src, tile_valid, *, n_rows):
    mp, d = h.shape
    rows = GATHER_ROWS
    assert n_rows % rows == 0 and MOE_TM % rows == 0
    body = functools.partial(_gather_body, rows=rows, steps_per_tile=MOE_TM // rows)
    return pl.pallas_call(
        body,
        grid_spec=pltpu.PrefetchScalarGridSpec(
            num_scalar_prefetch=2,
            grid=(n_rows // rows,),
            in_specs=[pl.BlockSpec(memory_space=pl.ANY)],
            out_specs=pl.BlockSpec((rows, d), lambda i, src, tv: (i, 0)),
            scratch_shapes=[pltpu.VMEM((2, rows, d), F32), pltpu.SemaphoreType.DMA((2,))],
        ),
        out_shape=jax.ShapeDtypeStruct((n_rows, d), BF16),
        compiler_params=_params(1, 32),
        name="moe_gather",
    )(src, tile_valid, h)


def _gmm_body(te_ref, tv_ref, nx_ref, *refs, n_w, swiglu, wbase, bn):
    it = iter(refs)
    x_ref = next(it)
    w_hbm = [next(it) for _ in range(n_w)]
    o_ref = next(it)
    wb_refs = [next(it) for _ in range(n_w)]
    wf_refs = [next(it) for _ in range(n_w)]
    sem = next(it)
    j = pl.program_id(0)
    i = pl.program_id(1)
    fresh = jnp.logical_or(i == 0, te_ref[i] != te_ref[jnp.maximum(i - 1, 0)])

    def weight_copy(w, expert, col_tile):
        cols = pl.ds(pl.multiple_of(col_tile * bn, bn), bn)
        return pltpu.make_async_copy(w_hbm[w].at[wbase + expert, :, cols], wf_refs[w], sem.at[w])

    @pl.when(fresh)
    def _():
        @pl.when(jnp.logical_and(i == 0, j == 0))
        def _():
            for w in range(n_w):
                weight_copy(w, te_ref[0], 0).start()
        for w in range(n_w):
            weight_copy(w, te_ref[i], j).wait()
            wb_refs[w][...] = wf_refs[w][...].astype(BF16)
        nxt = nx_ref[i]

        @pl.when(nxt >= 0)
        def _():
            for w in range(n_w):
                weight_copy(w, nxt, j).start()

        @pl.when(jnp.logical_and(nxt < 0, j + 1 < pl.num_programs(0)))
        def _():
            for w in range(n_w):
                weight_copy(w, te_ref[0], j + 1).start()

    @pl.when(tv_ref[i] == 1)
    def _():
        x = x_ref[...]
        y = jnp.dot(x, wb_refs[0][...], preferred_element_type=F32)
        if swiglu:
            u = jnp.dot(x, wb_refs[1][...], preferred_element_type=F32)
            y = (y * jax.nn.sigmoid(y)) * u
        o_ref[...] = y.astype(o_ref.dtype)

    @pl.when(tv_ref[i] == 0)
    def _():
        o_ref[...] = jnp.zeros_like(o_ref)


def _grouped_matmul(x, ws, tile_expert, tile_valid, next_expert, wbase, *, bn, out_dtype, swiglu,
                    name):
    m, k = x.shape
    n = ws[0].shape[2]
    tm = MOE_TM
    assert m % tm == 0 and n % bn == 0
    in_specs = [pl.BlockSpec((tm, k), lambda j, i, te, tv, nx: (i, 0))]
    in_specs += [pl.BlockSpec(memory_space=pl.ANY) for _ in ws]
    out_bytes = jnp.dtype(out_dtype).itemsize
    vmem = (2 * tm * k * 2 + len(ws) * k * bn * (4 + 2) + 2 * tm * bn * out_bytes
            + 4 * tm * bn * 4)
    body = functools.partial(_gmm_body, n_w=len(ws), swiglu=swiglu, wbase=wbase, bn=bn)
    return pl.pallas_call(
        body,
        grid_spec=pltpu.PrefetchScalarGridSpec(
            num_scalar_prefetch=3,
            grid=(n // bn, m // tm),
            in_specs=in_specs,
            out_specs=pl.BlockSpec((tm, bn), lambda j, i, te, tv, nx: (i, j)),
            scratch_shapes=([pltpu.VMEM((k, bn), BF16) for _ in ws]
                            + [pltpu.VMEM((k, bn), F32) for _ in ws]
                            + [pltpu.SemaphoreType.DMA((len(ws),))]),
        ),
        out_shape=jax.ShapeDtypeStruct((m, n), out_dtype),
        compiler_params=_params(2, vmem // (1024 * 1024) + 4),
        name=name,
    )(tile_expert, tile_valid, next_expert, x, *ws)


def _combine_ln_body(pos_ref, h_ref, w_ref, g_ref, b_ref, y_hbm, o_ref, ob_ref, buf, sem,
                     *, alpha, rows, tp, pad):
    step = pl.program_id(0)

    def start_rows(st, slot):
        def issue(r, c):
            for s in range(TOP_K):
                p = pos_ref[(st * rows + r) * TOP_K + s]
                pltpu.make_async_copy(y_hbm.at[pl.ds(p, 1), :], buf.at[slot, s, pl.ds(r, 1), :],
                                      sem.at[slot]).start()
            return c
        lax.fori_loop(0, rows, issue, 0, unroll=4)

    @pl.when(step == 0)
    def _():
        start_rows(step, 0)

    @pl.when(step + 1 < pl.num_programs(0))
    def _():
        start_rows(step + 1, lax.rem(step + 1, 2))

    slot = lax.rem(step, 2)
    for s in range(TOP_K):
        pltpu.make_async_copy(y_hbm.at[pl.ds(0, rows), :], buf.at[slot, s], sem.at[slot]).wait()

    w = w_ref[...]
    lane = lax.broadcasted_iota(jnp.int32, w.shape, 1)
    f = None
    for s in range(TOP_K):
        ws = jnp.sum(jnp.where(lane == s, w, 0.0), axis=-1, keepdims=True)
        term = ws * buf[slot, s]
        f = term if f is None else f + term
    y = _layernorm_rows(alpha * h_ref[...] + f, g_ref[...], b_ref[...], rows=rows, tp=tp, pad=pad)
    o_ref[...] = y
    ob_ref[...] = y.astype(BF16)


def _combine_ln(h, ys, pos, wts, g, b, layer, *, alpha, tp, pad):
    mp, d = h.shape
    rows = _pick(tp, (384, 256, 128, 64, 32, 16))
    body = functools.partial(_combine_ln_body, alpha=alpha, rows=rows, tp=tp, pad=pad)
    row_spec = pl.BlockSpec((rows, d), lambda i, p: (i, 0))
    par_spec = pl.BlockSpec((None, 1, d), lambda i, p: (layer, 0, 0))
    return pl.pallas_call(
        body,
        grid_spec=pltpu.PrefetchScalarGridSpec(
            num_scalar_prefetch=1,
            grid=(mp // rows,),
            in_specs=[row_spec, pl.BlockSpec((rows, ROUTER_LANES), lambda i, p: (i, 0)),
                      par_spec, par_spec, pl.BlockSpec(memory_space=pl.ANY)],
            out_specs=[row_spec, row_spec],
            scratch_shapes=[pltpu.VMEM((2, TOP_K, rows, d), F32), pltpu.SemaphoreType.DMA((2,))],
        ),
        out_shape=[jax.ShapeDtypeStruct((mp, d), F32), jax.ShapeDtypeStruct((mp, d), BF16)],
        compiler_params=_params(1, 40),
        name="moe_combine_layernorm",
    )(pos, h, wts, g, b, ys)


def _route(idx, *, n_experts, tp, pad):
    mp = idx.shape[0]
    tm = MOE_TM
    n_assign = mp * TOP_K
    n_tiles = -(-n_assign // tm) + n_experts
    n_rows = n_tiles * tm
    token = jnp.arange(mp, dtype=jnp.int32)
    real = (token % tp) >= pad
    e = jnp.where(real[:, None], idx[:, :TOP_K], n_experts).reshape(n_assign)
    onehot = (e[:, None] == jnp.arange(n_experts, dtype=jnp.int32)[None, :]).astype(jnp.int32)
    csum = jnp.cumsum(onehot, axis=0)
    counts = csum[-1]
    rank = jnp.sum(onehot * csum, axis=1) - 1
    gsz = ((counts + tm - 1) // tm) * tm
    gend = jnp.cumsum(gsz)
    gstart = gend - gsz
    routed = e < n_experts
    dst = jnp.where(routed, gstart[jnp.minimum(e, n_experts - 1)] + rank, n_rows)
    assign_token = jnp.arange(n_assign, dtype=jnp.int32) // TOP_K
    src = jnp.zeros((n_rows,), jnp.int32).at[dst].set(assign_token, mode="drop")
    pos = jnp.where(routed, dst, 0).astype(jnp.int32)
    tile_start = jnp.arange(n_tiles, dtype=jnp.int32) * tm
    tile_valid = tile_start < gend[-1]
    tile_expert = jnp.sum((tile_start[:, None] >= gend[None, :]).astype(jnp.int32), axis=1)
    last_expert = jnp.max(jnp.where(tile_valid, tile_expert, 0))
    tile_expert = jnp.where(tile_valid, tile_expert, last_expert)
    later = jnp.arange(n_tiles)[None, :] > jnp.arange(n_tiles)[:, None]
    differs = jnp.logical_and(later, tile_expert[None, :] != tile_expert[:, None])
    first = jnp.argmax(differs, axis=1)
    next_expert = jnp.where(jnp.any(differs, axis=1), tile_expert[first], -1).astype(jnp.int32)
    return src, pos, tile_expert, tile_valid.astype(jnp.int32), next_expert, n_rows


def kernel(x, meta_tokens, w_in, ret_gn_g, hg_norm_g, hg_lb_logits, w_ret_out, w_hg_out, w_o,
           ln1_g, ln1_b, ln2_g, ln2_b, ffn_w_gate, ffn_w_up, ffn_w_down,
           moe_router, moe_w_gate, moe_w_up, moe_w_down):
    batch, seq, d = x.shape
    n_meta = meta_tokens.shape[0]
    depth, _, p_in = w_in.shape
    ret_v = w_ret_out.shape[1]
    hg_v = w_hg_out.shape[1]
    hg_k = hg_lb_logits.shape[1]
    ret_qk = (p_in - 2 * ret_v - 2 * hg_k - 2 * hg_v - 2 * d) // 2
    ret_heads = ret_qk // RET_DK
    hg_heads = hg_k // HG_DK
    n_experts = moe_router.shape[2]
    assert ret_v == ret_heads * RET_DV and hg_v == hg_heads * HG_DV and hg_k == hg_v
    assert n_experts <= ROUTER_LANES

    pad = RET_CHUNK - n_meta
    tp = pad + n_meta + seq
    assert tp % RET_CHUNK == 0 and tp % HG_CHUNK == 0
    mp = batch * tp
    alpha = (2 * depth) ** 0.25

    off = {}
    acc = 0
    for nm, wd in (("rq", ret_qk), ("rk", ret_qk), ("rv", ret_v), ("rg", ret_v), ("hq", hg_k),
                   ("hf", hg_k), ("hi", hg_v), ("hg", hg_v), ("ga", d), ("gb", d)):
        off[nm] = acc
        acc += wd

    meta = jnp.broadcast_to(meta_tokens[None].astype(F32), (batch, n_meta, d))
    h = jnp.concatenate([jnp.zeros((batch, pad, d), F32), meta, x.astype(F32)], axis=1)
    h = h.reshape(mp, d)
    hb = h.astype(BF16)

    pos = jnp.arange(tp, dtype=F32) - float(pad)
    inv = 1.0 / (ROPE_BASE ** jnp.linspace(0.0, 1.0, RET_DK // 2, dtype=F32))
    ang = pos[:, None] * inv[None, :]
    cos, sin = jnp.cos(ang), jnp.sin(ang)
    log_gamma = jnp.log1p(-jnp.exp2(-5.0 - jnp.arange(ret_heads, dtype=F32)))

    ret_gn3 = ret_gn_g.astype(F32).reshape(depth, 1, ret_v)
    hg_gn3 = hg_norm_g.astype(F32).reshape(depth, 1, hg_v)
    ln1_g3, ln1_b3 = ln1_g.reshape(depth, 1, d), ln1_b.reshape(depth, 1, d)
    ln2_g3, ln2_b3 = ln2_g.reshape(depth, 1, d), ln2_b.reshape(depth, 1, d)
    n_moe = moe_router.shape[0]
    d_ffe = moe_w_gate.shape[3]
    router_w = jnp.pad(moe_router.astype(F32), ((0, 0), (0, 0), (0, ROUTER_LANES - n_experts)))
    moe_wg = moe_w_gate.reshape(n_moe * n_experts, d, d_ffe)
    moe_wu = moe_w_up.reshape(n_moe * n_experts, d, d_ffe)
    moe_wd = moe_w_down.reshape(n_moe * n_experts, d_ffe, d)

    bm_half = _pick(mp, (528, 384, 256, 128, 64, 32, 16))
    tile = lambda n, cands: _pick(n, cands)

    for l in range(depth):
        proj = _matmul_rows_resident(hb, [w_in], l, rows=tp, bn=tile(p_in, (512, 256, 128)),
                                     out_dtype=BF16, name="in_proj")
        yr = _retention(proj, cos, sin, log_gamma, ret_gn3, l, batch=batch, tp=tp,
                        heads=ret_heads, off_q=off["rq"], off_k=off["rk"], off_v=off["rv"],
                        off_g=off["rg"])
        yh, t_r = _hgrn2_with_ret_out(
            proj, hg_lb_logits.astype(F32), hg_gn3, yr, w_ret_out, l, batch=batch, tp=tp,
            heads=hg_heads, off_q=off["hq"], off_f=off["hf"], off_i=off["hi"], off_g=off["hg"],
            off_ga=off["ga"])
        merged = _matmul_gate_add(yh, w_hg_out, l, (proj, off["gb"]), t_r, tp=tp,
                                  name="hg_out_proj")
        h, hb = _matmul_add_ln(merged, w_o, l, h, ln1_g3, ln1_b3, alpha=alpha, tp=tp, pad=pad)
        if l % 2 == 0:
            d_ff = ffn_w_gate.shape[2]
            act = _matmul_rows_resident(hb, [ffn_w_gate, ffn_w_up], l // 2, rows=tp,
                                        bn=tile(d_ff, (256, 128)), out_dtype=BF16, name="ffn_up")
            f = _matmul(act, ffn_w_down, l // 2, bm=bm_half, bn=tile(d, (512, 256, 128)),
                        out_dtype=F32, name="ffn_down")
            h, hb = _add_ln(h, f, ln2_g3, ln2_b3, l, alpha=alpha, tp=tp, pad=pad)
        else:
            idx, wts = _router(h, router_w, l // 2, n_experts, tp)
            src, pos, tile_expert, tile_valid, next_expert, n_rows = _route(
                idx, n_experts=n_experts, tp=tp, pad=pad)
            xs = _gather_rows(h, src, tile_valid, n_rows=n_rows)
            wbase = (l // 2) * n_experts
            act = _grouped_matmul(xs, [moe_wg, moe_wu], tile_expert, tile_valid, next_expert, wbase,
                                  bn=tile(d_ffe, (1408, 256, 128)), out_dtype=BF16, swiglu=True,
                                  name="moe_up")
            ys = _grouped_matmul(act, [moe_wd], tile_expert, tile_valid, next_expert, wbase,
                                 bn=tile(d, (1024, 512, 256, 128)), out_dtype=F32, swiglu=False,
                                 name="moe_down")
            h, hb = _combine_ln(h, ys, pos, wts, ln2_g3, ln2_b3, l, alpha=alpha, tp=tp, pad=pad)

    out = h.reshape(batch, tp, d)[:, pad + n_meta:, :]
    return out.astype(x.dtype)
```

```python
import functools

import jax
import jax.numpy as jnp
from jax import lax
from jax.experimental import pallas as pl
from jax.experimental.pallas import tpu as pltpu

F32 = jnp.float32
BF16 = jnp.bfloat16

RET_DK = 256
RET_DV = 512
RET_CHUNK = 128
HG_DK = 128
HG_DV = 128
HG_CHUNK = 64
TOP_K = 2
ROPE_BASE = 10000.0
LN_EPS = 1e-5
ROUTER_LANES = 128
VMEM_LIMIT_CAP_MB = 56


def _pick(n, candidates):
    for c in candidates:
        if n % c == 0:
            return c
    raise ValueError(f"no tile in {candidates} divides {n}")


def _params(n_axes, vmem_mb):
    return pltpu.CompilerParams(
        dimension_semantics=("arbitrary",) * n_axes,
        vmem_limit_bytes=min(vmem_mb, VMEM_LIMIT_CAP_MB) * 1024 * 1024)


def _mm_body(x_ref, w_ref, o_ref, wb_ref):
    @pl.when(pl.program_id(1) == 0)
    def _():
        wb_ref[...] = w_ref[...].astype(BF16)

    o_ref[...] = jnp.dot(x_ref[...], wb_ref[...], preferred_element_type=F32).astype(o_ref.dtype)


def _matmul(x, w, widx, *, bm, bn, out_dtype, name):
    m, k = x.shape
    n = w.shape[2]
    assert m % bm == 0 and n % bn == 0, (m, bm, n, bn)
    out_bytes = jnp.dtype(out_dtype).itemsize
    vmem = 2 * bm * k * 2 + k * bn * (2 * 4 + 2) + 2 * bm * bn * out_bytes + 6 * bm * bn * 4
    return pl.pallas_call(
        _mm_body,
        grid=(n // bn, m // bm),
        in_specs=[pl.BlockSpec((bm, k), lambda j, i: (i, 0)),
                  pl.BlockSpec((None, k, bn), lambda j, i: (widx, 0, j))],
        out_specs=pl.BlockSpec((bm, bn), lambda j, i: (i, j)),
        out_shape=jax.ShapeDtypeStruct((m, n), out_dtype),
        scratch_shapes=[pltpu.VMEM((k, bn), BF16)],
        compiler_params=_params(2, vmem // (1024 * 1024) + 4),
        name=name,
    )(x, w)


def _mm_xres_body(*refs, n_w, sub, skip):
    x_ref, w_refs, o_ref = refs[0], refs[1:1 + n_w], refs[1 + n_w]
    wbs = [w_ref[...].astype(BF16) for w_ref in w_refs]
    rows = x_ref.shape[0]
    o_ref[0:skip, :] = jnp.zeros((skip, o_ref.shape[1]), o_ref.dtype)
    for r0 in range(skip, rows, sub):
        rs = slice(r0, min(r0 + sub, rows))
        y = jnp.dot(x_ref[rs, :], wbs[0], preferred_element_type=F32)
        if n_w == 2:
            u = jnp.dot(x_ref[rs, :], wbs[1], preferred_element_type=F32)
            y = (y * jax.nn.sigmoid(y)) * u
        o_ref[rs, :] = y.astype(o_ref.dtype)


def _matmul_rows_resident(x, ws, widx, *, rows, skip, bn, out_dtype, name):
    m, k = x.shape
    n = ws[0].shape[2]
    assert m % rows == 0 and n % bn == 0 and skip % 16 == 0
    sub = _pick(rows, (1056, 768, 512, 384, 256, 128, 64, 32, 16))
    out_bytes = jnp.dtype(out_dtype).itemsize
    vmem = (rows * k * 2 + len(ws) * k * bn * (2 * 4 + 2) + 2 * rows * bn * out_bytes
            + (2 + len(ws)) * sub * bn * 4)
    return pl.pallas_call(
        functools.partial(_mm_xres_body, n_w=len(ws), sub=sub, skip=skip),
        grid=(m // rows, n // bn),
        in_specs=([pl.BlockSpec((rows, k), lambda b, j: (b, 0), pipeline_mode=pl.Buffered(1))]
                  + [pl.BlockSpec((None, k, bn), lambda b, j: (widx, 0, j)) for _ in ws]),
        out_specs=pl.BlockSpec((rows, bn), lambda b, j: (b, j)),
        out_shape=jax.ShapeDtypeStruct((m, n), out_dtype),
        compiler_params=_params(2, vmem // (1024 * 1024) + 6),
        name=name,
    )(x, *ws)


def _ln_body(h_ref, m_ref, g_ref, b_ref, o_ref, ob_ref, *, alpha, rows, tp, pad):
    y = _layernorm_rows(alpha * h_ref[...] + m_ref[...], g_ref[...], b_ref[...],
                        rows=rows, tp=tp, pad=pad)
    o_ref[...] = y
    ob_ref[...] = y.astype(BF16)


def _layernorm_rows(z, g, b, *, rows, tp, pad, row0=0):
    mu = jnp.mean(z, axis=-1, keepdims=True)
    zc = z - mu
    var = jnp.mean(zc * zc, axis=-1, keepdims=True)
    y = zc * lax.rsqrt(var + LN_EPS) * g + b
    blk = lax.rem(pl.program_id(0), tp // rows)
    r = blk * rows + row0 + lax.broadcasted_iota(jnp.int32, (z.shape[0], 1), 0)
    return jnp.where(r >= pad, y, 0.0)


RESIDENT_ROWS = 384
RESIDENT_SUB = 128


def _mm_ln_body(x_ref, w_ref, h_ref, g_ref, b_ref, o_ref, ob_ref, wb_ref, *, alpha, rows, sub, tp,
                pad):
    @pl.when(pl.program_id(0) == 0)
    def _():
        wb_ref[...] = w_ref[...].astype(BF16)

    for r0 in range(0, rows, sub):
        rs = slice(r0, r0 + sub)
        m = jnp.dot(x_ref[rs, :], wb_ref[...], preferred_element_type=F32)
        y = _layernorm_rows(alpha * h_ref[rs, :] + m, g_ref[...], b_ref[...], rows=rows, tp=tp,
                            pad=pad, row0=r0)
        o_ref[rs, :] = y
        ob_ref[rs, :] = y.astype(BF16)


def _mm_gate_add_body(x_ref, w_ref, gate_ref, add_ref, o_ref, wb_ref, *, rows, sub):
    @pl.when(pl.program_id(0) == 0)
    def _():
        wb_ref[...] = w_ref[...].astype(BF16)

    for r0 in range(0, rows, sub):
        rs = slice(r0, r0 + sub)
        m = jnp.dot(x_ref[rs, :], wb_ref[...], preferred_element_type=F32)
        y = add_ref[rs, :] + jax.nn.sigmoid(gate_ref[rs, :].astype(F32)) * m
        o_ref[rs, :] = y.astype(o_ref.dtype)


def _resident_specs(x, w, layer, tp):
    mp, k = x.shape
    d = w.shape[2]
    rows = _pick(tp, (RESIDENT_ROWS, 256, 128, 64, 32, 16))
    sub = _pick(rows, (RESIDENT_SUB, 64, 32, 16))
    x_spec = pl.BlockSpec((rows, k), lambda i: (i, 0))
    w_spec = pl.BlockSpec((None, k, d), lambda i: (layer, 0, 0), pipeline_mode=pl.Buffered(1))
    return mp, k, d, rows, sub, x_spec, w_spec


def _matmul_add_ln(x, w, layer, h, g, b, *, alpha, tp, pad):
    mp, k, d, rows, sub, x_spec, w_spec = _resident_specs(x, w, layer, tp)
    body = functools.partial(_mm_ln_body, alpha=alpha, rows=rows, sub=sub, tp=tp, pad=pad)
    row_spec = pl.BlockSpec((rows, d), lambda i: (i, 0))
    par_spec = pl.BlockSpec((None, 1, d), lambda i: (layer, 0, 0))
    vmem = k * d * (4 + 2) + 2 * rows * k * 2 + rows * d * (2 * 4 + 2 * 4 + 2 * 2 + 3 * 4)
    return pl.pallas_call(
        body,
        grid=(mp // rows,),
        in_specs=[x_spec, w_spec, row_spec, par_spec, par_spec],
        out_specs=[row_spec, row_spec],
        out_shape=[jax.ShapeDtypeStruct((mp, d), F32), jax.ShapeDtypeStruct((mp, d), BF16)],
        scratch_shapes=[pltpu.VMEM((k, d), BF16)],
        compiler_params=_params(1, vmem // (1024 * 1024) + 4),
        name="mix_out_proj_layernorm",
    )(x, w, h, g, b)


def _matmul_gate_add(x, w, layer, gate, add, *, tp, name):
    mp, k, d, rows, sub, x_spec, w_spec = _resident_specs(x, w, layer, tp)
    garr, goff = gate
    assert goff % d == 0
    gblk = goff // d
    body = functools.partial(_mm_gate_add_body, rows=rows, sub=sub)
    row_spec = pl.BlockSpec((rows, d), lambda i: (i, 0))
    vmem = k * d * (4 + 2) + 2 * rows * k * 2 + rows * d * (2 * 2 + 2 * 4 + 2 * 2 + 3 * 4)
    return pl.pallas_call(
        body,
        grid=(mp // rows,),
        in_specs=[x_spec, w_spec, pl.BlockSpec((rows, d), lambda i: (i, gblk)), row_spec],
        out_specs=row_spec,
        out_shape=jax.ShapeDtypeStruct((mp, d), BF16),
        scratch_shapes=[pltpu.VMEM((k, d), BF16)],
        compiler_params=_params(1, vmem // (1024 * 1024) + 4),
        name=name,
    )(x, w, garr, add)


def _add_ln(h, m, g, b, layer, *, alpha, tp, pad):
    mp, d = h.shape
    rows = _pick(tp, (528, 384, 256, 128, 64, 32, 16))
    body = functools.partial(_ln_body, alpha=alpha, rows=rows, tp=tp, pad=pad)
    row_spec = pl.BlockSpec((rows, d), lambda i: (i, 0))
    par_spec = pl.BlockSpec((None, 1, d), lambda i: (layer, 0, 0))
    return pl.pallas_call(
        body,
        grid=(mp // rows,),
        in_specs=[row_spec, row_spec, par_spec, par_spec],
        out_specs=[row_spec, row_spec],
        out_shape=[jax.ShapeDtypeStruct((mp, d), F32), jax.ShapeDtypeStruct((mp, d), BF16)],
        compiler_params=_params(1, 40),
        name="add_layernorm",
    )(h, m, g, b)


def _ret_body(lg_ref, q_ref, k_ref, v_ref, g_ref, cos_ref, sin_ref, gn_ref, o_ref, r_ref,
              *, n_chunks, hpg):
    c_len = RET_CHUNK
    half = RET_DK // 2
    hs = range(hpg)
    ksl = [slice(i * RET_DK, (i + 1) * RET_DK) for i in hs]
    vsl = [slice(i * RET_DV, (i + 1) * RET_DV) for i in hs]

    @pl.when(pl.program_id(2) == 0)
    def _():
        r_ref[...] = jnp.zeros_like(r_ref)

    t_col = lax.broadcasted_iota(jnp.int32, (c_len, 1), 0).astype(F32)
    rel = (lax.broadcasted_iota(jnp.int32, (c_len, c_len), 0)
           - lax.broadcasted_iota(jnp.int32, (c_len, c_len), 1))
    rel_f = jnp.maximum(rel, 0).astype(F32)
    lg = [lg_ref[pl.program_id(1) * hpg + i] for i in hs]
    intra = [jnp.where(rel >= 0, jnp.exp(rel_f * x), 0.0) * (RET_DK ** -0.5) for x in lg]
    q_decay = [jnp.exp((t_col + 1.0) * x) * (RET_DK ** -0.5) for x in lg]
    k_decay = [jnp.exp((c_len - 1.0 - t_col) * x) for x in lg]
    c_decay = [jnp.exp(jnp.full((1, 1), float(c_len), F32) * x) for x in lg]
    gn = gn_ref[...]

    def rot(x, cos, sin):
        x1, x2 = x[:, :half], x[:, half:]
        return jnp.concatenate([x1 * cos - x2 * sin, x1 * sin + x2 * cos], axis=-1)

    def chunk(c, carry):
        rows = pl.ds(pl.multiple_of(c * c_len, c_len), c_len)
        cos = cos_ref[rows, :]
        sin = sin_ref[rows, :]
        q = [rot(q_ref[rows, ksl[i]].astype(F32), cos, sin) for i in hs]
        k = [rot(k_ref[rows, ksl[i]].astype(F32), cos, sin) for i in hs]
        v = [v_ref[rows, vsl[i]] for i in hs]
        s = [lax.dot_general(q[i].astype(BF16), k[i].astype(BF16),
                             (((1,), (1,)), ((), ())), preferred_element_type=F32) * intra[i]
             for i in hs]
        r_old = [r_ref[i] for i in hs]
        o = [jnp.dot(s[i].astype(BF16), v[i], preferred_element_type=F32)
             + jnp.dot((q[i] * q_decay[i]).astype(BF16), r_old[i].astype(BF16),
                       preferred_element_type=F32) for i in hs]
        for i in hs:
            r_ref[i] = r_old[i] * c_decay[i] + lax.dot_general(
                (k[i] * k_decay[i]).astype(BF16), v[i], (((0,), (0,)), ((), ())),
                preferred_element_type=F32)
        for i in hs:
            mu = jnp.mean(o[i], axis=-1, keepdims=True)
            oc = o[i] - mu
            var = jnp.mean(oc * oc, axis=-1, keepdims=True)
            gate = g_ref[rows, vsl[i]].astype(F32)
            y = oc * lax.rsqrt(var + LN_EPS) * gn[:, vsl[i]] * (gate * jax.nn.sigmoid(gate))
            o_ref[rows, vsl[i]] = y.astype(o_ref.dtype)
        return carry

    lax.fori_loop(0, n_chunks, chunk, 0)


def _retention(proj, cos, sin, log_gamma, gn, layer, *, batch, tp, heads, off_q, off_k, off_v,
               off_g):
    mp = proj.shape[0]
    n_chunks_total = tp // RET_CHUNK
    hpg = _pick(heads, (4, 2, 1))
    cpb = _pick(n_chunks_total, (3, 2, 1))
    tb = cpb * RET_CHUNK
    nblk = tp // tb
    gk, gv = hpg * RET_DK, hpg * RET_DV
    assert off_q % gk == 0 and off_k % gk == 0 and off_v % gv == 0 and off_g % gv == 0
    qb, kb, vb, gb = off_q // gk, off_k // gk, off_v // gv, off_g // gv
    row = lambda b, h, j: b * nblk + j
    body = functools.partial(_ret_body, n_chunks=cpb, hpg=hpg)
    return pl.pallas_call(
        body,
        grid=(batch, heads // hpg, nblk),
        in_specs=[
            pl.BlockSpec(memory_space=pltpu.SMEM),
            pl.BlockSpec((tb, gk), lambda b, h, j: (row(b, h, j), qb + h)),
            pl.BlockSpec((tb, gk), lambda b, h, j: (row(b, h, j), kb + h)),
            pl.BlockSpec((tb, gv), lambda b, h, j: (row(b, h, j), vb + h)),
            pl.BlockSpec((tb, gv), lambda b, h, j: (row(b, h, j), gb + h)),
            pl.BlockSpec((tb, RET_DK // 2), lambda b, h, j: (j, 0)),
            pl.BlockSpec((tb, RET_DK // 2), lambda b, h, j: (j, 0)),
            pl.BlockSpec((None, 1, gv), lambda b, h, j: (layer, 0, h)),
        ],
        out_specs=pl.BlockSpec((tb, gv), lambda b, h, j: (row(b, h, j), h)),
        out_shape=jax.ShapeDtypeStruct((mp, heads * RET_DV), BF16),
        scratch_shapes=[pltpu.VMEM((hpg, RET_DK, RET_DV), F32)],
        compiler_params=_params(3, 40),
        name="retention",
    )(log_gamma, proj, proj, proj, proj, cos, sin, gn)


HG_SAFE_RANGE = 85.0


FUSED_ROWS = 192
STAGE_ROWS = 512


def _hgrn_retout_body(lbl_ref, gn_ref, q_ref, f_ref, i_ref, g_ref, yr_ref, ga_ref, w_hbm,
                      o_ref, t_ref, s_ref, acc_ref, bx_ref, qx_ref, kx_ref, vx_ref, wb_ref,
                      wf_ref, sem, *, layer, n_chunks, heads):
    c_len = HG_CHUNK
    first = jnp.logical_and(pl.program_id(0) == 0, pl.program_id(1) == 0)

    @pl.when(first)
    def _():
        n_stage = wb_ref.shape[0] // STAGE_ROWS

        def stage_copy(c, slot):
            return pltpu.make_async_copy(
                w_hbm.at[layer, pl.ds(c * STAGE_ROWS, STAGE_ROWS), :], wf_ref.at[slot],
                sem.at[slot])
        stage_copy(0, 0).start()
        for c in range(n_stage):
            if c + 1 < n_stage:
                stage_copy(c + 1, (c + 1) % 2).start()
            stage_copy(c, c % 2).wait()
            wb_ref[c * STAGE_ROWS:(c + 1) * STAGE_ROWS, :] = wf_ref[c % 2].astype(BF16)

    @pl.when(pl.program_id(1) == 0)
    def _():
        s_ref[...] = jnp.zeros_like(s_ref)

    logits = lbl_ref[...]
    e = jnp.exp(logits - jnp.max(logits, axis=0, keepdims=True))
    sm = e / jnp.sum(e, axis=0, keepdims=True)
    lb = jnp.zeros_like(sm[0:1, :])
    for j in range(1, layer + 1):
        lb = lb + sm[j:j + 1, :]
    gn = gn_ref[...]
    tri = (lax.broadcasted_iota(jnp.int32, (c_len, c_len), 0)
           >= lax.broadcasted_iota(jnp.int32, (c_len, c_len), 1)).astype(BF16)
    t_idx = lax.broadcasted_iota(jnp.int32, (c_len, 1), 0)
    causal = (lax.broadcasted_iota(jnp.int32, (c_len, c_len), 0)
              >= lax.broadcasted_iota(jnp.int32, (c_len, c_len), 1))
    hs = range(heads)
    ksl = [slice(h * HG_DK, (h + 1) * HG_DK) for h in hs]
    vsl = [slice(h * HG_DV, (h + 1) * HG_DV) for h in hs]

    def head_terms(rows, h):
        hq = q_ref[rows, ksl[h]].astype(F32)
        q = hq * jax.nn.sigmoid(hq)
        f = lb[:, ksl[h]] + (1.0 - lb[:, ksl[h]]) * jax.nn.sigmoid(f_ref[rows, ksl[h]].astype(F32))
        logf = jnp.log(f)
        p0 = logf.astype(BF16)
        r1 = logf - p0.astype(F32)
        p1 = r1.astype(BF16)
        p2 = (r1 - p1.astype(F32)).astype(BF16)
        b = (jnp.dot(tri, p0, preferred_element_type=F32)
             + jnp.dot(tri, p1, preferred_element_type=F32)
             + jnp.dot(tri, p2, preferred_element_type=F32))
        return q, 1.0 - f, b

    def write_out(rows, h, o):
        gate = g_ref[rows, vsl[h]].astype(F32)
        gate = gate * jax.nn.sigmoid(gate)
        y = o * lax.rsqrt(jnp.mean(o * o, axis=-1, keepdims=True) + LN_EPS)
        o_ref[rows, vsl[h]] = (y * gn[:, vsl[h]] * gate).astype(o_ref.dtype)

    n_rows = n_chunks * c_len
    col_tile = 256
    pending = [slice(c0, c0 + col_tile) for c0 in range(0, t_ref.shape[1], col_tile)]

    def project_some(n=1):
        for _ in range(n):
            if pending:
                cs = pending.pop(0)
                m = jnp.dot(yr_ref[...], wb_ref[:, cs], preferred_element_type=F32)
                t_ref[:, cs] = jax.nn.sigmoid(ga_ref[:, cs].astype(F32)) * m

    def chunk_main(rows):
        terms = []
        for h in hs:
            terms.append(head_terms(rows, h))
            if h % 8 == 7:
                project_some()
        b_last = [t[2][c_len - 1:c_len, :] for t in terms]
        q_in = [t[0] * jnp.exp(t[2]) for t in terms]
        k_out = [t[1] * jnp.exp(bl - t[2]) for t, bl in zip(terms, b_last)]
        un_mid = [jnp.where(bl < -HG_SAFE_RANGE, 0.0, jnp.exp(-0.5 * bl)) for bl in b_last]
        q_mid = [(x * u).astype(BF16) for x, u in zip(q_in, un_mid)]
        k_mid = [(x * u).astype(BF16) for x, u in zip(k_out, un_mid)]
        sc = [lax.dot_general(qm, km, (((1,), (1,)), ((), ())), preferred_element_type=F32)
              for qm, km in zip(q_mid, k_mid)]
        a = [jnp.where(causal, x, 0.0).astype(BF16) for x in sc]
        project_some()
        vh = [i_ref[rows, vs] for vs in vsl]
        st = [s_ref[h] for h in hs]
        o = [jnp.dot(a[h], vh[h], preferred_element_type=F32)
             + lax.dot_general(q_in[h].astype(BF16), st[h].astype(BF16), (((1,), (1,)), ((), ())),
                               preferred_element_type=F32) for h in hs]
        for h in hs:
            s_ref[h] = st[h] * jnp.exp(b_last[h]) + lax.dot_general(
                vh[h], k_out[h].astype(BF16), (((0,), (0,)), ((), ())),
                preferred_element_type=F32)
        b_min = b_last[0]
        for h in hs:
            acc_ref[rows, vsl[h]] = o[h]
            write_out(rows, h, o[h])
            b_min = jnp.minimum(b_min, b_last[h])
        return b_min

    step_min = None
    for c in range(n_chunks):
        b_min = chunk_main(slice(c * c_len, (c + 1) * c_len))
        step_min = b_min if step_min is None else jnp.minimum(step_min, b_min)
    project_some(len(pending))

    @pl.when(jnp.min(step_min) < -HG_SAFE_RANGE)
    def _():
        def chunk_exact(c, carry):
            rows = pl.ds(pl.multiple_of(c * c_len, c_len), c_len)
            for h in hs:
                q, k, b = head_terms(rows, h)
                bx_ref[:, ksl[h]] = b
                qx_ref[:, ksl[h]] = jnp.where(b[c_len - 1:c_len, :] < -HG_SAFE_RANGE, q, 0.0)
                kx_ref[:, ksl[h]] = k
                vx_ref[:, vsl[h]] = i_ref[rows, vsl[h]].astype(F32)

            def pair(s, cc):
                bs = bx_ref[pl.ds(s, 1), :]
                d = jnp.where(t_idx >= s, jnp.exp(jnp.minimum(bx_ref[...] - bs, 0.0)), 0.0)
                p = qx_ref[...] * d * kx_ref[pl.ds(s, 1), :]
                v_s = vx_ref[pl.ds(s, 1), :]
                for h in hs:
                    acc_ref[rows, vsl[h]] += (jnp.sum(p[:, ksl[h]], axis=-1, keepdims=True)
                                              * v_s[:, vsl[h]])
                return cc
            lax.fori_loop(0, c_len, pair, 0)
            for h in hs:
                write_out(rows, h, acc_ref[rows, vsl[h]])
            return carry
        lax.fori_loop(0, n_chunks, chunk_exact, 0)


def _hgrn2_with_ret_out(proj, lb_logits, gn, yr, w_ret, layer, *, batch, tp, heads, off_q, off_f,
                        off_i, off_g, off_ga):
    mp = proj.shape[0]
    hk, hv = heads * HG_DK, heads * HG_DV
    kr, d = w_ret.shape[1], w_ret.shape[2]
    rb = _pick(tp, (FUSED_ROWS, 128, 64))
    nblk = tp // rb
    depth = lb_logits.shape[0]
    assert kr % STAGE_ROWS == 0 and off_ga % d == 0
    body = functools.partial(_hgrn_retout_body, layer=layer, n_chunks=rb // HG_CHUNK, heads=heads)
    slab = pltpu.VMEM((HG_CHUNK, hv), F32)
    row = lambda b, j: b * nblk + j
    vmem = (kr * d * 2 + 2 * STAGE_ROWS * d * 4 + rb * hv * 4 + 4 * HG_CHUNK * hv * 4
            + heads * HG_DV * HG_DK * 4
            + 2 * rb * (2 * hk * 2 + 2 * hv * 2 + kr * 2 + d * 2 + hv * 2 + d * 4) + 3 * rb * d * 4)
    return pl.pallas_call(
        body,
        grid=(batch, nblk),
        in_specs=[
            pl.BlockSpec((depth, hk), lambda b, j: (0, 0)),
            pl.BlockSpec((None, 1, hv), lambda b, j: (layer, 0, 0)),
            pl.BlockSpec((rb, hk), lambda b, j: (row(b, j), off_q // hk)),
            pl.BlockSpec((rb, hk), lambda b, j: (row(b, j), off_f // hk)),
            pl.BlockSpec((rb, hv), lambda b, j: (row(b, j), off_i // hv)),
            pl.BlockSpec((rb, hv), lambda b, j: (row(b, j), off_g // hv)),
            pl.BlockSpec((rb, kr), lambda b, j: (row(b, j), 0)),
            pl.BlockSpec((rb, d), lambda b, j: (row(b, j), off_ga // d)),
            pl.BlockSpec(memory_space=pl.ANY),
        ],
        out_specs=[pl.BlockSpec((rb, hv), lambda b, j: (row(b, j), 0)),
                   pl.BlockSpec((rb, d), lambda b, j: (row(b, j), 0))],
        out_shape=[jax.ShapeDtypeStruct((mp, hv), BF16), jax.ShapeDtypeStruct((mp, d), F32)],
        scratch_shapes=[pltpu.VMEM((heads, HG_DV, HG_DK), F32), pltpu.VMEM((rb, hv), F32),
                        slab, slab, slab, slab, pltpu.VMEM((kr, d), BF16),
                        pltpu.VMEM((2, STAGE_ROWS, d), F32), pltpu.SemaphoreType.DMA((2,))],
        compiler_params=_params(2, vmem // (1024 * 1024) + 4),
        name="hgrn2_ret_out_proj",
    )(lb_logits, gn, proj, proj, proj, proj, yr, proj, w_ret)


def _router_body(x_ref, w_ref, idx_ref, wts_ref, *, n_experts):
    x = x_ref[...]
    w = w_ref[...]
    xh = x.astype(BF16)
    xl = (x - xh.astype(F32)).astype(BF16)
    wh = w.astype(BF16)
    wl = (w - wh.astype(F32)).astype(BF16)
    logits = (jnp.dot(xh, wh, preferred_element_type=F32)
              + jnp.dot(xh, wl, preferred_element_type=F32)
              + jnp.dot(xl, wh, preferred_element_type=F32))
    lane = lax.broadcasted_iota(jnp.int32, logits.shape, 1)
    neg = jnp.float32(-jnp.inf)
    logits = jnp.where(lane < n_experts, logits, neg)
    v1 = jnp.max(logits, axis=-1, keepdims=True)
    i1 = jnp.min(jnp.where(logits == v1, lane, ROUTER_LANES), axis=-1, keepdims=True)
    rest = jnp.where(lane == i1, neg, logits)
    v2 = jnp.max(rest, axis=-1, keepdims=True)
    i2 = jnp.min(jnp.where(rest == v2, lane, ROUTER_LANES), axis=-1, keepdims=True)
    e2 = jnp.exp(v2 - v1)
    w1 = 1.0 / (1.0 + e2)
    w2 = e2 / (1.0 + e2)
    idx_ref[...] = jnp.where(lane == 0, i1, jnp.where(lane == 1, i2, 0))
    wts_ref[...] = jnp.where(lane == 0, w1, jnp.where(lane == 1, w2, 0.0))


def _router(h, w_router_padded, layer_idx, n_experts, tp):
    mp, d = h.shape
    rows = _pick(tp, (528, 384, 256, 128, 64, 32, 16))
    body = functools.partial(_router_body, n_experts=n_experts)
    out_spec = pl.BlockSpec((rows, ROUTER_LANES), lambda i: (i, 0))
    return pl.pallas_call(
        body,
        grid=(mp // rows,),
        in_specs=[pl.BlockSpec((rows, d), lambda i: (i, 0)),
                  pl.BlockSpec((None, d, ROUTER_LANES), lambda i: (layer_idx, 0, 0))],
        out_specs=[out_spec, out_spec],
        out_shape=[jax.ShapeDtypeStruct((mp, ROUTER_LANES), jnp.int32),
                   jax.ShapeDtypeStruct((mp, ROUTER_LANES), F32)],
        compiler_params=_params(1, 40),
        name="moe_router",
    )(h, w_router_padded)


MOE_TM = 512
GATHER_ROWS = 256
DMA_UNROLL = 8


def _gather_body(src_ref, tv_ref, x_hbm, o_ref, buf, sem, *, rows, steps_per_tile):
    step = pl.program_id(0)

    def start_rows(st, slot):
        @pl.when(tv_ref[st // steps_per_tile] == 1)
        def _():
            def issue(g, c):
                for u in range(DMA_UNROLL):
                    r = g * DMA_UNROLL + u
                    t = src_ref[st * rows + r]
                    pltpu.make_async_copy(x_hbm.at[pl.ds(t, 1), :], buf.at[slot, pl.ds(r, 1), :],
                                          sem.at[slot]).start(priority=u % 2)
                return c
            lax.fori_loop(0, rows // DMA_UNROLL, issue, 0)

    @pl.when(step == 0)
    def _():
        start_rows(step, 0)

    @pl.when(step + 1 < pl.num_programs(0))
    def _():
        start_rows(step + 1, lax.rem(step + 1, 2))

    slot = lax.rem(step, 2)
    valid = tv_ref[step // steps_per_tile] == 1

    @pl.when(valid)
    def _():
        pltpu.make_async_copy(x_hbm.at[pl.ds(0, rows), :], buf.at[slot], sem.at[slot]).wait()
        o_ref[...] = buf[slot].astype(o_ref.dtype)

    @pl.when(jnp.logical_not(valid))
    def _():
        o_ref[...] = jnp.zeros_like(o_ref)


def _gather_rows(h, src, tile_valid, *, n_rows):
    mp, d = h.shape
    rows = GATHER_ROWS
    assert n_rows % rows == 0 and MOE_TM % rows == 0
    body = functools.partial(_gather_body, rows=rows, steps_per_tile=MOE_TM // rows)
    return pl.pallas_call(
        body,
        grid_spec=pltpu.PrefetchScalarGridSpec(
            num_scalar_prefetch=2,
            grid=(n_rows // rows,),
            in_specs=[pl.BlockSpec(memory_space=pl.ANY)],
            out_specs=pl.BlockSpec((rows, d), lambda i, src, tv: (i, 0)),
            scratch_shapes=[pltpu.VMEM((2, rows, d), F32), pltpu.SemaphoreType.DMA((2,))],
        ),
        out_shape=jax.ShapeDtypeStruct((n_rows, d), BF16),
        compiler_params=_params(1, 32),
        name="moe_gather",
    )(src, tile_valid, h)


def _gmm_body(te_ref, tv_ref, nx_ref, *refs, n_w, swiglu, wbase, bn):
    it = iter(refs)
    x_ref = next(it)
    w_hbm = [next(it) for _ in range(n_w)]
    o_ref = next(it)
    wb_refs = [next(it) for _ in range(n_w)]
    wf_refs = [next(it) for _ in range(n_w)]
    sem = next(it)
    j = pl.program_id(0)
    i = pl.program_id(1)
    fresh = jnp.logical_or(i == 0, te_ref[i] != te_ref[jnp.maximum(i - 1, 0)])

    def weight_copy(w, expert, col_tile):
        cols = pl.ds(pl.multiple_of(col_tile * bn, bn), bn)
        return pltpu.make_async_copy(w_hbm[w].at[wbase + expert, :, cols], wf_refs[w], sem.at[w])

    @pl.when(fresh)
    def _():
        @pl.when(jnp.logical_and(i == 0, j == 0))
        def _():
            for w in range(n_w):
                weight_copy(w, te_ref[0], 0).start()
        for w in range(n_w):
            weight_copy(w, te_ref[i], j).wait()
            wb_refs[w][...] = wf_refs[w][...].astype(BF16)
        nxt = nx_ref[i]

        @pl.when(nxt >= 0)
        def _():
            for w in range(n_w):
                weight_copy(w, nxt, j).start()

        @pl.when(jnp.logical_and(nxt < 0, j + 1 < pl.num_programs(0)))
        def _():
            for w in range(n_w):
                weight_copy(w, te_ref[0], j + 1).start()

    @pl.when(tv_ref[i] == 1)
    def _():
        x = x_ref[...]
        y = jnp.dot(x, wb_refs[0][...], preferred_element_type=F32)
        if swiglu:
            u = jnp.dot(x, wb_refs[1][...], preferred_element_type=F32)
            y = (y * jax.nn.sigmoid(y)) * u
        o_ref[...] = y.astype(o_ref.dtype)

    @pl.when(tv_ref[i] == 0)
    def _():
        o_ref[...] = jnp.zeros_like(o_ref)


def _grouped_matmul(x, ws, tile_expert, tile_valid, next_expert, wbase, *, bn, out_dtype, swiglu,
                    name):
    m, k = x.shape
    n = ws[0].shape[2]
    tm = MOE_TM
    assert m % tm == 0 and n % bn == 0
    in_specs = [pl.BlockSpec((tm, k), lambda j, i, te, tv, nx: (i, 0))]
    in_specs += [pl.BlockSpec(memory_space=pl.ANY) for _ in ws]
    out_bytes = jnp.dtype(out_dtype).itemsize
    vmem = (2 * tm * k * 2 + len(ws) * k * bn * (4 + 2) + 2 * tm * bn * out_bytes
            + 4 * tm * bn * 4)
    body = functools.partial(_gmm_body, n_w=len(ws), swiglu=swiglu, wbase=wbase, bn=bn)
    return pl.pallas_call(
        body,
        grid_spec=pltpu.PrefetchScalarGridSpec(
            num_scalar_prefetch=3,
            grid=(n // bn, m // tm),
            in_specs=in_specs,
            out_specs=pl.BlockSpec((tm, bn), lambda j, i, te, tv, nx: (i, j)),
            scratch_shapes=([pltpu.VMEM((k, bn), BF16) for _ in ws]
                            + [pltpu.VMEM((k, bn), F32) for _ in ws]
                            + [pltpu.SemaphoreType.DMA((len(ws),))]),
        ),
        out_shape=jax.ShapeDtypeStruct((m, n), out_dtype),
        compiler_params=_params(2, vmem // (1024 * 1024) + 4),
        name=name,
    )(tile_expert, tile_valid, next_expert, x, *ws)


def _combine_ln_body(pos_ref, h_ref, w_ref, g_ref, b_ref, y_hbm, o_ref, ob_ref, buf, sem,
                     *, alpha, rows, tp, pad):
    step = pl.program_id(0)

    def start_rows(st, slot):
        def issue(g, c):
            for u in range(DMA_UNROLL // TOP_K):
                r = g * (DMA_UNROLL // TOP_K) + u
                for s in range(TOP_K):
                    p = pos_ref[(st * rows + r) * TOP_K + s]
                    pltpu.make_async_copy(y_hbm.at[pl.ds(p, 1), :],
                                          buf.at[slot, s, pl.ds(r, 1), :],
                                          sem.at[slot]).start(priority=s % 2)
            return c
        lax.fori_loop(0, rows // (DMA_UNROLL // TOP_K), issue, 0)

    @pl.when(step == 0)
    def _():
        start_rows(step, 0)

    @pl.when(step + 1 < pl.num_programs(0))
    def _():
        start_rows(step + 1, lax.rem(step + 1, 2))

    slot = lax.rem(step, 2)
    for s in range(TOP_K):
        pltpu.make_async_copy(y_hbm.at[pl.ds(0, rows), :], buf.at[slot, s], sem.at[slot]).wait()

    w = w_ref[...]
    lane = lax.broadcasted_iota(jnp.int32, w.shape, 1)
    f = None
    for s in range(TOP_K):
        ws = jnp.sum(jnp.where(lane == s, w, 0.0), axis=-1, keepdims=True)
        term = ws * buf[slot, s]
        f = term if f is None else f + term
    y = _layernorm_rows(alpha * h_ref[...] + f, g_ref[...], b_ref[...], rows=rows, tp=tp, pad=pad)
    o_ref[...] = y
    ob_ref[...] = y.astype(BF16)


def _combine_ln(h, ys, pos, wts, g, b, layer, *, alpha, tp, pad):
    mp, d = h.shape
    rows = _pick(tp, (384, 256, 128, 64, 32, 16))
    body = functools.partial(_combine_ln_body, alpha=alpha, rows=rows, tp=tp, pad=pad)
    row_spec = pl.BlockSpec((rows, d), lambda i, p: (i, 0))
    par_spec = pl.BlockSpec((None, 1, d), lambda i, p: (layer, 0, 0))
    return pl.pallas_call(
        body,
        grid_spec=pltpu.PrefetchScalarGridSpec(
            num_scalar_prefetch=1,
            grid=(mp // rows,),
            in_specs=[row_spec, pl.BlockSpec((rows, ROUTER_LANES), lambda i, p: (i, 0)),
                      par_spec, par_spec, pl.BlockSpec(memory_space=pl.ANY)],
            out_specs=[row_spec, row_spec],
            scratch_shapes=[pltpu.VMEM((2, TOP_K, rows, d), F32), pltpu.SemaphoreType.DMA((2,))],
        ),
        out_shape=[jax.ShapeDtypeStruct((mp, d), F32), jax.ShapeDtypeStruct((mp, d), BF16)],
        compiler_params=_params(1, 40),
        name="moe_combine_layernorm",
    )(pos, h, wts, g, b, ys)


def _route(idx, *, n_experts, tp, pad):
    mp = idx.shape[0]
    tm = MOE_TM
    n_assign = mp * TOP_K
    n_tiles = -(-n_assign // tm) + n_experts
    n_rows = n_tiles * tm
    token = jnp.arange(mp, dtype=jnp.int32)
    real = (token % tp) >= pad
    e = jnp.where(real[:, None], idx[:, :TOP_K], n_experts).reshape(n_assign)
    onehot = (e[:, None] == jnp.arange(n_experts, dtype=jnp.int32)[None, :]).astype(jnp.int32)
    csum = jnp.cumsum(onehot, axis=0)
    counts = csum[-1]
    rank = jnp.sum(onehot * csum, axis=1) - 1
    gsz = ((counts + tm - 1) // tm) * tm
    gend = jnp.cumsum(gsz)
    gstart = gend - gsz
    routed = e < n_experts
    dst = jnp.where(routed, gstart[jnp.minimum(e, n_experts - 1)] + rank, n_rows)
    assign_token = jnp.arange(n_assign, dtype=jnp.int32) // TOP_K
    src = jnp.zeros((n_rows,), jnp.int32).at[dst].set(assign_token, mode="drop")
    pos = jnp.where(routed, dst, 0).astype(jnp.int32)
    tile_start = jnp.arange(n_tiles, dtype=jnp.int32) * tm
    tile_valid = tile_start < gend[-1]
    tile_expert = jnp.sum((tile_start[:, None] >= gend[None, :]).astype(jnp.int32), axis=1)
    last_expert = jnp.max(jnp.where(tile_valid, tile_expert, 0))
    tile_expert = jnp.where(tile_valid, tile_expert, last_expert)
    later = jnp.arange(n_tiles)[None, :] > jnp.arange(n_tiles)[:, None]
    differs = jnp.logical_and(later, tile_expert[None, :] != tile_expert[:, None])
    first = jnp.argmax(differs, axis=1)
    next_expert = jnp.where(jnp.any(differs, axis=1), tile_expert[first], -1).astype(jnp.int32)
    return src, pos, tile_expert, tile_valid.astype(jnp.int32), next_expert, n_rows


def kernel(x, meta_tokens, w_in, ret_gn_g, hg_norm_g, hg_lb_logits, w_ret_out, w_hg_out, w_o,
           ln1_g, ln1_b, ln2_g, ln2_b, ffn_w_gate, ffn_w_up, ffn_w_down,
           moe_router, moe_w_gate, moe_w_up, moe_w_down):
    batch, seq, d = x.shape
    n_meta = meta_tokens.shape[0]
    depth, _, p_in = w_in.shape
    ret_v = w_ret_out.shape[1]
    hg_v = w_hg_out.shape[1]
    hg_k = hg_lb_logits.shape[1]
    ret_qk = (p_in - 2 * ret_v - 2 * hg_k - 2 * hg_v - 2 * d) // 2
    ret_heads = ret_qk // RET_DK
    hg_heads = hg_k // HG_DK
    n_experts = moe_router.shape[2]
    assert ret_v == ret_heads * RET_DV and hg_v == hg_heads * HG_DV and hg_k == hg_v
    assert n_experts <= ROUTER_LANES

    pad = RET_CHUNK - n_meta
    tp = pad + n_meta + seq
    assert tp % RET_CHUNK == 0 and tp % HG_CHUNK == 0
    mp = batch * tp
    alpha = (2 * depth) ** 0.25

    off = {}
    acc = 0
    for nm, wd in (("rq", ret_qk), ("rk", ret_qk), ("rv", ret_v), ("rg", ret_v), ("hq", hg_k),
                   ("hf", hg_k), ("hi", hg_v), ("hg", hg_v), ("ga", d), ("gb", d)):
        off[nm] = acc
        acc += wd

    meta = jnp.broadcast_to(meta_tokens[None].astype(F32), (batch, n_meta, d))
    h = jnp.concatenate([jnp.zeros((batch, pad, d), F32), meta, x.astype(F32)], axis=1)
    h = h.reshape(mp, d)
    hb = h.astype(BF16)

    pos = jnp.arange(tp, dtype=F32) - float(pad)
    inv = 1.0 / (ROPE_BASE ** jnp.linspace(0.0, 1.0, RET_DK // 2, dtype=F32))
    ang = pos[:, None] * inv[None, :]
    cos, sin = jnp.cos(ang), jnp.sin(ang)
    log_gamma = jnp.log1p(-jnp.exp2(-5.0 - jnp.arange(ret_heads, dtype=F32)))

    ret_gn3 = ret_gn_g.astype(F32).reshape(depth, 1, ret_v)
    hg_gn3 = hg_norm_g.astype(F32).reshape(depth, 1, hg_v)
    ln1_g3, ln1_b3 = ln1_g.reshape(depth, 1, d), ln1_b.reshape(depth, 1, d)
    ln2_g3, ln2_b3 = ln2_g.reshape(depth, 1, d), ln2_b.reshape(depth, 1, d)
    n_moe = moe_router.shape[0]
    d_ffe = moe_w_gate.shape[3]
    router_w = jnp.pad(moe_router.astype(F32), ((0, 0), (0, 0), (0, ROUTER_LANES - n_experts)))
    moe_wg = moe_w_gate.reshape(n_moe * n_experts, d, d_ffe)
    moe_wu = moe_w_up.reshape(n_moe * n_experts, d, d_ffe)
    moe_wd = moe_w_down.reshape(n_moe * n_experts, d_ffe, d)

    bm_half = _pick(mp, (528, 384, 256, 128, 64, 32, 16))
    tile = lambda n, cands: _pick(n, cands)

    for l in range(depth):
        proj = _matmul_rows_resident(hb, [w_in], l, rows=tp, skip=pad,
                                     bn=tile(p_in, (512, 256, 128)), out_dtype=BF16,
                                     name="in_proj")
        yr = _retention(proj, cos, sin, log_gamma, ret_gn3, l, batch=batch, tp=tp,
                        heads=ret_heads, off_q=off["rq"], off_k=off["rk"], off_v=off["rv"],
                        off_g=off["rg"])
        yh, t_r = _hgrn2_with_ret_out(
            proj, hg_lb_logits.astype(F32), hg_gn3, yr, w_ret_out, l, batch=batch, tp=tp,
            heads=hg_heads, off_q=off["hq"], off_f=off["hf"], off_i=off["hi"], off_g=off["hg"],
            off_ga=off["ga"])
        merged = _matmul_gate_add(yh, w_hg_out, l, (proj, off["gb"]), t_r, tp=tp,
                                  name="hg_out_proj")
        h, hb = _matmul_add_ln(merged, w_o, l, h, ln1_g3, ln1_b3, alpha=alpha, tp=tp, pad=pad)
        if l % 2 == 0:
            d_ff = ffn_w_gate.shape[2]
            act = _matmul_rows_resident(hb, [ffn_w_gate, ffn_w_up], l // 2, rows=tp, skip=pad,
                                        bn=tile(d_ff, (256, 128)), out_dtype=BF16, name="ffn_up")
            f = _matmul(act, ffn_w_down, l // 2, bm=bm_half, bn=tile(d, (512, 256, 128)),
                        out_dtype=F32, name="ffn_down")
            h, hb = _add_ln(h, f, ln2_g3, ln2_b3, l, alpha=alpha, tp=tp, pad=pad)
        else:
            idx, wts = _router(h, router_w, l // 2, n_experts, tp)
            src, pos, tile_expert, tile_valid, next_expert, n_rows = _route(
                idx, n_experts=n_experts, tp=tp, pad=pad)
            xs = _gather_rows(h, src, tile_valid, n_rows=n_rows)
            wbase = (l // 2) * n_experts
            act = _grouped_matmul(xs, [moe_wg, moe_wu], tile_expert, tile_valid, next_expert, wbase,
                                  bn=tile(d_ffe, (1408, 256, 128)), out_dtype=BF16, swiglu=True,
                                  name="moe_up")
            ys = _grouped_matmul(act, [moe_wd], tile_expert, tile_valid, next_expert, wbase,
                                 bn=tile(d, (1024, 512, 256, 128)), out_dtype=F32, swiglu=False,
                                 name="moe_down")
            h, hb = _combine_ln(h, ys, pos, wts, ln2_g3, ln2_b3, l, alpha=alpha, tp=tp, pad=pad)

    out = h.reshape(batch, tp, d)[:, pad + n_meta:, :]
    return out.astype(x.dtype)
```

```python
import functools

import jax
import jax.numpy as jnp
from jax import lax
from jax.experimental import pallas as pl
from jax.experimental.pallas import tpu as pltpu

F32 = jnp.float32
BF16 = jnp.bfloat16

RET_DK = 256
RET_DV = 512
RET_CHUNK = 128
HG_DK = 128
HG_DV = 128
HG_CHUNK = 64
TOP_K = 2
ROPE_BASE = 10000.0
LN_EPS = 1e-5
ROUTER_LANES = 128
VMEM_LIMIT_CAP_MB = 56


def _pick(n, candidates):
    for c in candidates:
        if n % c == 0:
            return c
    raise ValueError(f"no tile in {candidates} divides {n}")


def _params(n_axes, vmem_mb):
    return pltpu.CompilerParams(
        dimension_semantics=("arbitrary",) * n_axes,
        vmem_limit_bytes=min(vmem_mb, VMEM_LIMIT_CAP_MB) * 1024 * 1024)


def _mm_body(x_ref, w_ref, o_ref, wb_ref):
    @pl.when(pl.program_id(1) == 0)
    def _():
        wb_ref[...] = w_ref[...].astype(BF16)

    o_ref[...] = jnp.dot(x_ref[...], wb_ref[...], preferred_element_type=F32).astype(o_ref.dtype)


def _matmul(x, w, widx, *, bm, bn, out_dtype, name):
    m, k = x.shape
    n = w.shape[2]
    assert m % bm == 0 and n % bn == 0, (m, bm, n, bn)
    out_bytes = jnp.dtype(out_dtype).itemsize
    vmem = 2 * bm * k * 2 + k * bn * (2 * 4 + 2) + 2 * bm * bn * out_bytes + 6 * bm * bn * 4
    return pl.pallas_call(
        _mm_body,
        grid=(n // bn, m // bm),
        in_specs=[pl.BlockSpec((bm, k), lambda j, i: (i, 0)),
                  pl.BlockSpec((None, k, bn), lambda j, i: (widx, 0, j))],
        out_specs=pl.BlockSpec((bm, bn), lambda j, i: (i, j)),
        out_shape=jax.ShapeDtypeStruct((m, n), out_dtype),
        scratch_shapes=[pltpu.VMEM((k, bn), BF16)],
        compiler_params=_params(2, vmem // (1024 * 1024) + 4),
        name=name,
    )(x, w)


def _mm_xres_body(*refs, n_w, sub, skip):
    x_ref, w_refs, o_ref = refs[0], refs[1:1 + n_w], refs[1 + n_w]
    wbs = [w_ref[...].astype(BF16) for w_ref in w_refs]
    rows = x_ref.shape[0]
    o_ref[0:skip, :] = jnp.zeros((skip, o_ref.shape[1]), o_ref.dtype)
    for r0 in range(skip, rows, sub):
        rs = slice(r0, min(r0 + sub, rows))
        y = jnp.dot(x_ref[rs, :], wbs[0], preferred_element_type=F32)
        if n_w == 2:
            u = jnp.dot(x_ref[rs, :], wbs[1], preferred_element_type=F32)
            y = (y * jax.nn.sigmoid(y)) * u
        o_ref[rs, :] = y.astype(o_ref.dtype)


def _matmul_rows_resident(x, ws, widx, *, rows, skip, bn, out_dtype, name):
    m, k = x.shape
    n = ws[0].shape[2]
    assert m % rows == 0 and n % bn == 0 and skip % 16 == 0
    sub = _pick(rows, (1056, 768, 512, 384, 256, 128, 64, 32, 16))
    out_bytes = jnp.dtype(out_dtype).itemsize
    vmem = (rows * k * 2 + len(ws) * k * bn * (2 * 4 + 2) + 2 * rows * bn * out_bytes
            + (2 + len(ws)) * sub * bn * 4)
    return pl.pallas_call(
        functools.partial(_mm_xres_body, n_w=len(ws), sub=sub, skip=skip),
        grid=(m // rows, n // bn),
        in_specs=([pl.BlockSpec((rows, k), lambda b, j: (b, 0), pipeline_mode=pl.Buffered(1))]
                  + [pl.BlockSpec((None, k, bn), lambda b, j: (widx, 0, j)) for _ in ws]),
        out_specs=pl.BlockSpec((rows, bn), lambda b, j: (b, j)),
        out_shape=jax.ShapeDtypeStruct((m, n), out_dtype),
        compiler_params=_params(2, vmem // (1024 * 1024) + 6),
        name=name,
    )(x, *ws)


def _ln_body(h_ref, m_ref, g_ref, b_ref, o_ref, ob_ref, *, alpha, rows, tp, pad):
    y = _layernorm_rows(alpha * h_ref[...] + m_ref[...], g_ref[...], b_ref[...],
                        rows=rows, tp=tp, pad=pad)
    o_ref[...] = y
    ob_ref[...] = y.astype(BF16)


def _layernorm_rows(z, g, b, *, rows, tp, pad, row0=0):
    mu = jnp.mean(z, axis=-1, keepdims=True)
    zc = z - mu
    var = jnp.mean(zc * zc, axis=-1, keepdims=True)
    y = zc * lax.rsqrt(var + LN_EPS) * g + b
    blk = lax.rem(pl.program_id(0), tp // rows)
    r = blk * rows + row0 + lax.broadcasted_iota(jnp.int32, (z.shape[0], 1), 0)
    return jnp.where(r >= pad, y, 0.0)


RESIDENT_ROWS = 384
RESIDENT_SUB = 128


def _mm_ln_body(x_ref, w_ref, h_ref, g_ref, b_ref, o_ref, ob_ref, wb_ref, *, alpha, rows, sub, tp,
                pad):
    @pl.when(pl.program_id(0) == 0)
    def _():
        wb_ref[...] = w_ref[...].astype(BF16)

    def finish(r0, m):
        rs = slice(r0, r0 + sub)
        y = _layernorm_rows(alpha * h_ref[rs, :] + m, g_ref[...], b_ref[...], rows=rows, tp=tp,
                            pad=pad, row0=r0)
        o_ref[rs, :] = y
        ob_ref[rs, :] = y.astype(BF16)

    prev = None
    for r0 in range(0, rows, sub):
        m = jnp.dot(x_ref[r0:r0 + sub, :], wb_ref[...], preferred_element_type=F32)
        if prev is not None:
            finish(*prev)
        prev = (r0, m)
    finish(*prev)


def _mm_gate_add_body(x_ref, w_ref, gate_ref, add_ref, o_ref, wb_ref, *, rows, sub):
    @pl.when(pl.program_id(0) == 0)
    def _():
        wb_ref[...] = w_ref[...].astype(BF16)

    for r0 in range(0, rows, sub):
        rs = slice(r0, r0 + sub)
        m = jnp.dot(x_ref[rs, :], wb_ref[...], preferred_element_type=F32)
        y = add_ref[rs, :] + jax.nn.sigmoid(gate_ref[rs, :].astype(F32)) * m
        o_ref[rs, :] = y.astype(o_ref.dtype)


def _resident_specs(x, w, layer, tp):
    mp, k = x.shape
    d = w.shape[2]
    rows = _pick(tp, (RESIDENT_ROWS, 256, 128, 64, 32, 16))
    sub = _pick(rows, (RESIDENT_SUB, 64, 32, 16))
    x_spec = pl.BlockSpec((rows, k), lambda i: (i, 0))
    w_spec = pl.BlockSpec((None, k, d), lambda i: (layer, 0, 0), pipeline_mode=pl.Buffered(1))
    return mp, k, d, rows, sub, x_spec, w_spec


def _matmul_add_ln(x, w, layer, h, g, b, *, alpha, tp, pad):
    mp, k, d, rows, sub, x_spec, w_spec = _resident_specs(x, w, layer, tp)
    body = functools.partial(_mm_ln_body, alpha=alpha, rows=rows, sub=sub, tp=tp, pad=pad)
    row_spec = pl.BlockSpec((rows, d), lambda i: (i, 0))
    par_spec = pl.BlockSpec((None, 1, d), lambda i: (layer, 0, 0))
    vmem = k * d * (4 + 2) + 2 * rows * k * 2 + rows * d * (2 * 4 + 2 * 4 + 2 * 2 + 3 * 4)
    return pl.pallas_call(
        body,
        grid=(mp // rows,),
        in_specs=[x_spec, w_spec, row_spec, par_spec, par_spec],
        out_specs=[row_spec, row_spec],
        out_shape=[jax.ShapeDtypeStruct((mp, d), F32), jax.ShapeDtypeStruct((mp, d), BF16)],
        scratch_shapes=[pltpu.VMEM((k, d), BF16)],
        compiler_params=_params(1, vmem // (1024 * 1024) + 4),
        name="mix_out_proj_layernorm",
    )(x, w, h, g, b)


def _matmul_gate_add(x, w, layer, gate, add, *, tp, name):
    mp, k, d, rows, sub, x_spec, w_spec = _resident_specs(x, w, layer, tp)
    garr, goff = gate
    assert goff % d == 0
    gblk = goff // d
    body = functools.partial(_mm_gate_add_body, rows=rows, sub=sub)
    row_spec = pl.BlockSpec((rows, d), lambda i: (i, 0))
    vmem = k * d * (4 + 2) + 2 * rows * k * 2 + rows * d * (2 * 2 + 2 * 4 + 2 * 2 + 3 * 4)
    return pl.pallas_call(
        body,
        grid=(mp // rows,),
        in_specs=[x_spec, w_spec, pl.BlockSpec((rows, d), lambda i: (i, gblk)), row_spec],
        out_specs=row_spec,
        out_shape=jax.ShapeDtypeStruct((mp, d), BF16),
        scratch_shapes=[pltpu.VMEM((k, d), BF16)],
        compiler_params=_params(1, vmem // (1024 * 1024) + 4),
        name=name,
    )(x, w, garr, add)


def _add_ln(h, m, g, b, layer, *, alpha, tp, pad):
    mp, d = h.shape
    rows = _pick(tp, (528, 384, 256, 128, 64, 32, 16))
    body = functools.partial(_ln_body, alpha=alpha, rows=rows, tp=tp, pad=pad)
    row_spec = pl.BlockSpec((rows, d), lambda i: (i, 0))
    par_spec = pl.BlockSpec((None, 1, d), lambda i: (layer, 0, 0))
    return pl.pallas_call(
        body,
        grid=(mp // rows,),
        in_specs=[row_spec, row_spec, par_spec, par_spec],
        out_specs=[row_spec, row_spec],
        out_shape=[jax.ShapeDtypeStruct((mp, d), F32), jax.ShapeDtypeStruct((mp, d), BF16)],
        compiler_params=_params(1, 40),
        name="add_layernorm",
    )(h, m, g, b)


def _ret_body(lg_ref, q_ref, k_ref, v_ref, g_ref, cos_ref, sin_ref, gn_ref, o_ref, r_ref,
              *, n_chunks, hpg):
    c_len = RET_CHUNK
    half = RET_DK // 2
    hs = range(hpg)
    ksl = [slice(i * RET_DK, (i + 1) * RET_DK) for i in hs]
    vsl = [slice(i * RET_DV, (i + 1) * RET_DV) for i in hs]

    @pl.when(pl.program_id(2) == 0)
    def _():
        r_ref[...] = jnp.zeros_like(r_ref)

    t_col = lax.broadcasted_iota(jnp.int32, (c_len, 1), 0).astype(F32)
    rel = (lax.broadcasted_iota(jnp.int32, (c_len, c_len), 0)
           - lax.broadcasted_iota(jnp.int32, (c_len, c_len), 1))
    rel_f = jnp.maximum(rel, 0).astype(F32)
    lg = [lg_ref[pl.program_id(1) * hpg + i] for i in hs]
    intra = [jnp.where(rel >= 0, jnp.exp(rel_f * x), 0.0) * (RET_DK ** -0.5) for x in lg]
    q_decay = [jnp.exp((t_col + 1.0) * x) * (RET_DK ** -0.5) for x in lg]
    k_decay = [jnp.exp((c_len - 1.0 - t_col) * x) for x in lg]
    c_decay = [jnp.exp(jnp.full((1, 1), float(c_len), F32) * x) for x in lg]
    gn = gn_ref[...]

    def rot(x, cos, sin):
        x1, x2 = x[:, :half], x[:, half:]
        return jnp.concatenate([x1 * cos - x2 * sin, x1 * sin + x2 * cos], axis=-1)

    def chunk(c, carry):
        rows = pl.ds(pl.multiple_of(c * c_len, c_len), c_len)
        cos = cos_ref[rows, :]
        sin = sin_ref[rows, :]
        q = [rot(q_ref[rows, ksl[i]].astype(F32), cos, sin) for i in hs]
        k = [rot(k_ref[rows, ksl[i]].astype(F32), cos, sin) for i in hs]
        v = [v_ref[rows, vsl[i]] for i in hs]
        s = [lax.dot_general(q[i].astype(BF16), k[i].astype(BF16),
                             (((1,), (1,)), ((), ())), preferred_element_type=F32) * intra[i]
             for i in hs]
        r_old = [r_ref[i] for i in hs]
        o = [jnp.dot(s[i].astype(BF16), v[i], preferred_element_type=F32)
             + jnp.dot((q[i] * q_decay[i]).astype(BF16), r_old[i].astype(BF16),
                       preferred_element_type=F32) for i in hs]
        for i in hs:
            r_ref[i] = r_old[i] * c_decay[i] + lax.dot_general(
                (k[i] * k_decay[i]).astype(BF16), v[i], (((0,), (0,)), ((), ())),
                preferred_element_type=F32)
        for i in hs:
            mu = jnp.mean(o[i], axis=-1, keepdims=True)
            oc = o[i] - mu
            var = jnp.mean(oc * oc, axis=-1, keepdims=True)
            gate = g_ref[rows, vsl[i]].astype(F32)
            y = oc * lax.rsqrt(var + LN_EPS) * gn[:, vsl[i]] * (gate * jax.nn.sigmoid(gate))
            o_ref[rows, vsl[i]] = y.astype(o_ref.dtype)
        return carry

    lax.fori_loop(0, n_chunks, chunk, 0)


def _retention(proj, cos, sin, log_gamma, gn, layer, *, batch, tp, heads, off_q, off_k, off_v,
               off_g):
    mp = proj.shape[0]
    n_chunks_total = tp // RET_CHUNK
    hpg = _pick(heads, (4, 2, 1))
    cpb = _pick(n_chunks_total, (3, 2, 1))
    tb = cpb * RET_CHUNK
    nblk = tp // tb
    gk, gv = hpg * RET_DK, hpg * RET_DV
    assert off_q % gk == 0 and off_k % gk == 0 and off_v % gv == 0 and off_g % gv == 0
    qb, kb, vb, gb = off_q // gk, off_k // gk, off_v // gv, off_g // gv
    row = lambda b, h, j: b * nblk + j
    body = functools.partial(_ret_body, n_chunks=cpb, hpg=hpg)
    return pl.pallas_call(
        body,
        grid=(batch, heads // hpg, nblk),
        in_specs=[
            pl.BlockSpec(memory_space=pltpu.SMEM),
            pl.BlockSpec((tb, gk), lambda b, h, j: (row(b, h, j), qb + h)),
            pl.BlockSpec((tb, gk), lambda b, h, j: (row(b, h, j), kb + h)),
            pl.BlockSpec((tb, gv), lambda b, h, j: (row(b, h, j), vb + h)),
            pl.BlockSpec((tb, gv), lambda b, h, j: (row(b, h, j), gb + h)),
            pl.BlockSpec((tb, RET_DK // 2), lambda b, h, j: (j, 0)),
            pl.BlockSpec((tb, RET_DK // 2), lambda b, h, j: (j, 0)),
            pl.BlockSpec((None, 1, gv), lambda b, h, j: (layer, 0, h)),
        ],
        out_specs=pl.BlockSpec((tb, gv), lambda b, h, j: (row(b, h, j), h)),
        out_shape=jax.ShapeDtypeStruct((mp, heads * RET_DV), BF16),
        scratch_shapes=[pltpu.VMEM((hpg, RET_DK, RET_DV), F32)],
        compiler_params=_params(3, 40),
        name="retention",
    )(log_gamma, proj, proj, proj, proj, cos, sin, gn)


HG_SAFE_RANGE = 85.0


FUSED_ROWS = 192
STAGE_ROWS = 512


def _hgrn_retout_body(lbl_ref, gn_ref, q_ref, f_ref, i_ref, g_ref, yr_ref, ga_ref, w_hbm,
                      o_ref, t_ref, s_ref, acc_ref, bx_ref, qx_ref, kx_ref, vx_ref, wb_ref,
                      wf_ref, sem, *, layer, n_chunks, heads):
    c_len = HG_CHUNK
    first = jnp.logical_and(pl.program_id(0) == 0, pl.program_id(1) == 0)

    @pl.when(first)
    def _():
        n_stage = wb_ref.shape[0] // STAGE_ROWS

        def stage_copy(c, slot):
            return pltpu.make_async_copy(
                w_hbm.at[layer, pl.ds(c * STAGE_ROWS, STAGE_ROWS), :], wf_ref.at[slot],
                sem.at[slot])
        stage_copy(0, 0).start()
        for c in range(n_stage):
            if c + 1 < n_stage:
                stage_copy(c + 1, (c + 1) % 2).start()
            stage_copy(c, c % 2).wait()
            wb_ref[c * STAGE_ROWS:(c + 1) * STAGE_ROWS, :] = wf_ref[c % 2].astype(BF16)

    @pl.when(pl.program_id(1) == 0)
    def _():
        s_ref[...] = jnp.zeros_like(s_ref)

    logits = lbl_ref[...]
    e = jnp.exp(logits - jnp.max(logits, axis=0, keepdims=True))
    sm = e / jnp.sum(e, axis=0, keepdims=True)
    lb = jnp.zeros_like(sm[0:1, :])
    for j in range(1, layer + 1):
        lb = lb + sm[j:j + 1, :]
    gn = gn_ref[...]
    tri = (lax.broadcasted_iota(jnp.int32, (c_len, c_len), 0)
           >= lax.broadcasted_iota(jnp.int32, (c_len, c_len), 1)).astype(BF16)
    t_idx = lax.broadcasted_iota(jnp.int32, (c_len, 1), 0)
    causal = (lax.broadcasted_iota(jnp.int32, (c_len, c_len), 0)
              >= lax.broadcasted_iota(jnp.int32, (c_len, c_len), 1))
    hs = range(heads)
    ksl = [slice(h * HG_DK, (h + 1) * HG_DK) for h in hs]
    vsl = [slice(h * HG_DV, (h + 1) * HG_DV) for h in hs]

    def head_terms(rows, h):
        hq = q_ref[rows, ksl[h]].astype(F32)
        q = hq * jax.nn.sigmoid(hq)
        f = lb[:, ksl[h]] + (1.0 - lb[:, ksl[h]]) * jax.nn.sigmoid(f_ref[rows, ksl[h]].astype(F32))
        logf = jnp.log(f)
        p0 = logf.astype(BF16)
        r1 = logf - p0.astype(F32)
        p1 = r1.astype(BF16)
        p2 = (r1 - p1.astype(F32)).astype(BF16)
        b = (jnp.dot(tri, p0, preferred_element_type=F32)
             + jnp.dot(tri, p1, preferred_element_type=F32)
             + jnp.dot(tri, p2, preferred_element_type=F32))
        return q, 1.0 - f, b

    def write_out(rows, h, o):
        gate = g_ref[rows, vsl[h]].astype(F32)
        gate = gate * jax.nn.sigmoid(gate)
        y = o * lax.rsqrt(jnp.mean(o * o, axis=-1, keepdims=True) + LN_EPS)
        o_ref[rows, vsl[h]] = (y * gn[:, vsl[h]] * gate).astype(o_ref.dtype)

    n_rows = n_chunks * c_len
    col_tile = 256
    pending = [slice(c0, c0 + col_tile) for c0 in range(0, t_ref.shape[1], col_tile)]

    def project_some(n=1):
        for _ in range(n):
            if pending:
                cs = pending.pop(0)
                m = jnp.dot(yr_ref[...], wb_ref[:, cs], preferred_element_type=F32)
                t_ref[:, cs] = jax.nn.sigmoid(ga_ref[:, cs].astype(F32)) * m

    def chunk_main(rows):
        terms = [head_terms(rows, h) for h in hs]
        project_some()
        b_last = [t[2][c_len - 1:c_len, :] for t in terms]
        q_in = [t[0] * jnp.exp(t[2]) for t in terms]
        k_out = [t[1] * jnp.exp(bl - t[2]) for t, bl in zip(terms, b_last)]
        un_mid = [jnp.where(bl < -HG_SAFE_RANGE, 0.0, jnp.exp(-0.5 * bl)) for bl in b_last]
        q_mid = [(x * u).astype(BF16) for x, u in zip(q_in, un_mid)]
        k_mid = [(x * u).astype(BF16) for x, u in zip(k_out, un_mid)]
        sc = [lax.dot_general(qm, km, (((1,), (1,)), ((), ())), preferred_element_type=F32)
              for qm, km in zip(q_mid, k_mid)]
        a = [jnp.where(causal, x, 0.0).astype(BF16) for x in sc]
        project_some()
        vh = [i_ref[rows, vs] for vs in vsl]
        st = [s_ref[h] for h in hs]
        o = [jnp.dot(a[h], vh[h], preferred_element_type=F32)
             + lax.dot_general(q_in[h].astype(BF16), st[h].astype(BF16), (((1,), (1,)), ((), ())),
                               preferred_element_type=F32) for h in hs]
        for h in hs:
            s_ref[h] = st[h] * jnp.exp(b_last[h]) + lax.dot_general(
                vh[h], k_out[h].astype(BF16), (((0,), (0,)), ((), ())),
                preferred_element_type=F32)
        b_min = b_last[0]
        for h in hs:
            acc_ref[rows, vsl[h]] = o[h]
            write_out(rows, h, o[h])
            b_min = jnp.minimum(b_min, b_last[h])
        project_some()
        return b_min

    step_min = None
    for c in range(n_chunks):
        b_min = chunk_main(slice(c * c_len, (c + 1) * c_len))
        step_min = b_min if step_min is None else jnp.minimum(step_min, b_min)
    project_some(len(pending))

    @pl.when(jnp.min(step_min) < -HG_SAFE_RANGE)
    def _():
        def chunk_exact(c, carry):
            rows = pl.ds(pl.multiple_of(c * c_len, c_len), c_len)
            for h in hs:
                q, k, b = head_terms(rows, h)
                bx_ref[:, ksl[h]] = b
                qx_ref[:, ksl[h]] = jnp.where(b[c_len - 1:c_len, :] < -HG_SAFE_RANGE, q, 0.0)
                kx_ref[:, ksl[h]] = k
                vx_ref[:, vsl[h]] = i_ref[rows, vsl[h]].astype(F32)

            def pair(s, cc):
                bs = bx_ref[pl.ds(s, 1), :]
                d = jnp.where(t_idx >= s, jnp.exp(jnp.minimum(bx_ref[...] - bs, 0.0)), 0.0)
                p = qx_ref[...] * d * kx_ref[pl.ds(s, 1), :]
                v_s = vx_ref[pl.ds(s, 1), :]
                for h in hs:
                    acc_ref[rows, vsl[h]] += (jnp.sum(p[:, ksl[h]], axis=-1, keepdims=True)
                                              * v_s[:, vsl[h]])
                return cc
            lax.fori_loop(0, c_len, pair, 0)
            for h in hs:
                write_out(rows, h, acc_ref[rows, vsl[h]])
            return carry
        lax.fori_loop(0, n_chunks, chunk_exact, 0)


def _hgrn2_with_ret_out(proj, lb_logits, gn, yr, w_ret, layer, *, batch, tp, heads, off_q, off_f,
                        off_i, off_g, off_ga):
    mp = proj.shape[0]
    hk, hv = heads * HG_DK, heads * HG_DV
    kr, d = w_ret.shape[1], w_ret.shape[2]
    rb = _pick(tp, (FUSED_ROWS, 128, 64))
    nblk = tp // rb
    depth = lb_logits.shape[0]
    assert kr % STAGE_ROWS == 0 and off_ga % d == 0
    body = functools.partial(_hgrn_retout_body, layer=layer, n_chunks=rb // HG_CHUNK, heads=heads)
    slab = pltpu.VMEM((HG_CHUNK, hv), F32)
    row = lambda b, j: b * nblk + j
    vmem = (kr * d * 2 + 2 * STAGE_ROWS * d * 4 + rb * hv * 4 + 4 * HG_CHUNK * hv * 4
            + heads * HG_DV * HG_DK * 4
            + 2 * rb * (2 * hk * 2 + 2 * hv * 2 + kr * 2 + d * 2 + hv * 2 + d * 4) + 3 * rb * d * 4)
    return pl.pallas_call(
        body,
        grid=(batch, nblk),
        in_specs=[
            pl.BlockSpec((depth, hk), lambda b, j: (0, 0)),
            pl.BlockSpec((None, 1, hv), lambda b, j: (layer, 0, 0)),
            pl.BlockSpec((rb, hk), lambda b, j: (row(b, j), off_q // hk)),
            pl.BlockSpec((rb, hk), lambda b, j: (row(b, j), off_f // hk)),
            pl.BlockSpec((rb, hv), lambda b, j: (row(b, j), off_i // hv)),
            pl.BlockSpec((rb, hv), lambda b, j: (row(b, j), off_g // hv)),
            pl.BlockSpec((rb, kr), lambda b, j: (row(b, j), 0)),
            pl.BlockSpec((rb, d), lambda b, j: (row(b, j), off_ga // d)),
            pl.BlockSpec(memory_space=pl.ANY),
        ],
        out_specs=[pl.BlockSpec((rb, hv), lambda b, j: (row(b, j), 0)),
                   pl.BlockSpec((rb, d), lambda b, j: (row(b, j), 0))],
        out_shape=[jax.ShapeDtypeStruct((mp, hv), BF16), jax.ShapeDtypeStruct((mp, d), F32)],
        scratch_shapes=[pltpu.VMEM((heads, HG_DV, HG_DK), F32), pltpu.VMEM((rb, hv), F32),
                        slab, slab, slab, slab, pltpu.VMEM((kr, d), BF16),
                        pltpu.VMEM((2, STAGE_ROWS, d), F32), pltpu.SemaphoreType.DMA((2,))],
        compiler_params=_params(2, vmem // (1024 * 1024) + 4),
        name="hgrn2_ret_out_proj",
    )(lb_logits, gn, proj, proj, proj, proj, yr, proj, w_ret)


def _router_body(x_ref, w_ref, idx_ref, wts_ref, *, n_experts):
    x = x_ref[...]
    w = w_ref[...]
    xh = x.astype(BF16)
    xl = (x - xh.astype(F32)).astype(BF16)
    wh = w.astype(BF16)
    wl = (w - wh.astype(F32)).astype(BF16)
    logits = (jnp.dot(xh, wh, preferred_element_type=F32)
              + jnp.dot(xh, wl, preferred_element_type=F32)
              + jnp.dot(xl, wh, preferred_element_type=F32))
    lane = lax.broadcasted_iota(jnp.int32, logits.shape, 1)
    neg = jnp.float32(-jnp.inf)
    logits = jnp.where(lane < n_experts, logits, neg)
    v1 = jnp.max(logits, axis=-1, keepdims=True)
    i1 = jnp.min(jnp.where(logits == v1, lane, ROUTER_LANES), axis=-1, keepdims=True)
    rest = jnp.where(lane == i1, neg, logits)
    v2 = jnp.max(rest, axis=-1, keepdims=True)
    i2 = jnp.min(jnp.where(rest == v2, lane, ROUTER_LANES), axis=-1, keepdims=True)
    e2 = jnp.exp(v2 - v1)
    w1 = 1.0 / (1.0 + e2)
    w2 = e2 / (1.0 + e2)
    idx_ref[...] = jnp.where(lane == 0, i1, jnp.where(lane == 1, i2, 0))
    wts_ref[...] = jnp.where(lane == 0, w1, jnp.where(lane == 1, w2, 0.0))


def _router(h, w_router_padded, layer_idx, n_experts, tp):
    mp, d = h.shape
    rows = _pick(tp, (528, 384, 256, 128, 64, 32, 16))
    body = functools.partial(_router_body, n_experts=n_experts)
    out_spec = pl.BlockSpec((rows, ROUTER_LANES), lambda i: (i, 0))
    return pl.pallas_call(
        body,
        grid=(mp // rows,),
        in_specs=[pl.BlockSpec((rows, d), lambda i: (i, 0)),
                  pl.BlockSpec((None, d, ROUTER_LANES), lambda i: (layer_idx, 0, 0))],
        out_specs=[out_spec, out_spec],
        out_shape=[jax.ShapeDtypeStruct((mp, ROUTER_LANES), jnp.int32),
                   jax.ShapeDtypeStruct((mp, ROUTER_LANES), F32)],
        compiler_params=_params(1, 40),
        name="moe_router",
    )(h, w_router_padded)


MOE_TM = 512
GATHER_ROWS = 256
SUBLANES = 8


def _gather_body(src_ref, tv_ref, x_hbm, o_ref, buf, sem, *, rows, steps_per_tile):
    step = pl.program_id(0)

    def start_rows(st, slot):
        @pl.when(tv_ref[st // steps_per_tile] == 1)
        def _():
            def issue(g, c):
                for u in range(SUBLANES):
                    t = src_ref[st * rows + g * SUBLANES + u]
                    pltpu.make_async_copy(x_hbm.at[pl.ds(t, 1), :],
                                          buf.at[slot, g, pl.ds(u, 1), :],
                                          sem.at[slot]).start(priority=u % 2)
                return c
            lax.fori_loop(0, rows // SUBLANES, issue, 0)

    @pl.when(step == 0)
    def _():
        start_rows(step, 0)

    @pl.when(step + 1 < pl.num_programs(0))
    def _():
        start_rows(step + 1, lax.rem(step + 1, 2))

    slot = lax.rem(step, 2)
    valid = tv_ref[step // steps_per_tile] == 1

    @pl.when(valid)
    def _():
        pltpu.make_async_copy(buf.at[slot], buf.at[slot], sem.at[slot]).wait()
        o_ref[...] = buf[slot].reshape(rows, buf.shape[-1]).astype(o_ref.dtype)

    @pl.when(jnp.logical_not(valid))
    def _():
        o_ref[...] = jnp.zeros_like(o_ref)


def _gather_rows(h, src, tile_valid, *, n_rows):
    mp, d = h.shape
    rows = GATHER_ROWS
    assert n_rows % rows == 0 and MOE_TM % rows == 0
    body = functools.partial(_gather_body, rows=rows, steps_per_tile=MOE_TM // rows)
    return pl.pallas_call(
        body,
        grid_spec=pltpu.PrefetchScalarGridSpec(
            num_scalar_prefetch=2,
            grid=(n_rows // rows,),
            in_specs=[pl.BlockSpec(memory_space=pl.ANY)],
            out_specs=pl.BlockSpec((rows, d), lambda i, src, tv: (i, 0)),
            scratch_shapes=[pltpu.VMEM((2, rows // SUBLANES, SUBLANES, d), F32),
                            pltpu.SemaphoreType.DMA((2,))],
        ),
        out_shape=jax.ShapeDtypeStruct((n_rows, d), BF16),
        compiler_params=_params(1, 32),
        name="moe_gather",
    )(src, tile_valid, h)


def _gmm_body(te_ref, tv_ref, nx_ref, *refs, n_w, swiglu, wbase, bn):
    it = iter(refs)
    x_ref = next(it)
    w_hbm = [next(it) for _ in range(n_w)]
    o_ref = next(it)
    wb_refs = [next(it) for _ in range(n_w)]
    wf_refs = [next(it) for _ in range(n_w)]
    sem = next(it)
    j = pl.program_id(0)
    i = pl.program_id(1)
    fresh = jnp.logical_or(i == 0, te_ref[i] != te_ref[jnp.maximum(i - 1, 0)])

    def weight_copy(w, expert, col_tile):
        cols = pl.ds(pl.multiple_of(col_tile * bn, bn), bn)
        return pltpu.make_async_copy(w_hbm[w].at[wbase + expert, :, cols], wf_refs[w], sem.at[w])

    @pl.when(fresh)
    def _():
        @pl.when(jnp.logical_and(i == 0, j == 0))
        def _():
            for w in range(n_w):
                weight_copy(w, te_ref[0], 0).start()
        for w in range(n_w):
            weight_copy(w, te_ref[i], j).wait()
            wb_refs[w][...] = wf_refs[w][...].astype(BF16)
        nxt = nx_ref[i]

        @pl.when(nxt >= 0)
        def _():
            for w in range(n_w):
                weight_copy(w, nxt, j).start()

        @pl.when(jnp.logical_and(nxt < 0, j + 1 < pl.num_programs(0)))
        def _():
            for w in range(n_w):
                weight_copy(w, te_ref[0], j + 1).start()

    @pl.when(tv_ref[i] == 1)
    def _():
        x = x_ref[...]
        y = jnp.dot(x, wb_refs[0][...], preferred_element_type=F32)
        if swiglu:
            u = jnp.dot(x, wb_refs[1][...], preferred_element_type=F32)
            y = (y * jax.nn.sigmoid(y)) * u
        o_ref[...] = y.astype(o_ref.dtype)

    @pl.when(tv_ref[i] == 0)
    def _():
        o_ref[...] = jnp.zeros_like(o_ref)


def _grouped_matmul(x, ws, tile_expert, tile_valid, next_expert, wbase, *, bn, out_dtype, swiglu,
                    name):
    m, k = x.shape
    n = ws[0].shape[2]
    tm = MOE_TM
    assert m % tm == 0 and n % bn == 0
    in_specs = [pl.BlockSpec((tm, k), lambda j, i, te, tv, nx: (i, 0))]
    in_specs += [pl.BlockSpec(memory_space=pl.ANY) for _ in ws]
    out_bytes = jnp.dtype(out_dtype).itemsize
    vmem = (2 * tm * k * 2 + len(ws) * k * bn * (4 + 2) + 2 * tm * bn * out_bytes
            + 4 * tm * bn * 4)
    body = functools.partial(_gmm_body, n_w=len(ws), swiglu=swiglu, wbase=wbase, bn=bn)
    return pl.pallas_call(
        body,
        grid_spec=pltpu.PrefetchScalarGridSpec(
            num_scalar_prefetch=3,
            grid=(n // bn, m // tm),
            in_specs=in_specs,
            out_specs=pl.BlockSpec((tm, bn), lambda j, i, te, tv, nx: (i, j)),
            scratch_shapes=([pltpu.VMEM((k, bn), BF16) for _ in ws]
                            + [pltpu.VMEM((k, bn), F32) for _ in ws]
                            + [pltpu.SemaphoreType.DMA((len(ws),))]),
        ),
        out_shape=jax.ShapeDtypeStruct((m, n), out_dtype),
        compiler_params=_params(2, vmem // (1024 * 1024) + 4),
        name=name,
    )(tile_expert, tile_valid, next_expert, x, *ws)


def _combine_ln_body(pos_ref, h_ref, w_ref, g_ref, b_ref, y_hbm, o_ref, ob_ref, buf, sem,
                     *, alpha, rows, tp, pad):
    step = pl.program_id(0)

    def start_rows(st, slot):
        def issue(g, c):
            for u in range(SUBLANES):
                for s in range(TOP_K):
                    p = pos_ref[(st * rows + g * SUBLANES + u) * TOP_K + s]
                    pltpu.make_async_copy(y_hbm.at[pl.ds(p, 1), :],
                                          buf.at[slot, s, g, pl.ds(u, 1), :],
                                          sem.at[slot]).start(priority=s % 2)
            return c
        lax.fori_loop(0, rows // SUBLANES, issue, 0)

    @pl.when(step == 0)
    def _():
        start_rows(step, 0)

    @pl.when(step + 1 < pl.num_programs(0))
    def _():
        start_rows(step + 1, lax.rem(step + 1, 2))

    slot = lax.rem(step, 2)
    for s in range(TOP_K):
        pltpu.make_async_copy(buf.at[slot, s], buf.at[slot, s], sem.at[slot]).wait()

    w = w_ref[...]
    lane = lax.broadcasted_iota(jnp.int32, w.shape, 1)
    f = None
    for s in range(TOP_K):
        ws = jnp.sum(jnp.where(lane == s, w, 0.0), axis=-1, keepdims=True)
        term = ws * buf[slot, s].reshape(rows, buf.shape[-1])
        f = term if f is None else f + term
    y = _layernorm_rows(alpha * h_ref[...] + f, g_ref[...], b_ref[...], rows=rows, tp=tp, pad=pad)
    o_ref[...] = y
    ob_ref[...] = y.astype(BF16)


def _combine_ln(h, ys, pos, wts, g, b, layer, *, alpha, tp, pad):
    mp, d = h.shape
    rows = _pick(tp, (384, 256, 128, 64, 32, 16))
    body = functools.partial(_combine_ln_body, alpha=alpha, rows=rows, tp=tp, pad=pad)
    row_spec = pl.BlockSpec((rows, d), lambda i, p: (i, 0))
    par_spec = pl.BlockSpec((None, 1, d), lambda i, p: (layer, 0, 0))
    return pl.pallas_call(
        body,
        grid_spec=pltpu.PrefetchScalarGridSpec(
            num_scalar_prefetch=1,
            grid=(mp // rows,),
            in_specs=[row_spec, pl.BlockSpec((rows, ROUTER_LANES), lambda i, p: (i, 0)),
                      par_spec, par_spec, pl.BlockSpec(memory_space=pl.ANY)],
            out_specs=[row_spec, row_spec],
            scratch_shapes=[pltpu.VMEM((2, TOP_K, rows // SUBLANES, SUBLANES, d), F32),
                            pltpu.SemaphoreType.DMA((2,))],
        ),
        out_shape=[jax.ShapeDtypeStruct((mp, d), F32), jax.ShapeDtypeStruct((mp, d), BF16)],
        compiler_params=_params(1, 40),
        name="moe_combine_layernorm",
    )(pos, h, wts, g, b, ys)


def _route(idx, *, n_experts, tp, pad):
    mp = idx.shape[0]
    tm = MOE_TM
    n_assign = mp * TOP_K
    n_tiles = -(-n_assign // tm) + n_experts
    n_rows = n_tiles * tm
    token = jnp.arange(mp, dtype=jnp.int32)
    real = (token % tp) >= pad
    e = jnp.where(real[:, None], idx[:, :TOP_K], n_experts).reshape(n_assign)
    onehot = (e[:, None] == jnp.arange(n_experts, dtype=jnp.int32)[None, :]).astype(jnp.int32)
    csum = jnp.cumsum(onehot, axis=0)
    counts = csum[-1]
    rank = jnp.sum(onehot * csum, axis=1) - 1
    gsz = ((counts + tm - 1) // tm) * tm
    gend = jnp.cumsum(gsz)
    gstart = gend - gsz
    routed = e < n_experts
    dst = jnp.where(routed, gstart[jnp.minimum(e, n_experts - 1)] + rank, n_rows)
    assign_token = jnp.arange(n_assign, dtype=jnp.int32) // TOP_K
    src = jnp.zeros((n_rows,), jnp.int32).at[dst].set(assign_token, mode="drop")
    pos = jnp.where(routed, dst, 0).astype(jnp.int32)
    tile_start = jnp.arange(n_tiles, dtype=jnp.int32) * tm
    tile_valid = tile_start < gend[-1]
    tile_expert = jnp.sum((tile_start[:, None] >= gend[None, :]).astype(jnp.int32), axis=1)
    last_expert = jnp.max(jnp.where(tile_valid, tile_expert, 0))
    tile_expert = jnp.where(tile_valid, tile_expert, last_expert)
    later = jnp.arange(n_tiles)[None, :] > jnp.arange(n_tiles)[:, None]
    differs = jnp.logical_and(later, tile_expert[None, :] != tile_expert[:, None])
    first = jnp.argmax(differs, axis=1)
    next_expert = jnp.where(jnp.any(differs, axis=1), tile_expert[first], -1).astype(jnp.int32)
    return src, pos, tile_expert, tile_valid.astype(jnp.int32), next_expert, n_rows


def kernel(x, meta_tokens, w_in, ret_gn_g, hg_norm_g, hg_lb_logits, w_ret_out, w_hg_out, w_o,
           ln1_g, ln1_b, ln2_g, ln2_b, ffn_w_gate, ffn_w_up, ffn_w_down,
           moe_router, moe_w_gate, moe_w_up, moe_w_down):
    batch, seq, d = x.shape
    n_meta = meta_tokens.shape[0]
    depth, _, p_in = w_in.shape
    ret_v = w_ret_out.shape[1]
    hg_v = w_hg_out.shape[1]
    hg_k = hg_lb_logits.shape[1]
    ret_qk = (p_in - 2 * ret_v - 2 * hg_k - 2 * hg_v - 2 * d) // 2
    ret_heads = ret_qk // RET_DK
    hg_heads = hg_k // HG_DK
    n_experts = moe_router.shape[2]
    assert ret_v == ret_heads * RET_DV and hg_v == hg_heads * HG_DV and hg_k == hg_v
    assert n_experts <= ROUTER_LANES

    pad = RET_CHUNK - n_meta
    tp = pad + n_meta + seq
    assert tp % RET_CHUNK == 0 and tp % HG_CHUNK == 0
    mp = batch * tp
    alpha = (2 * depth) ** 0.25

    off = {}
    acc = 0
    for nm, wd in (("rq", ret_qk), ("rk", ret_qk), ("rv", ret_v), ("rg", ret_v), ("hq", hg_k),
                   ("hf", hg_k), ("hi", hg_v), ("hg", hg_v), ("ga", d), ("gb", d)):
        off[nm] = acc
        acc += wd

    meta = jnp.broadcast_to(meta_tokens[None].astype(F32), (batch, n_meta, d))
    h = jnp.concatenate([jnp.zeros((batch, pad, d), F32), meta, x.astype(F32)], axis=1)
    h = h.reshape(mp, d)
    hb = h.astype(BF16)

    pos = jnp.arange(tp, dtype=F32) - float(pad)
    inv = 1.0 / (ROPE_BASE ** jnp.linspace(0.0, 1.0, RET_DK // 2, dtype=F32))
    ang = pos[:, None] * inv[None, :]
    cos, sin = jnp.cos(ang), jnp.sin(ang)
    log_gamma = jnp.log1p(-jnp.exp2(-5.0 - jnp.arange(ret_heads, dtype=F32)))

    ret_gn3 = ret_gn_g.astype(F32).reshape(depth, 1, ret_v)
    hg_gn3 = hg_norm_g.astype(F32).reshape(depth, 1, hg_v)
    ln1_g3, ln1_b3 = ln1_g.reshape(depth, 1, d), ln1_b.reshape(depth, 1, d)
    ln2_g3, ln2_b3 = ln2_g.reshape(depth, 1, d), ln2_b.reshape(depth, 1, d)
    n_moe = moe_router.shape[0]
    d_ffe = moe_w_gate.shape[3]
    router_w = jnp.pad(moe_router.astype(F32), ((0, 0), (0, 0), (0, ROUTER_LANES - n_experts)))
    moe_wg = moe_w_gate.reshape(n_moe * n_experts, d, d_ffe)
    moe_wu = moe_w_up.reshape(n_moe * n_experts, d, d_ffe)
    moe_wd = moe_w_down.reshape(n_moe * n_experts, d_ffe, d)

    bm_half = _pick(mp, (528, 384, 256, 128, 64, 32, 16))
    tile = lambda n, cands: _pick(n, cands)

    for l in range(depth):
        proj = _matmul_rows_resident(hb, [w_in], l, rows=tp, skip=pad,
                                     bn=tile(p_in, (512, 256, 128)), out_dtype=BF16,
                                     name="in_proj")
        yr = _retention(proj, cos, sin, log_gamma, ret_gn3, l, batch=batch, tp=tp,
                        heads=ret_heads, off_q=off["rq"], off_k=off["rk"], off_v=off["rv"],
                        off_g=off["rg"])
        yh, t_r = _hgrn2_with_ret_out(
            proj, hg_lb_logits.astype(F32), hg_gn3, yr, w_ret_out, l, batch=batch, tp=tp,
            heads=hg_heads, off_q=off["hq"], off_f=off["hf"], off_i=off["hi"], off_g=off["hg"],
            off_ga=off["ga"])
        merged = _matmul_gate_add(yh, w_hg_out, l, (proj, off["gb"]), t_r, tp=tp,
                                  name="hg_out_proj")
        h, hb = _matmul_add_ln(merged, w_o, l, h, ln1_g3, ln1_b3, alpha=alpha, tp=tp, pad=pad)
        if l % 2 == 0:
            d_ff = ffn_w_gate.shape[2]
            act = _matmul_rows_resident(hb, [ffn_w_gate, ffn_w_up], l // 2, rows=tp, skip=pad,
                                        bn=tile(d_ff, (256, 128)), out_dtype=BF16, name="ffn_up")
            f = _matmul(act, ffn_w_down, l // 2, bm=bm_half, bn=tile(d, (512, 256, 128)),
                        out_dtype=F32, name="ffn_down")
            h, hb = _add_ln(h, f, ln2_g3, ln2_b3, l, alpha=alpha, tp=tp, pad=pad)
        else:
            idx, wts = _router(h, router_w, l // 2, n_experts, tp)
            src, pos, tile_expert, tile_valid, next_expert, n_rows = _route(
                idx, n_experts=n_experts, tp=tp, pad=pad)
            xs = _gather_rows(h, src, tile_valid, n_rows=n_rows)
            wbase = (l // 2) * n_experts
            act = _grouped_matmul(xs, [moe_wg, moe_wu], tile_expert, tile_valid, next_expert, wbase,
                                  bn=tile(d_ffe, (1408, 256, 128)), out_dtype=BF16, swiglu=True,
                                  name="moe_up")
            ys = _grouped_matmul(act, [moe_wd], tile_expert, tile_valid, next_expert, wbase,
                                 bn=tile(d, (1024, 512, 256, 128)), out_dtype=F32, swiglu=False,
                                 name="moe_down")
            h, hb = _combine_ln(h, ys, pos, wts, ln2_g3, ln2_b3, l, alpha=alpha, tp=tp, pad=pad)

    out = h.reshape(batch, tp, d)[:, pad + n_meta:, :]
    return out.astype(x.dtype)
```

```python
import functools

import jax
import jax.numpy as jnp
from jax import lax
from jax.experimental import pallas as pl
from jax.experimental.pallas import tpu as pltpu

F32 = jnp.float32
BF16 = jnp.bfloat16

RET_DK = 256
RET_DV = 512
RET_CHUNK = 128
HG_DK = 128
HG_DV = 128
HG_CHUNK = 64
TOP_K = 2
ROPE_BASE = 10000.0
LN_EPS = 1e-5
ROUTER_LANES = 128
VMEM_LIMIT_CAP_MB = 56


def _pick(n, candidates):
    for c in candidates:
        if n % c == 0:
            return c
    raise ValueError(f"no tile in {candidates} divides {n}")


def _params(n_axes, vmem_mb):
    return pltpu.CompilerParams(
        dimension_semantics=("arbitrary",) * n_axes,
        vmem_limit_bytes=min(vmem_mb, VMEM_LIMIT_CAP_MB) * 1024 * 1024)


def _mm_xres_body(*refs, n_w, sub, skip):
    x_ref, w_refs, o_ref = refs[0], refs[1:1 + n_w], refs[1 + n_w]
    wbs = [w_ref[...].astype(BF16) for w_ref in w_refs]
    rows = x_ref.shape[0]
    o_ref[0:skip, :] = jnp.zeros((skip, o_ref.shape[1]), o_ref.dtype)
    for r0 in range(skip, rows, sub):
        rs = slice(r0, min(r0 + sub, rows))
        y = jnp.dot(x_ref[rs, :], wbs[0], preferred_element_type=F32)
        if n_w == 2:
            u = jnp.dot(x_ref[rs, :], wbs[1], preferred_element_type=F32)
            y = (y * jax.nn.sigmoid(y)) * u
        o_ref[rs, :] = y.astype(o_ref.dtype)


def _matmul_rows_resident(x, ws, widx, *, rows, skip, bn, out_dtype, name):
    m, k = x.shape
    n = ws[0].shape[2]
    assert m % rows == 0 and n % bn == 0 and skip % 16 == 0
    sub = _pick(rows, (1056, 768, 512, 384, 256, 128, 64, 32, 16))
    out_bytes = jnp.dtype(out_dtype).itemsize
    vmem = (rows * k * 2 + len(ws) * k * bn * (2 * 4 + 2) + 2 * rows * bn * out_bytes
            + (2 + len(ws)) * sub * bn * 4)
    return pl.pallas_call(
        functools.partial(_mm_xres_body, n_w=len(ws), sub=sub, skip=skip),
        grid=(m // rows, n // bn),
        in_specs=([pl.BlockSpec((rows, k), lambda b, j: (b, 0), pipeline_mode=pl.Buffered(1))]
                  + [pl.BlockSpec((None, k, bn), lambda b, j: (widx, 0, j)) for _ in ws]),
        out_specs=pl.BlockSpec((rows, bn), lambda b, j: (b, j)),
        out_shape=jax.ShapeDtypeStruct((m, n), out_dtype),
        compiler_params=_params(2, vmem // (1024 * 1024) + 6),
        name=name,
    )(x, *ws)


def _layernorm_rows(z, g, b, *, rows, tp, pad, row0=0):
    mu = jnp.mean(z, axis=-1, keepdims=True)
    zc = z - mu
    var = jnp.mean(zc * zc, axis=-1, keepdims=True)
    y = zc * lax.rsqrt(var + LN_EPS) * g + b
    blk = lax.rem(pl.program_id(0), tp // rows)
    r = blk * rows + row0 + lax.broadcasted_iota(jnp.int32, (z.shape[0], 1), 0)
    return jnp.where(r >= pad, y, 0.0)


RESIDENT_ROWS = 384
RESIDENT_SUB = 128


STAGE_ROWS = 512


def _stage_weight(w_hbm, layer, wb_ref, wf_ref, sem):
    n_stage = wb_ref.shape[0] // STAGE_ROWS

    def stage_copy(c, slot):
        return pltpu.make_async_copy(
            w_hbm.at[layer, pl.ds(c * STAGE_ROWS, STAGE_ROWS), :], wf_ref.at[slot], sem.at[slot])
    stage_copy(0, 0).start()
    for c in range(n_stage):
        if c + 1 < n_stage:
            stage_copy(c + 1, (c + 1) % 2).start()
        stage_copy(c, c % 2).wait()
        wb_ref[c * STAGE_ROWS:(c + 1) * STAGE_ROWS, :] = wf_ref[c % 2].astype(BF16)


def _mm_ln_body(x_ref, w_ref, h_ref, g_ref, b_ref, o_ref, ob_ref, wb_ref, *stage, layer, alpha,
                rows, sub, tp, pad):
    @pl.when(pl.program_id(0) == 0)
    def _():
        if stage:
            _stage_weight(w_ref, layer, wb_ref, *stage)
        else:
            wb_ref[...] = w_ref[...].astype(BF16)

    def finish(r0, m):
        rs = slice(r0, r0 + sub)
        y = _layernorm_rows(alpha * h_ref[rs, :] + m, g_ref[...], b_ref[...], rows=rows, tp=tp,
                            pad=pad, row0=r0)
        o_ref[rs, :] = y
        ob_ref[rs, :] = y.astype(BF16)

    prev = None
    for r0 in range(0, rows, sub):
        m = jnp.dot(x_ref[r0:r0 + sub, :], wb_ref[...], preferred_element_type=F32)
        if prev is not None:
            finish(*prev)
        prev = (r0, m)
    finish(*prev)


def _mm_gate_add_body(x_ref, w_ref, gate_ref, add_ref, o_ref, wb_ref, *, rows, sub):
    @pl.when(pl.program_id(0) == 0)
    def _():
        wb_ref[...] = w_ref[...].astype(BF16)

    for r0 in range(0, rows, sub):
        rs = slice(r0, r0 + sub)
        m = jnp.dot(x_ref[rs, :], wb_ref[...], preferred_element_type=F32)
        y = add_ref[rs, :] + jax.nn.sigmoid(gate_ref[rs, :].astype(F32)) * m
        o_ref[rs, :] = y.astype(o_ref.dtype)


def _resident_specs(x, w, layer, tp):
    mp, k = x.shape
    d = w.shape[2]
    rows = _pick(tp, (RESIDENT_ROWS, 256, 128, 64, 32, 16))
    sub = _pick(rows, (RESIDENT_SUB, 64, 32, 16))
    x_spec = pl.BlockSpec((rows, k), lambda i: (i, 0))
    w_spec = pl.BlockSpec((None, k, d), lambda i: (layer, 0, 0), pipeline_mode=pl.Buffered(1))
    return mp, k, d, rows, sub, x_spec, w_spec


def _matmul_add_ln(x, w, w_layer, h, g, b, ln_layer, *, alpha, tp, pad, staged, name):
    mp, k, d, rows, sub, x_spec, w_spec = _resident_specs(x, w, w_layer, tp)
    scratch = [pltpu.VMEM((k, d), BF16)]
    w_bytes = k * d * (4 + 2)
    if staged:
        assert k % STAGE_ROWS == 0
        rows = _pick(tp, (FUSED_ROWS, 128, 64, 32, 16))
        sub = _pick(rows, (RESIDENT_SUB, rows))
        x_spec = pl.BlockSpec((rows, k), lambda i: (i, 0))
        w_spec = pl.BlockSpec(memory_space=pl.ANY)
        scratch += [pltpu.VMEM((2, STAGE_ROWS, d), F32), pltpu.SemaphoreType.DMA((2,))]
        w_bytes = k * d * 2 + 2 * STAGE_ROWS * d * 4
    body = functools.partial(_mm_ln_body, layer=w_layer, alpha=alpha, rows=rows, sub=sub, tp=tp,
                             pad=pad)
    row_spec = pl.BlockSpec((rows, d), lambda i: (i, 0))
    par_spec = pl.BlockSpec((None, 1, d), lambda i: (ln_layer, 0, 0))
    vmem = w_bytes + 2 * rows * k * 2 + rows * d * (2 * 4 + 2 * 4 + 2 * 2 + 3 * 4)
    return pl.pallas_call(
        body,
        grid=(mp // rows,),
        in_specs=[x_spec, w_spec, row_spec, par_spec, par_spec],
        out_specs=[row_spec, row_spec],
        out_shape=[jax.ShapeDtypeStruct((mp, d), F32), jax.ShapeDtypeStruct((mp, d), BF16)],
        scratch_shapes=scratch,
        compiler_params=_params(1, vmem // (1024 * 1024) + 4),
        name=name,
    )(x, w, h, g, b)


def _matmul_gate_add(x, w, layer, gate, add, *, tp, name):
    mp, k, d, rows, sub, x_spec, w_spec = _resident_specs(x, w, layer, tp)
    garr, goff = gate
    assert goff % d == 0
    gblk = goff // d
    body = functools.partial(_mm_gate_add_body, rows=rows, sub=sub)
    row_spec = pl.BlockSpec((rows, d), lambda i: (i, 0))
    vmem = k * d * (4 + 2) + 2 * rows * k * 2 + rows * d * (2 * 2 + 2 * 4 + 2 * 2 + 3 * 4)
    return pl.pallas_call(
        body,
        grid=(mp // rows,),
        in_specs=[x_spec, w_spec, pl.BlockSpec((rows, d), lambda i: (i, gblk)), row_spec],
        out_specs=row_spec,
        out_shape=jax.ShapeDtypeStruct((mp, d), BF16),
        scratch_shapes=[pltpu.VMEM((k, d), BF16)],
        compiler_params=_params(1, vmem // (1024 * 1024) + 4),
        name=name,
    )(x, w, garr, add)


def _ret_body(lg_ref, q_ref, k_ref, v_ref, g_ref, cos_ref, sin_ref, gn_ref, o_ref, r_ref,
              *, n_chunks, hpg):
    c_len = RET_CHUNK
    half = RET_DK // 2
    hs = range(hpg)
    ksl = [slice(i * RET_DK, (i + 1) * RET_DK) for i in hs]
    vsl = [slice(i * RET_DV, (i + 1) * RET_DV) for i in hs]

    @pl.when(pl.program_id(2) == 0)
    def _():
        r_ref[...] = jnp.zeros_like(r_ref)

    t_col = lax.broadcasted_iota(jnp.int32, (c_len, 1), 0).astype(F32)
    rel = (lax.broadcasted_iota(jnp.int32, (c_len, c_len), 0)
           - lax.broadcasted_iota(jnp.int32, (c_len, c_len), 1))
    rel_f = jnp.maximum(rel, 0).astype(F32)
    lg = [lg_ref[pl.program_id(1) * hpg + i] for i in hs]
    intra = [jnp.where(rel >= 0, jnp.exp(rel_f * x), 0.0) * (RET_DK ** -0.5) for x in lg]
    q_decay = [jnp.exp((t_col + 1.0) * x) * (RET_DK ** -0.5) for x in lg]
    k_decay = [jnp.exp((c_len - 1.0 - t_col) * x) for x in lg]
    c_decay = [jnp.exp(jnp.full((1, 1), float(c_len), F32) * x) for x in lg]
    gn = gn_ref[...]

    def rot(x, cos, sin):
        x1, x2 = x[:, :half], x[:, half:]
        return jnp.concatenate([x1 * cos - x2 * sin, x1 * sin + x2 * cos], axis=-1)

    def chunk(c, carry):
        rows = pl.ds(pl.multiple_of(c * c_len, c_len), c_len)
        cos = cos_ref[rows, :]
        sin = sin_ref[rows, :]
        q = [rot(q_ref[rows, ksl[i]].astype(F32), cos, sin) for i in hs]
        k = [rot(k_ref[rows, ksl[i]].astype(F32), cos, sin) for i in hs]
        v = [v_ref[rows, vsl[i]] for i in hs]
        s = [lax.dot_general(q[i].astype(BF16), k[i].astype(BF16),
                             (((1,), (1,)), ((), ())), preferred_element_type=F32) * intra[i]
             for i in hs]
        r_old = [r_ref[i] for i in hs]
        o = [jnp.dot(s[i].astype(BF16), v[i], preferred_element_type=F32)
             + jnp.dot((q[i] * q_decay[i]).astype(BF16), r_old[i].astype(BF16),
                       preferred_element_type=F32) for i in hs]
        for i in hs:
            r_ref[i] = r_old[i] * c_decay[i] + lax.dot_general(
                (k[i] * k_decay[i]).astype(BF16), v[i], (((0,), (0,)), ((), ())),
                preferred_element_type=F32)
        for i in hs:
            mu = jnp.mean(o[i], axis=-1, keepdims=True)
            oc = o[i] - mu
            var = jnp.mean(oc * oc, axis=-1, keepdims=True)
            gate = g_ref[rows, vsl[i]].astype(F32)
            y = oc * lax.rsqrt(var + LN_EPS) * gn[:, vsl[i]] * (gate * jax.nn.sigmoid(gate))
            o_ref[rows, vsl[i]] = y.astype(o_ref.dtype)
        return carry

    lax.fori_loop(0, n_chunks, chunk, 0)


def _retention(proj, cos, sin, log_gamma, gn, layer, *, batch, tp, heads, off_q, off_k, off_v,
               off_g):
    mp = proj.shape[0]
    n_chunks_total = tp // RET_CHUNK
    hpg = _pick(heads, (4, 2, 1))
    cpb = _pick(n_chunks_total, (3, 2, 1))
    tb = cpb * RET_CHUNK
    nblk = tp // tb
    gk, gv = hpg * RET_DK, hpg * RET_DV
    assert off_q % gk == 0 and off_k % gk == 0 and off_v % gv == 0 and off_g % gv == 0
    qb, kb, vb, gb = off_q // gk, off_k // gk, off_v // gv, off_g // gv
    row = lambda b, h, j: b * nblk + j
    body = functools.partial(_ret_body, n_chunks=cpb, hpg=hpg)
    return pl.pallas_call(
        body,
        grid=(batch, heads // hpg, nblk),
        in_specs=[
            pl.BlockSpec(memory_space=pltpu.SMEM),
            pl.BlockSpec((tb, gk), lambda b, h, j: (row(b, h, j), qb + h)),
            pl.BlockSpec((tb, gk), lambda b, h, j: (row(b, h, j), kb + h)),
            pl.BlockSpec((tb, gv), lambda b, h, j: (row(b, h, j), vb + h)),
            pl.BlockSpec((tb, gv), lambda b, h, j: (row(b, h, j), gb + h)),
            pl.BlockSpec((tb, RET_DK // 2), lambda b, h, j: (j, 0)),
            pl.BlockSpec((tb, RET_DK // 2), lambda b, h, j: (j, 0)),
            pl.BlockSpec((None, 1, gv), lambda b, h, j: (layer, 0, h)),
        ],
        out_specs=pl.BlockSpec((tb, gv), lambda b, h, j: (row(b, h, j), h)),
        out_shape=jax.ShapeDtypeStruct((mp, heads * RET_DV), BF16),
        scratch_shapes=[pltpu.VMEM((hpg, RET_DK, RET_DV), F32)],
        compiler_params=_params(3, 40),
        name="retention",
    )(log_gamma, proj, proj, proj, proj, cos, sin, gn)


HG_SAFE_RANGE = 85.0


FUSED_ROWS = 192


def _hgrn_retout_body(lbl_ref, gn_ref, q_ref, f_ref, i_ref, g_ref, yr_ref, ga_ref, w_hbm,
                      o_ref, t_ref, s_ref, acc_ref, bx_ref, qx_ref, kx_ref, vx_ref, wb_ref,
                      wf_ref, sem, *, layer, n_chunks, heads):
    c_len = HG_CHUNK
    first = jnp.logical_and(pl.program_id(0) == 0, pl.program_id(1) == 0)

    @pl.when(first)
    def _():
        _stage_weight(w_hbm, layer, wb_ref, wf_ref, sem)

    @pl.when(pl.program_id(1) == 0)
    def _():
        s_ref[...] = jnp.zeros_like(s_ref)

    logits = lbl_ref[...]
    e = jnp.exp(logits - jnp.max(logits, axis=0, keepdims=True))
    sm = e / jnp.sum(e, axis=0, keepdims=True)
    lb = jnp.zeros_like(sm[0:1, :])
    for j in range(1, layer + 1):
        lb = lb + sm[j:j + 1, :]
    gn = gn_ref[...]
    tri = (lax.broadcasted_iota(jnp.int32, (c_len, c_len), 0)
           >= lax.broadcasted_iota(jnp.int32, (c_len, c_len), 1)).astype(BF16)
    t_idx = lax.broadcasted_iota(jnp.int32, (c_len, 1), 0)
    causal = (lax.broadcasted_iota(jnp.int32, (c_len, c_len), 0)
              >= lax.broadcasted_iota(jnp.int32, (c_len, c_len), 1))
    hs = range(heads)
    ksl = [slice(h * HG_DK, (h + 1) * HG_DK) for h in hs]
    vsl = [slice(h * HG_DV, (h + 1) * HG_DV) for h in hs]

    def head_terms(rows, h):
        hq = q_ref[rows, ksl[h]].astype(F32)
        q = hq * jax.nn.sigmoid(hq)
        f = lb[:, ksl[h]] + (1.0 - lb[:, ksl[h]]) * jax.nn.sigmoid(f_ref[rows, ksl[h]].astype(F32))
        logf = jnp.log(f)
        p0 = logf.astype(BF16)
        r1 = logf - p0.astype(F32)
        p1 = r1.astype(BF16)
        p2 = (r1 - p1.astype(F32)).astype(BF16)
        b = (jnp.dot(tri, p0, preferred_element_type=F32)
             + jnp.dot(tri, p1, preferred_element_type=F32)
             + jnp.dot(tri, p2, preferred_element_type=F32))
        return q, 1.0 - f, b

    def write_out(rows, h, o):
        gate = g_ref[rows, vsl[h]].astype(F32)
        gate = gate * jax.nn.sigmoid(gate)
        y = o * lax.rsqrt(jnp.mean(o * o, axis=-1, keepdims=True) + LN_EPS)
        o_ref[rows, vsl[h]] = (y * gn[:, vsl[h]] * gate).astype(o_ref.dtype)

    n_rows = n_chunks * c_len
    col_tile = 256
    pending = [slice(c0, c0 + col_tile) for c0 in range(0, t_ref.shape[1], col_tile)]

    def project_some(n=1):
        for _ in range(n):
            if pending:
                cs = pending.pop(0)
                m = jnp.dot(yr_ref[...], wb_ref[:, cs], preferred_element_type=F32)
                t_ref[:, cs] = jax.nn.sigmoid(ga_ref[:, cs].astype(F32)) * m

    def chunk_main(rows):
        terms = [head_terms(rows, h) for h in hs]
        project_some()
        b_last = [t[2][c_len - 1:c_len, :] for t in terms]
        q_in = [t[0] * jnp.exp(t[2]) for t in terms]
        k_out = [t[1] * jnp.exp(bl - t[2]) for t, bl in zip(terms, b_last)]
        un_mid = [jnp.where(bl < -HG_SAFE_RANGE, 0.0, jnp.exp(-0.5 * bl)) for bl in b_last]
        q_mid = [(x * u).astype(BF16) for x, u in zip(q_in, un_mid)]
        k_mid = [(x * u).astype(BF16) for x, u in zip(k_out, un_mid)]
        sc = [lax.dot_general(qm, km, (((1,), (1,)), ((), ())), preferred_element_type=F32)
              for qm, km in zip(q_mid, k_mid)]
        a = [jnp.where(causal, x, 0.0).astype(BF16) for x in sc]
        project_some()
        vh = [i_ref[rows, vs] for vs in vsl]
        st = [s_ref[h] for h in hs]
        o = [jnp.dot(a[h], vh[h], preferred_element_type=F32)
             + lax.dot_general(q_in[h].astype(BF16), st[h].astype(BF16), (((1,), (1,)), ((), ())),
                               preferred_element_type=F32) for h in hs]
        for h in hs:
            s_ref[h] = st[h] * jnp.exp(b_last[h]) + lax.dot_general(
                vh[h], k_out[h].astype(BF16), (((0,), (0,)), ((), ())),
                preferred_element_type=F32)
        b_min = b_last[0]
        for h in hs:
            acc_ref[rows, vsl[h]] = o[h]
            write_out(rows, h, o[h])
            b_min = jnp.minimum(b_min, b_last[h])
        project_some()
        return b_min

    step_min = None
    for c in range(n_chunks):
        b_min = chunk_main(slice(c * c_len, (c + 1) * c_len))
        step_min = b_min if step_min is None else jnp.minimum(step_min, b_min)
    project_some(len(pending))

    @pl.when(jnp.min(step_min) < -HG_SAFE_RANGE)
    def _():
        def chunk_exact(c, carry):
            rows = pl.ds(pl.multiple_of(c * c_len, c_len), c_len)
            for h in hs:
                q, k, b = head_terms(rows, h)
                bx_ref[:, ksl[h]] = b
                qx_ref[:, ksl[h]] = jnp.where(b[c_len - 1:c_len, :] < -HG_SAFE_RANGE, q, 0.0)
                kx_ref[:, ksl[h]] = k
                vx_ref[:, vsl[h]] = i_ref[rows, vsl[h]].astype(F32)

            def pair(s, cc):
                bs = bx_ref[pl.ds(s, 1), :]
                d = jnp.where(t_idx >= s, jnp.exp(jnp.minimum(bx_ref[...] - bs, 0.0)), 0.0)
                p = qx_ref[...] * d * kx_ref[pl.ds(s, 1), :]
                v_s = vx_ref[pl.ds(s, 1), :]
                for h in hs:
                    acc_ref[rows, vsl[h]] += (jnp.sum(p[:, ksl[h]], axis=-1, keepdims=True)
                                              * v_s[:, vsl[h]])
                return cc
            lax.fori_loop(0, c_len, pair, 0)
            for h in hs:
                write_out(rows, h, acc_ref[rows, vsl[h]])
            return carry
        lax.fori_loop(0, n_chunks, chunk_exact, 0)


def _hgrn2_with_ret_out(proj, lb_logits, gn, yr, w_ret, layer, *, batch, tp, heads, off_q, off_f,
                        off_i, off_g, off_ga):
    mp = proj.shape[0]
    hk, hv = heads * HG_DK, heads * HG_DV
    kr, d = w_ret.shape[1], w_ret.shape[2]
    rb = _pick(tp, (FUSED_ROWS, 128, 64))
    nblk = tp // rb
    depth = lb_logits.shape[0]
    assert kr % STAGE_ROWS == 0 and off_ga % d == 0
    body = functools.partial(_hgrn_retout_body, layer=layer, n_chunks=rb // HG_CHUNK, heads=heads)
    slab = pltpu.VMEM((HG_CHUNK, hv), F32)
    row = lambda b, j: b * nblk + j
    vmem = (kr * d * 2 + 2 * STAGE_ROWS * d * 4 + rb * hv * 4 + 4 * HG_CHUNK * hv * 4
            + heads * HG_DV * HG_DK * 4
            + 2 * rb * (2 * hk * 2 + 2 * hv * 2 + kr * 2 + d * 2 + hv * 2 + d * 4) + 3 * rb * d * 4)
    return pl.pallas_call(
        body,
        grid=(batch, nblk),
        in_specs=[
            pl.BlockSpec((depth, hk), lambda b, j: (0, 0)),
            pl.BlockSpec((None, 1, hv), lambda b, j: (layer, 0, 0)),
            pl.BlockSpec((rb, hk), lambda b, j: (row(b, j), off_q // hk)),
            pl.BlockSpec((rb, hk), lambda b, j: (row(b, j), off_f // hk)),
            pl.BlockSpec((rb, hv), lambda b, j: (row(b, j), off_i // hv)),
            pl.BlockSpec((rb, hv), lambda b, j: (row(b, j), off_g // hv)),
            pl.BlockSpec((rb, kr), lambda b, j: (row(b, j), 0)),
            pl.BlockSpec((rb, d), lambda b, j: (row(b, j), off_ga // d)),
            pl.BlockSpec(memory_space=pl.ANY),
        ],
        out_specs=[pl.BlockSpec((rb, hv), lambda b, j: (row(b, j), 0)),
                   pl.BlockSpec((rb, d), lambda b, j: (row(b, j), 0))],
        out_shape=[jax.ShapeDtypeStruct((mp, hv), BF16), jax.ShapeDtypeStruct((mp, d), F32)],
        scratch_shapes=[pltpu.VMEM((heads, HG_DV, HG_DK), F32), pltpu.VMEM((rb, hv), F32),
                        slab, slab, slab, slab, pltpu.VMEM((kr, d), BF16),
                        pltpu.VMEM((2, STAGE_ROWS, d), F32), pltpu.SemaphoreType.DMA((2,))],
        compiler_params=_params(2, vmem // (1024 * 1024) + 4),
        name="hgrn2_ret_out_proj",
    )(lb_logits, gn, proj, proj, proj, proj, yr, proj, w_ret)


def _router_body(x_ref, w_ref, idx_ref, wts_ref, *, n_experts):
    x = x_ref[...]
    w = w_ref[...]
    xh = x.astype(BF16)
    xl = (x - xh.astype(F32)).astype(BF16)
    wh = w.astype(BF16)
    wl = (w - wh.astype(F32)).astype(BF16)
    logits = (jnp.dot(xh, wh, preferred_element_type=F32)
              + jnp.dot(xh, wl, preferred_element_type=F32)
              + jnp.dot(xl, wh, preferred_element_type=F32))
    lane = lax.broadcasted_iota(jnp.int32, logits.shape, 1)
    neg = jnp.float32(-jnp.inf)
    logits = jnp.where(lane < n_experts, logits, neg)
    v1 = jnp.max(logits, axis=-1, keepdims=True)
    i1 = jnp.min(jnp.where(logits == v1, lane, ROUTER_LANES), axis=-1, keepdims=True)
    rest = jnp.where(lane == i1, neg, logits)
    v2 = jnp.max(rest, axis=-1, keepdims=True)
    i2 = jnp.min(jnp.where(rest == v2, lane, ROUTER_LANES), axis=-1, keepdims=True)
    e2 = jnp.exp(v2 - v1)
    w1 = 1.0 / (1.0 + e2)
    w2 = e2 / (1.0 + e2)
    idx_ref[...] = jnp.where(lane == 0, i1, jnp.where(lane == 1, i2, 0))
    wts_ref[...] = jnp.where(lane == 0, w1, jnp.where(lane == 1, w2, 0.0))


def _router(h, w_router_padded, layer_idx, n_experts, tp):
    mp, d = h.shape
    rows = _pick(tp, (528, 384, 256, 128, 64, 32, 16))
    body = functools.partial(_router_body, n_experts=n_experts)
    out_spec = pl.BlockSpec((rows, ROUTER_LANES), lambda i: (i, 0))
    return pl.pallas_call(
        body,
        grid=(mp // rows,),
        in_specs=[pl.BlockSpec((rows, d), lambda i: (i, 0)),
                  pl.BlockSpec((None, d, ROUTER_LANES), lambda i: (layer_idx, 0, 0))],
        out_specs=[out_spec, out_spec],
        out_shape=[jax.ShapeDtypeStruct((mp, ROUTER_LANES), jnp.int32),
                   jax.ShapeDtypeStruct((mp, ROUTER_LANES), F32)],
        compiler_params=_params(1, 40),
        name="moe_router",
    )(h, w_router_padded)


MOE_TM = 512
GATHER_ROWS = 256
SUBLANES = 8


def _gather_body(src_ref, tv_ref, x_hbm, o_ref, buf, sem, *, rows, steps_per_tile):
    step = pl.program_id(0)

    def start_rows(st, slot):
        @pl.when(tv_ref[st // steps_per_tile] == 1)
        def _():
            def issue(g, c):
                for u in range(SUBLANES):
                    t = src_ref[st * rows + g * SUBLANES + u]
                    pltpu.make_async_copy(x_hbm.at[pl.ds(t, 1), :],
                                          buf.at[slot, g, pl.ds(u, 1), :],
                                          sem.at[slot]).start(priority=u % 2)
                return c
            lax.fori_loop(0, rows // SUBLANES, issue, 0)

    @pl.when(step == 0)
    def _():
        start_rows(step, 0)

    @pl.when(step + 1 < pl.num_programs(0))
    def _():
        start_rows(step + 1, lax.rem(step + 1, 2))

    slot = lax.rem(step, 2)
    valid = tv_ref[step // steps_per_tile] == 1

    @pl.when(valid)
    def _():
        pltpu.make_async_copy(buf.at[slot], buf.at[slot], sem.at[slot]).wait()
        o_ref[...] = buf[slot].reshape(rows, buf.shape[-1]).astype(o_ref.dtype)

    @pl.when(jnp.logical_not(valid))
    def _():
        o_ref[...] = jnp.zeros_like(o_ref)


def _gather_rows(h, src, tile_valid, *, n_rows):
    mp, d = h.shape
    rows = GATHER_ROWS
    assert n_rows % rows == 0 and MOE_TM % rows == 0
    body = functools.partial(_gather_body, rows=rows, steps_per_tile=MOE_TM // rows)
    return pl.pallas_call(
        body,
        grid_spec=pltpu.PrefetchScalarGridSpec(
            num_scalar_prefetch=2,
            grid=(n_rows // rows,),
            in_specs=[pl.BlockSpec(memory_space=pl.ANY)],
            out_specs=pl.BlockSpec((rows, d), lambda i, src, tv: (i, 0)),
            scratch_shapes=[pltpu.VMEM((2, rows // SUBLANES, SUBLANES, d), F32),
                            pltpu.SemaphoreType.DMA((2,))],
        ),
        out_shape=jax.ShapeDtypeStruct((n_rows, d), BF16),
        compiler_params=_params(1, 32),
        name="moe_gather",
    )(src, tile_valid, h)


def _gmm_body(te_ref, tv_ref, nx_ref, *refs, n_w, swiglu, wbase, bn):
    it = iter(refs)
    x_ref = next(it)
    w_hbm = [next(it) for _ in range(n_w)]
    o_ref = next(it)
    wb_refs = [next(it) for _ in range(n_w)]
    wf_refs = [next(it) for _ in range(n_w)]
    sem = next(it)
    j = pl.program_id(0)
    i = pl.program_id(1)
    fresh = jnp.logical_or(i == 0, te_ref[i] != te_ref[jnp.maximum(i - 1, 0)])

    def weight_copy(w, expert, col_tile):
        cols = pl.ds(pl.multiple_of(col_tile * bn, bn), bn)
        return pltpu.make_async_copy(w_hbm[w].at[wbase + expert, :, cols], wf_refs[w], sem.at[w])

    @pl.when(fresh)
    def _():
        @pl.when(jnp.logical_and(i == 0, j == 0))
        def _():
            for w in range(n_w):
                weight_copy(w, te_ref[0], 0).start()
        for w in range(n_w):
            weight_copy(w, te_ref[i], j).wait()
            wb_refs[w][...] = wf_refs[w][...].astype(BF16)
        nxt = nx_ref[i]

        @pl.when(nxt >= 0)
        def _():
            for w in range(n_w):
                weight_copy(w, nxt, j).start()

        @pl.when(jnp.logical_and(nxt < 0, j + 1 < pl.num_programs(0)))
        def _():
            for w in range(n_w):
                weight_copy(w, te_ref[0], j + 1).start()

    @pl.when(tv_ref[i] == 1)
    def _():
        x = x_ref[...]
        y = jnp.dot(x, wb_refs[0][...], preferred_element_type=F32)
        if swiglu:
            u = jnp.dot(x, wb_refs[1][...], preferred_element_type=F32)
            y = (y * jax.nn.sigmoid(y)) * u
        o_ref[...] = y.astype(o_ref.dtype)

    @pl.when(tv_ref[i] == 0)
    def _():
        o_ref[...] = jnp.zeros_like(o_ref)


def _grouped_matmul(x, ws, tile_expert, tile_valid, next_expert, wbase, *, bn, out_dtype, swiglu,
                    name):
    m, k = x.shape
    n = ws[0].shape[2]
    tm = MOE_TM
    assert m % tm == 0 and n % bn == 0
    in_specs = [pl.BlockSpec((tm, k), lambda j, i, te, tv, nx: (i, 0))]
    in_specs += [pl.BlockSpec(memory_space=pl.ANY) for _ in ws]
    out_bytes = jnp.dtype(out_dtype).itemsize
    vmem = (2 * tm * k * 2 + len(ws) * k * bn * (4 + 2) + 2 * tm * bn * out_bytes
            + 4 * tm * bn * 4)
    body = functools.partial(_gmm_body, n_w=len(ws), swiglu=swiglu, wbase=wbase, bn=bn)
    return pl.pallas_call(
        body,
        grid_spec=pltpu.PrefetchScalarGridSpec(
            num_scalar_prefetch=3,
            grid=(n // bn, m // tm),
            in_specs=in_specs,
            out_specs=pl.BlockSpec((tm, bn), lambda j, i, te, tv, nx: (i, j)),
            scratch_shapes=([pltpu.VMEM((k, bn), BF16) for _ in ws]
                            + [pltpu.VMEM((k, bn), F32) for _ in ws]
                            + [pltpu.SemaphoreType.DMA((len(ws),))]),
        ),
        out_shape=jax.ShapeDtypeStruct((m, n), out_dtype),
        compiler_params=_params(2, vmem // (1024 * 1024) + 4),
        name=name,
    )(tile_expert, tile_valid, next_expert, x, *ws)


def _combine_ln_body(pos_ref, h_ref, w_ref, g_ref, b_ref, y_hbm, o_ref, ob_ref, buf, sem,
                     *, alpha, rows, tp, pad):
    step = pl.program_id(0)

    def start_rows(st, slot):
        def issue(g, c):
            for u in range(SUBLANES):
                for s in range(TOP_K):
                    p = pos_ref[(st * rows + g * SUBLANES + u) * TOP_K + s]
                    pltpu.make_async_copy(y_hbm.at[pl.ds(p, 1), :],
                                          buf.at[slot, s, g, pl.ds(u, 1), :],
                                          sem.at[slot]).start(priority=s % 2)
            return c
        lax.fori_loop(0, rows // SUBLANES, issue, 0)

    @pl.when(step == 0)
    def _():
        start_rows(step, 0)

    @pl.when(step + 1 < pl.num_programs(0))
    def _():
        start_rows(step + 1, lax.rem(step + 1, 2))

    slot = lax.rem(step, 2)
    for s in range(TOP_K):
        pltpu.make_async_copy(buf.at[slot, s], buf.at[slot, s], sem.at[slot]).wait()

    w = w_ref[...]
    lane = lax.broadcasted_iota(jnp.int32, w.shape, 1)
    f = None
    for s in range(TOP_K):
        ws = jnp.sum(jnp.where(lane == s, w, 0.0), axis=-1, keepdims=True)
        term = ws * buf[slot, s].reshape(rows, buf.shape[-1])
        f = term if f is None else f + term
    y = _layernorm_rows(alpha * h_ref[...] + f, g_ref[...], b_ref[...], rows=rows, tp=tp, pad=pad)
    o_ref[...] = y
    ob_ref[...] = y.astype(BF16)


def _combine_ln(h, ys, pos, wts, g, b, layer, *, alpha, tp, pad):
    mp, d = h.shape
    rows = _pick(tp, (384, 256, 128, 64, 32, 16))
    body = functools.partial(_combine_ln_body, alpha=alpha, rows=rows, tp=tp, pad=pad)
    row_spec = pl.BlockSpec((rows, d), lambda i, p: (i, 0))
    par_spec = pl.BlockSpec((None, 1, d), lambda i, p: (layer, 0, 0))
    return pl.pallas_call(
        body,
        grid_spec=pltpu.PrefetchScalarGridSpec(
            num_scalar_prefetch=1,
            grid=(mp // rows,),
            in_specs=[row_spec, pl.BlockSpec((rows, ROUTER_LANES), lambda i, p: (i, 0)),
                      par_spec, par_spec, pl.BlockSpec(memory_space=pl.ANY)],
            out_specs=[row_spec, row_spec],
            scratch_shapes=[pltpu.VMEM((2, TOP_K, rows // SUBLANES, SUBLANES, d), F32),
                            pltpu.SemaphoreType.DMA((2,))],
        ),
        out_shape=[jax.ShapeDtypeStruct((mp, d), F32), jax.ShapeDtypeStruct((mp, d), BF16)],
        compiler_params=_params(1, 40),
        name="moe_combine_layernorm",
    )(pos, h, wts, g, b, ys)


def _route(idx, *, n_experts, tp, pad):
    mp = idx.shape[0]
    tm = MOE_TM
    n_assign = mp * TOP_K
    n_tiles = -(-n_assign // tm) + n_experts
    n_rows = n_tiles * tm
    token = jnp.arange(mp, dtype=jnp.int32)
    real = (token % tp) >= pad
    e = jnp.where(real[:, None], idx[:, :TOP_K], n_experts).reshape(n_assign)
    onehot = (e[:, None] == jnp.arange(n_experts, dtype=jnp.int32)[None, :]).astype(jnp.int32)
    csum = jnp.cumsum(onehot, axis=0)
    counts = csum[-1]
    rank = jnp.sum(onehot * csum, axis=1) - 1
    gsz = ((counts + tm - 1) // tm) * tm
    gend = jnp.cumsum(gsz)
    gstart = gend - gsz
    routed = e < n_experts
    dst = jnp.where(routed, gstart[jnp.minimum(e, n_experts - 1)] + rank, n_rows)
    assign_token = jnp.arange(n_assign, dtype=jnp.int32) // TOP_K
    src = jnp.zeros((n_rows,), jnp.int32).at[dst].set(assign_token, mode="drop")
    pos = jnp.where(routed, dst, 0).astype(jnp.int32)
    tile_start = jnp.arange(n_tiles, dtype=jnp.int32) * tm
    tile_valid = tile_start < gend[-1]
    tile_expert = jnp.sum((tile_start[:, None] >= gend[None, :]).astype(jnp.int32), axis=1)
    last_expert = jnp.max(jnp.where(tile_valid, tile_expert, 0))
    tile_expert = jnp.where(tile_valid, tile_expert, last_expert)
    later = jnp.arange(n_tiles)[None, :] > jnp.arange(n_tiles)[:, None]
    differs = jnp.logical_and(later, tile_expert[None, :] != tile_expert[:, None])
    first = jnp.argmax(differs, axis=1)
    next_expert = jnp.where(jnp.any(differs, axis=1), tile_expert[first], -1).astype(jnp.int32)
    return src, pos, tile_expert, tile_valid.astype(jnp.int32), next_expert, n_rows


def kernel(x, meta_tokens, w_in, ret_gn_g, hg_norm_g, hg_lb_logits, w_ret_out, w_hg_out, w_o,
           ln1_g, ln1_b, ln2_g, ln2_b, ffn_w_gate, ffn_w_up, ffn_w_down,
           moe_router, moe_w_gate, moe_w_up, moe_w_down):
    batch, seq, d = x.shape
    n_meta = meta_tokens.shape[0]
    depth, _, p_in = w_in.shape
    ret_v = w_ret_out.shape[1]
    hg_v = w_hg_out.shape[1]
    hg_k = hg_lb_logits.shape[1]
    ret_qk = (p_in - 2 * ret_v - 2 * hg_k - 2 * hg_v - 2 * d) // 2
    ret_heads = ret_qk // RET_DK
    hg_heads = hg_k // HG_DK
    n_experts = moe_router.shape[2]
    assert ret_v == ret_heads * RET_DV and hg_v == hg_heads * HG_DV and hg_k == hg_v
    assert n_experts <= ROUTER_LANES

    pad = RET_CHUNK - n_meta
    tp = pad + n_meta + seq
    assert tp % RET_CHUNK == 0 and tp % HG_CHUNK == 0
    mp = batch * tp
    alpha = (2 * depth) ** 0.25

    off = {}
    acc = 0
    for nm, wd in (("rq", ret_qk), ("rk", ret_qk), ("rv", ret_v), ("rg", ret_v), ("hq", hg_k),
                   ("hf", hg_k), ("hi", hg_v), ("hg", hg_v), ("ga", d), ("gb", d)):
        off[nm] = acc
        acc += wd

    meta = jnp.broadcast_to(meta_tokens[None].astype(F32), (batch, n_meta, d))
    h = jnp.concatenate([jnp.zeros((batch, pad, d), F32), meta, x.astype(F32)], axis=1)
    h = h.reshape(mp, d)
    hb = h.astype(BF16)

    pos = jnp.arange(tp, dtype=F32) - float(pad)
    inv = 1.0 / (ROPE_BASE ** jnp.linspace(0.0, 1.0, RET_DK // 2, dtype=F32))
    ang = pos[:, None] * inv[None, :]
    cos, sin = jnp.cos(ang), jnp.sin(ang)
    log_gamma = jnp.log1p(-jnp.exp2(-5.0 - jnp.arange(ret_heads, dtype=F32)))

    ret_gn3 = ret_gn_g.astype(F32).reshape(depth, 1, ret_v)
    hg_gn3 = hg_norm_g.astype(F32).reshape(depth, 1, hg_v)
    ln1_g3, ln1_b3 = ln1_g.reshape(depth, 1, d), ln1_b.reshape(depth, 1, d)
    ln2_g3, ln2_b3 = ln2_g.reshape(depth, 1, d), ln2_b.reshape(depth, 1, d)
    n_moe = moe_router.shape[0]
    d_ffe = moe_w_gate.shape[3]
    router_w = jnp.pad(moe_router.astype(F32), ((0, 0), (0, 0), (0, ROUTER_LANES - n_experts)))
    moe_wg = moe_w_gate.reshape(n_moe * n_experts, d, d_ffe)
    moe_wu = moe_w_up.reshape(n_moe * n_experts, d, d_ffe)
    moe_wd = moe_w_down.reshape(n_moe * n_experts, d_ffe, d)

    tile = lambda n, cands: _pick(n, cands)

    for l in range(depth):
        proj = _matmul_rows_resident(hb, [w_in], l, rows=tp, skip=pad,
                                     bn=tile(p_in, (512, 256, 128)), out_dtype=BF16,
                                     name="in_proj")
        yr = _retention(proj, cos, sin, log_gamma, ret_gn3, l, batch=batch, tp=tp,
                        heads=ret_heads, off_q=off["rq"], off_k=off["rk"], off_v=off["rv"],
                        off_g=off["rg"])
        yh, t_r = _hgrn2_with_ret_out(
            proj, hg_lb_logits.astype(F32), hg_gn3, yr, w_ret_out, l, batch=batch, tp=tp,
            heads=hg_heads, off_q=off["hq"], off_f=off["hf"], off_i=off["hi"], off_g=off["hg"],
            off_ga=off["ga"])
        merged = _matmul_gate_add(yh, w_hg_out, l, (proj, off["gb"]), t_r, tp=tp,
                                  name="hg_out_proj")
        h, hb = _matmul_add_ln(merged, w_o, l, h, ln1_g3, ln1_b3, l, alpha=alpha, tp=tp, pad=pad,
                               staged=False, name="mix_out_proj_layernorm")
        if l % 2 == 0:
            d_ff = ffn_w_gate.shape[2]
            act = _matmul_rows_resident(hb, [ffn_w_gate, ffn_w_up], l // 2, rows=tp, skip=pad,
                                        bn=tile(d_ff, (256, 128)), out_dtype=BF16, name="ffn_up")
            h, hb = _matmul_add_ln(act, ffn_w_down, l // 2, h, ln2_g3, ln2_b3, l, alpha=alpha,
                                   tp=tp, pad=pad, staged=True, name="ffn_down_layernorm")
        else:
            idx, wts = _router(h, router_w, l // 2, n_experts, tp)
            src, pos, tile_expert, tile_valid, next_expert, n_rows = _route(
                idx, n_experts=n_experts, tp=tp, pad=pad)
            xs = _gather_rows(h, src, tile_valid, n_rows=n_rows)
            wbase = (l // 2) * n_experts
            act = _grouped_matmul(xs, [moe_wg, moe_wu], tile_expert, tile_valid, next_expert, wbase,
                                  bn=tile(d_ffe, (1408, 256, 128)), out_dtype=BF16, swiglu=True,
                                  name="moe_up")
            ys = _grouped_matmul(act, [moe_wd], tile_expert, tile_valid, next_expert, wbase,
                                 bn=tile(d, (1024, 512, 256, 128)), out_dtype=F32, swiglu=False,
                                 name="moe_down")
            h, hb = _combine_ln(h, ys, pos, wts, ln2_g3, ln2_b3, l, alpha=alpha, tp=tp, pad=pad)

    out = h.reshape(batch, tp, d)[:, pad + n_meta:, :]
    return out.astype(x.dtype)
```

```python
import functools

import jax
import jax.numpy as jnp
from jax import lax
from jax.experimental import pallas as pl
from jax.experimental.pallas import tpu as pltpu

F32 = jnp.float32
BF16 = jnp.bfloat16

RET_DK = 256
RET_DV = 512
RET_CHUNK = 128
HG_DK = 128
HG_DV = 128
HG_CHUNK = 64
TOP_K = 2
ROPE_BASE = 10000.0
LN_EPS = 1e-5
ROUTER_LANES = 128
VMEM_LIMIT_CAP_MB = 56


def _pick(n, candidates):
    for c in candidates:
        if n % c == 0:
            return c
    raise ValueError(f"no tile in {candidates} divides {n}")


def _params(n_axes, vmem_mb):
    return pltpu.CompilerParams(
        dimension_semantics=("arbitrary",) * n_axes,
        vmem_limit_bytes=min(vmem_mb, VMEM_LIMIT_CAP_MB) * 1024 * 1024)


def _mm_xres_body(*refs, n_w, sub, skip):
    x_ref, w_refs, o_ref = refs[0], refs[1:1 + n_w], refs[1 + n_w]
    wbs = [w_ref[...].astype(BF16) for w_ref in w_refs]
    rows = x_ref.shape[0]
    o_ref[0:skip, :] = jnp.zeros((skip, o_ref.shape[1]), o_ref.dtype)
    for r0 in range(skip, rows, sub):
        rs = slice(r0, min(r0 + sub, rows))
        y = jnp.dot(x_ref[rs, :], wbs[0], preferred_element_type=F32)
        if n_w == 2:
            u = jnp.dot(x_ref[rs, :], wbs[1], preferred_element_type=F32)
            y = (y * jax.nn.sigmoid(y)) * u
        o_ref[rs, :] = y.astype(o_ref.dtype)


def _matmul_rows_resident(x, ws, widx, *, rows, skip, bn, out_dtype, name):
    m, k = x.shape
    n = ws[0].shape[2]
    assert m % rows == 0 and n % bn == 0 and skip % 16 == 0
    sub = _pick(rows, (1056, 768, 512, 384, 256, 128, 64, 32, 16))
    out_bytes = jnp.dtype(out_dtype).itemsize
    vmem = (rows * k * 2 + len(ws) * k * bn * (2 * 4 + 2) + 2 * rows * bn * out_bytes
            + (2 + len(ws)) * sub * bn * 4)
    return pl.pallas_call(
        functools.partial(_mm_xres_body, n_w=len(ws), sub=sub, skip=skip),
        grid=(m // rows, n // bn),
        in_specs=([pl.BlockSpec((rows, k), lambda b, j: (b, 0), pipeline_mode=pl.Buffered(1))]
                  + [pl.BlockSpec((None, k, bn), lambda b, j: (widx, 0, j)) for _ in ws]),
        out_specs=pl.BlockSpec((rows, bn), lambda b, j: (b, j)),
        out_shape=jax.ShapeDtypeStruct((m, n), out_dtype),
        compiler_params=_params(2, vmem // (1024 * 1024) + 6),
        name=name,
    )(x, *ws)


def _layernorm_rows(z, g, b, *, rows, tp, pad, row0=0):
    mu = jnp.mean(z, axis=-1, keepdims=True)
    zc = z - mu
    var = jnp.mean(zc * zc, axis=-1, keepdims=True)
    y = zc * lax.rsqrt(var + LN_EPS) * g + b
    blk = lax.rem(pl.program_id(0), tp // rows)
    r = blk * rows + row0 + lax.broadcasted_iota(jnp.int32, (z.shape[0], 1), 0)
    return jnp.where(r >= pad, y, 0.0)


RESIDENT_ROWS = 384
RESIDENT_SUB = 128


STAGE_ROWS = 512


def _stage_weight(w_hbm, layer, wb_ref, wf_ref, sem):
    n_stage = wb_ref.shape[0] // STAGE_ROWS

    def stage_copy(c, slot):
        return pltpu.make_async_copy(
            w_hbm.at[layer, pl.ds(c * STAGE_ROWS, STAGE_ROWS), :], wf_ref.at[slot], sem.at[slot])
    stage_copy(0, 0).start()
    for c in range(n_stage):
        if c + 1 < n_stage:
            stage_copy(c + 1, (c + 1) % 2).start()
        stage_copy(c, c % 2).wait()
        wb_ref[c * STAGE_ROWS:(c + 1) * STAGE_ROWS, :] = wf_ref[c % 2].astype(BF16)


def _mm_ln_body(x_ref, w_ref, h_ref, g_ref, b_ref, o_ref, ob_ref, wb_ref, *stage, layer, alpha,
                rows, sub, tp, pad):
    @pl.when(pl.program_id(0) == 0)
    def _():
        if stage:
            _stage_weight(w_ref, layer, wb_ref, *stage)
        else:
            wb_ref[...] = w_ref[...].astype(BF16)

    for r0 in range(0, rows, sub):
        rs = slice(r0, r0 + sub)
        m = jnp.dot(x_ref[rs, :], wb_ref[...], preferred_element_type=F32)
        y = _layernorm_rows(alpha * h_ref[rs, :] + m, g_ref[...], b_ref[...], rows=rows, tp=tp,
                            pad=pad, row0=r0)
        o_ref[rs, :] = y
        ob_ref[rs, :] = y.astype(BF16)


def _mm_gate_add_body(x_ref, w_ref, gate_ref, add_ref, o_ref, wb_ref, *, rows, sub):
    @pl.when(pl.program_id(0) == 0)
    def _():
        wb_ref[...] = w_ref[...].astype(BF16)

    for r0 in range(0, rows, sub):
        rs = slice(r0, r0 + sub)
        m = jnp.dot(x_ref[rs, :], wb_ref[...], preferred_element_type=F32)
        y = add_ref[rs, :] + jax.nn.sigmoid(gate_ref[rs, :].astype(F32)) * m
        o_ref[rs, :] = y.astype(o_ref.dtype)


def _resident_specs(x, w, layer, tp):
    mp, k = x.shape
    d = w.shape[2]
    rows = _pick(tp, (RESIDENT_ROWS, 256, 128, 64, 32, 16))
    sub = _pick(rows, (RESIDENT_SUB, 64, 32, 16))
    x_spec = pl.BlockSpec((rows, k), lambda i: (i, 0))
    w_spec = pl.BlockSpec((None, k, d), lambda i: (layer, 0, 0), pipeline_mode=pl.Buffered(1))
    return mp, k, d, rows, sub, x_spec, w_spec


def _matmul_add_ln(x, w, w_layer, h, g, b, ln_layer, *, alpha, tp, pad, staged, name):
    mp, k, d, rows, sub, x_spec, w_spec = _resident_specs(x, w, w_layer, tp)
    scratch = [pltpu.VMEM((k, d), BF16)]
    w_bytes = k * d * (4 + 2)
    if staged:
        assert k % STAGE_ROWS == 0
        rows = _pick(tp, (FUSED_ROWS, 128, 64, 32, 16))
        sub = _pick(rows, (RESIDENT_SUB, rows))
        x_spec = pl.BlockSpec((rows, k), lambda i: (i, 0))
        w_spec = pl.BlockSpec(memory_space=pl.ANY)
        scratch += [pltpu.VMEM((2, STAGE_ROWS, d), F32), pltpu.SemaphoreType.DMA((2,))]
        w_bytes = k * d * 2 + 2 * STAGE_ROWS * d * 4
    body = functools.partial(_mm_ln_body, layer=w_layer, alpha=alpha, rows=rows, sub=sub, tp=tp,
                             pad=pad)
    row_spec = pl.BlockSpec((rows, d), lambda i: (i, 0))
    par_spec = pl.BlockSpec((None, 1, d), lambda i: (ln_layer, 0, 0))
    vmem = w_bytes + 2 * rows * k * 2 + rows * d * (2 * 4 + 2 * 4 + 2 * 2 + 3 * 4)
    return pl.pallas_call(
        body,
        grid=(mp // rows,),
        in_specs=[x_spec, w_spec, row_spec, par_spec, par_spec],
        out_specs=[row_spec, row_spec],
        out_shape=[jax.ShapeDtypeStruct((mp, d), F32), jax.ShapeDtypeStruct((mp, d), BF16)],
        scratch_shapes=scratch,
        compiler_params=_params(1, vmem // (1024 * 1024) + 4),
        name=name,
    )(x, w, h, g, b)


def _matmul_gate_add(x, w, layer, gate, add, *, tp, name):
    mp, k, d, rows, sub, x_spec, w_spec = _resident_specs(x, w, layer, tp)
    garr, goff = gate
    assert goff % d == 0
    gblk = goff // d
    body = functools.partial(_mm_gate_add_body, rows=rows, sub=sub)
    row_spec = pl.BlockSpec((rows, d), lambda i: (i, 0))
    vmem = k * d * (4 + 2) + 2 * rows * k * 2 + rows * d * (2 * 2 + 2 * 4 + 2 * 2 + 3 * 4)
    return pl.pallas_call(
        body,
        grid=(mp // rows,),
        in_specs=[x_spec, w_spec, pl.BlockSpec((rows, d), lambda i: (i, gblk)), row_spec],
        out_specs=row_spec,
        out_shape=jax.ShapeDtypeStruct((mp, d), BF16),
        scratch_shapes=[pltpu.VMEM((k, d), BF16)],
        compiler_params=_params(1, vmem // (1024 * 1024) + 4),
        name=name,
    )(x, w, garr, add)


def _ret_body(lg_ref, q_ref, k_ref, v_ref, g_ref, cos_ref, sin_ref, gn_ref, o_ref, r_ref,
              *, n_chunks, hpg):
    c_len = RET_CHUNK
    half = RET_DK // 2
    hs = range(hpg)
    ksl = [slice(i * RET_DK, (i + 1) * RET_DK) for i in hs]
    vsl = [slice(i * RET_DV, (i + 1) * RET_DV) for i in hs]

    @pl.when(pl.program_id(2) == 0)
    def _():
        r_ref[...] = jnp.zeros_like(r_ref)

    t_col = lax.broadcasted_iota(jnp.int32, (c_len, 1), 0).astype(F32)
    rel = (lax.broadcasted_iota(jnp.int32, (c_len, c_len), 0)
           - lax.broadcasted_iota(jnp.int32, (c_len, c_len), 1))
    rel_f = jnp.maximum(rel, 0).astype(F32)
    lg = [lg_ref[pl.program_id(1) * hpg + i] for i in hs]
    intra = [jnp.where(rel >= 0, jnp.exp(rel_f * x), 0.0) * (RET_DK ** -0.5) for x in lg]
    q_decay = [jnp.exp((t_col + 1.0) * x) * (RET_DK ** -0.5) for x in lg]
    k_decay = [jnp.exp((c_len - 1.0 - t_col) * x) for x in lg]
    c_decay = [jnp.exp(jnp.full((1, 1), float(c_len), F32) * x) for x in lg]
    gn = gn_ref[...]

    def rot(x, cos, sin):
        x1, x2 = x[:, :half], x[:, half:]
        return jnp.concatenate([x1 * cos - x2 * sin, x1 * sin + x2 * cos], axis=-1)

    def chunk(c, carry):
        rows = pl.ds(pl.multiple_of(c * c_len, c_len), c_len)
        cos = cos_ref[rows, :]
        sin = sin_ref[rows, :]
        q = [rot(q_ref[rows, ksl[i]].astype(F32), cos, sin) for i in hs]
        k = [rot(k_ref[rows, ksl[i]].astype(F32), cos, sin) for i in hs]
        v = [v_ref[rows, vsl[i]] for i in hs]
        s = [lax.dot_general(q[i].astype(BF16), k[i].astype(BF16),
                             (((1,), (1,)), ((), ())), preferred_element_type=F32) * intra[i]
             for i in hs]
        r_old = [r_ref[i] for i in hs]
        o = [jnp.dot(s[i].astype(BF16), v[i], preferred_element_type=F32)
             + jnp.dot((q[i] * q_decay[i]).astype(BF16), r_old[i].astype(BF16),
                       preferred_element_type=F32) for i in hs]
        for i in hs:
            r_ref[i] = r_old[i] * c_decay[i] + lax.dot_general(
                (k[i] * k_decay[i]).astype(BF16), v[i], (((0,), (0,)), ((), ())),
                preferred_element_type=F32)
        for i in hs:
            mu = jnp.mean(o[i], axis=-1, keepdims=True)
            oc = o[i] - mu
            var = jnp.mean(oc * oc, axis=-1, keepdims=True)
            gate = g_ref[rows, vsl[i]].astype(F32)
            y = oc * lax.rsqrt(var + LN_EPS) * gn[:, vsl[i]] * (gate * jax.nn.sigmoid(gate))
            o_ref[rows, vsl[i]] = y.astype(o_ref.dtype)
        return carry

    lax.fori_loop(0, n_chunks, chunk, 0)


def _retention(proj, cos, sin, log_gamma, gn, layer, *, batch, tp, heads, off_q, off_k, off_v,
               off_g):
    mp = proj.shape[0]
    n_chunks_total = tp // RET_CHUNK
    hpg = _pick(heads, (4, 2, 1))
    cpb = _pick(n_chunks_total, (3, 2, 1))
    tb = cpb * RET_CHUNK
    nblk = tp // tb
    gk, gv = hpg * RET_DK, hpg * RET_DV
    assert off_q % gk == 0 and off_k % gk == 0 and off_v % gv == 0 and off_g % gv == 0
    qb, kb, vb, gb = off_q // gk, off_k // gk, off_v // gv, off_g // gv
    row = lambda b, h, j: b * nblk + j
    body = functools.partial(_ret_body, n_chunks=cpb, hpg=hpg)
    return pl.pallas_call(
        body,
        grid=(batch, heads // hpg, nblk),
        in_specs=[
            pl.BlockSpec(memory_space=pltpu.SMEM),
            pl.BlockSpec((tb, gk), lambda b, h, j: (row(b, h, j), qb + h)),
            pl.BlockSpec((tb, gk), lambda b, h, j: (row(b, h, j), kb + h)),
            pl.BlockSpec((tb, gv), lambda b, h, j: (row(b, h, j), vb + h)),
            pl.BlockSpec((tb, gv), lambda b, h, j: (row(b, h, j), gb + h)),
            pl.BlockSpec((tb, RET_DK // 2), lambda b, h, j: (j, 0)),
            pl.BlockSpec((tb, RET_DK // 2), lambda b, h, j: (j, 0)),
            pl.BlockSpec((None, 1, gv), lambda b, h, j: (layer, 0, h)),
        ],
        out_specs=pl.BlockSpec((tb, gv), lambda b, h, j: (row(b, h, j), h)),
        out_shape=jax.ShapeDtypeStruct((mp, heads * RET_DV), BF16),
        scratch_shapes=[pltpu.VMEM((hpg, RET_DK, RET_DV), F32)],
        compiler_params=_params(3, 40),
        name="retention",
    )(log_gamma, proj, proj, proj, proj, cos, sin, gn)


HG_SAFE_RANGE = 85.0


FUSED_ROWS = 192


def _hgrn_retout_body(lbl_ref, gn_ref, q_ref, f_ref, i_ref, g_ref, yr_ref, ga_ref, w_hbm,
                      o_ref, t_ref, s_ref, acc_ref, bx_ref, qx_ref, kx_ref, vx_ref, wb_ref,
                      wf_ref, sem, *, layer, n_chunks, heads):
    c_len = HG_CHUNK
    first = jnp.logical_and(pl.program_id(0) == 0, pl.program_id(1) == 0)

    @pl.when(first)
    def _():
        _stage_weight(w_hbm, layer, wb_ref, wf_ref, sem)

    @pl.when(pl.program_id(1) == 0)
    def _():
        s_ref[...] = jnp.zeros_like(s_ref)

    logits = lbl_ref[...]
    e = jnp.exp(logits - jnp.max(logits, axis=0, keepdims=True))
    sm = e / jnp.sum(e, axis=0, keepdims=True)
    lb = jnp.zeros_like(sm[0:1, :])
    for j in range(1, layer + 1):
        lb = lb + sm[j:j + 1, :]
    gn = gn_ref[...]
    tri = (lax.broadcasted_iota(jnp.int32, (c_len, c_len), 0)
           >= lax.broadcasted_iota(jnp.int32, (c_len, c_len), 1)).astype(BF16)
    t_idx = lax.broadcasted_iota(jnp.int32, (c_len, 1), 0)
    causal = (lax.broadcasted_iota(jnp.int32, (c_len, c_len), 0)
              >= lax.broadcasted_iota(jnp.int32, (c_len, c_len), 1))
    hs = range(heads)
    ksl = [slice(h * HG_DK, (h + 1) * HG_DK) for h in hs]
    vsl = [slice(h * HG_DV, (h + 1) * HG_DV) for h in hs]

    def head_terms(rows, h):
        hq = q_ref[rows, ksl[h]].astype(F32)
        q = hq * jax.nn.sigmoid(hq)
        f = lb[:, ksl[h]] + (1.0 - lb[:, ksl[h]]) * jax.nn.sigmoid(f_ref[rows, ksl[h]].astype(F32))
        logf = jnp.log(f)
        p0 = logf.astype(BF16)
        r1 = logf - p0.astype(F32)
        p1 = r1.astype(BF16)
        p2 = (r1 - p1.astype(F32)).astype(BF16)
        b = (jnp.dot(tri, p0, preferred_element_type=F32)
             + jnp.dot(tri, p1, preferred_element_type=F32)
             + jnp.dot(tri, p2, preferred_element_type=F32))
        return q, 1.0 - f, b

    def write_out(rows, h, o):
        gate = g_ref[rows, vsl[h]].astype(F32)
        gate = gate * jax.nn.sigmoid(gate)
        y = o * lax.rsqrt(jnp.mean(o * o, axis=-1, keepdims=True) + LN_EPS)
        o_ref[rows, vsl[h]] = (y * gn[:, vsl[h]] * gate).astype(o_ref.dtype)

    n_rows = n_chunks * c_len
    col_tile = 256
    pending = [slice(c0, c0 + col_tile) for c0 in range(0, t_ref.shape[1], col_tile)]

    def project_some(n=1):
        for _ in range(n):
            if pending:
                cs = pending.pop(0)
                m = jnp.dot(yr_ref[...], wb_ref[:, cs], preferred_element_type=F32)
                t_ref[:, cs] = jax.nn.sigmoid(ga_ref[:, cs].astype(F32)) * m

    def chunk_main(rows):
        terms = [head_terms(rows, h) for h in hs]
        project_some()
        b_last = [t[2][c_len - 1:c_len, :] for t in terms]
        q_in = [t[0] * jnp.exp(t[2]) for t in terms]
        k_out = [t[1] * jnp.exp(bl - t[2]) for t, bl in zip(terms, b_last)]
        un_mid = [jnp.where(bl < -HG_SAFE_RANGE, 0.0, jnp.exp(-0.5 * bl)) for bl in b_last]
        q_mid = [(x * u).astype(BF16) for x, u in zip(q_in, un_mid)]
        k_mid = [(x * u).astype(BF16) for x, u in zip(k_out, un_mid)]
        sc = [lax.dot_general(qm, km, (((1,), (1,)), ((), ())), preferred_element_type=F32)
              for qm, km in zip(q_mid, k_mid)]
        a = [jnp.where(causal, x, 0.0).astype(BF16) for x in sc]
        project_some()
        vh = [i_ref[rows, vs] for vs in vsl]
        st = [s_ref[h] for h in hs]
        o = [jnp.dot(a[h], vh[h], preferred_element_type=F32)
             + lax.dot_general(q_in[h].astype(BF16), st[h].astype(BF16), (((1,), (1,)), ((), ())),
                               preferred_element_type=F32) for h in hs]
        for h in hs:
            s_ref[h] = st[h] * jnp.exp(b_last[h]) + lax.dot_general(
                vh[h], k_out[h].astype(BF16), (((0,), (0,)), ((), ())),
                preferred_element_type=F32)
        b_min = b_last[0]
        for h in hs:
            acc_ref[rows, vsl[h]] = o[h]
            write_out(rows, h, o[h])
            b_min = jnp.minimum(b_min, b_last[h])
        project_some()
        return b_min

    step_min = None
    for c in range(n_chunks):
        b_min = chunk_main(slice(c * c_len, (c + 1) * c_len))
        step_min = b_min if step_min is None else jnp.minimum(step_min, b_min)
    project_some(len(pending))

    @pl.when(jnp.min(step_min) < -HG_SAFE_RANGE)
    def _():
        def chunk_exact(c, carry):
            rows = pl.ds(pl.multiple_of(c * c_len, c_len), c_len)
            for h in hs:
                q, k, b = head_terms(rows, h)
                bx_ref[:, ksl[h]] = b
                qx_ref[:, ksl[h]] = jnp.where(b[c_len - 1:c_len, :] < -HG_SAFE_RANGE, q, 0.0)
                kx_ref[:, ksl[h]] = k
                vx_ref[:, vsl[h]] = i_ref[rows, vsl[h]].astype(F32)

            def pair(s, cc):
                bs = bx_ref[pl.ds(s, 1), :]
                d = jnp.where(t_idx >= s, jnp.exp(jnp.minimum(bx_ref[...] - bs, 0.0)), 0.0)
                p = qx_ref[...] * d * kx_ref[pl.ds(s, 1), :]
                v_s = vx_ref[pl.ds(s, 1), :]
                for h in hs:
                    acc_ref[rows, vsl[h]] += (jnp.sum(p[:, ksl[h]], axis=-1, keepdims=True)
                                              * v_s[:, vsl[h]])
                return cc
            lax.fori_loop(0, c_len, pair, 0)
            for h in hs:
                write_out(rows, h, acc_ref[rows, vsl[h]])
            return carry
        lax.fori_loop(0, n_chunks, chunk_exact, 0)


def _hgrn2_with_ret_out(proj, lb_logits, gn, yr, w_ret, layer, *, batch, tp, heads, off_q, off_f,
                        off_i, off_g, off_ga):
    mp = proj.shape[0]
    hk, hv = heads * HG_DK, heads * HG_DV
    kr, d = w_ret.shape[1], w_ret.shape[2]
    rb = _pick(tp, (FUSED_ROWS, 128, 64))
    nblk = tp // rb
    depth = lb_logits.shape[0]
    assert kr % STAGE_ROWS == 0 and off_ga % d == 0
    body = functools.partial(_hgrn_retout_body, layer=layer, n_chunks=rb // HG_CHUNK, heads=heads)
    slab = pltpu.VMEM((HG_CHUNK, hv), F32)
    row = lambda b, j: b * nblk + j
    vmem = (kr * d * 2 + 2 * STAGE_ROWS * d * 4 + rb * hv * 4 + 4 * HG_CHUNK * hv * 4
            + heads * HG_DV * HG_DK * 4
            + 2 * rb * (2 * hk * 2 + 2 * hv * 2 + kr * 2 + d * 2 + hv * 2 + d * 4) + 3 * rb * d * 4)
    return pl.pallas_call(
        body,
        grid=(batch, nblk),
        in_specs=[
            pl.BlockSpec((depth, hk), lambda b, j: (0, 0)),
            pl.BlockSpec((None, 1, hv), lambda b, j: (layer, 0, 0)),
            pl.BlockSpec((rb, hk), lambda b, j: (row(b, j), off_q // hk)),
            pl.BlockSpec((rb, hk), lambda b, j: (row(b, j), off_f // hk)),
            pl.BlockSpec((rb, hv), lambda b, j: (row(b, j), off_i // hv)),
            pl.BlockSpec((rb, hv), lambda b, j: (row(b, j), off_g // hv)),
            pl.BlockSpec((rb, kr), lambda b, j: (row(b, j), 0)),
            pl.BlockSpec((rb, d), lambda b, j: (row(b, j), off_ga // d)),
            pl.BlockSpec(memory_space=pl.ANY),
        ],
        out_specs=[pl.BlockSpec((rb, hv), lambda b, j: (row(b, j), 0)),
                   pl.BlockSpec((rb, d), lambda b, j: (row(b, j), 0))],
        out_shape=[jax.ShapeDtypeStruct((mp, hv), BF16), jax.ShapeDtypeStruct((mp, d), F32)],
        scratch_shapes=[pltpu.VMEM((heads, HG_DV, HG_DK), F32), pltpu.VMEM((rb, hv), F32),
                        slab, slab, slab, slab, pltpu.VMEM((kr, d), BF16),
                        pltpu.VMEM((2, STAGE_ROWS, d), F32), pltpu.SemaphoreType.DMA((2,))],
        compiler_params=_params(2, vmem // (1024 * 1024) + 4),
        name="hgrn2_ret_out_proj",
    )(lb_logits, gn, proj, proj, proj, proj, yr, proj, w_ret)


def _router_body(x_ref, w_ref, idx_ref, wts_ref, *, n_experts):
    x = x_ref[...]
    w = w_ref[...]
    xh = x.astype(BF16)
    xl = (x - xh.astype(F32)).astype(BF16)
    wh = w.astype(BF16)
    wl = (w - wh.astype(F32)).astype(BF16)
    logits = (jnp.dot(xh, wh, preferred_element_type=F32)
              + jnp.dot(xh, wl, preferred_element_type=F32)
              + jnp.dot(xl, wh, preferred_element_type=F32))
    lane = lax.broadcasted_iota(jnp.int32, logits.shape, 1)
    neg = jnp.float32(-jnp.inf)
    logits = jnp.where(lane < n_experts, logits, neg)
    v1 = jnp.max(logits, axis=-1, keepdims=True)
    i1 = jnp.min(jnp.where(logits == v1, lane, ROUTER_LANES), axis=-1, keepdims=True)
    rest = jnp.where(lane == i1, neg, logits)
    v2 = jnp.max(rest, axis=-1, keepdims=True)
    i2 = jnp.min(jnp.where(rest == v2, lane, ROUTER_LANES), axis=-1, keepdims=True)
    e2 = jnp.exp(v2 - v1)
    w1 = 1.0 / (1.0 + e2)
    w2 = e2 / (1.0 + e2)
    idx_ref[...] = jnp.where(lane == 0, i1, jnp.where(lane == 1, i2, 0))
    wts_ref[...] = jnp.where(lane == 0, w1, jnp.where(lane == 1, w2, 0.0))


def _router(h, w_router_padded, layer_idx, n_experts, tp):
    mp, d = h.shape
    rows = _pick(tp, (528, 384, 256, 128, 64, 32, 16))
    body = functools.partial(_router_body, n_experts=n_experts)
    out_spec = pl.BlockSpec((rows, ROUTER_LANES), lambda i: (i, 0))
    return pl.pallas_call(
        body,
        grid=(mp // rows,),
        in_specs=[pl.BlockSpec((rows, d), lambda i: (i, 0)),
                  pl.BlockSpec((None, d, ROUTER_LANES), lambda i: (layer_idx, 0, 0))],
        out_specs=[out_spec, out_spec],
        out_shape=[jax.ShapeDtypeStruct((mp, ROUTER_LANES), jnp.int32),
                   jax.ShapeDtypeStruct((mp, ROUTER_LANES), F32)],
        compiler_params=_params(1, 40),
        name="moe_router",
    )(h, w_router_padded)


MOE_TM = 512
GATHER_ROWS = 256
SUBLANES = 8


def _gather_body(src_ref, tv_ref, x_hbm, o_ref, buf, sem, *, rows, steps_per_tile):
    step = pl.program_id(0)

    def start_rows(st, slot):
        @pl.when(tv_ref[st // steps_per_tile] == 1)
        def _():
            def issue(g, c):
                for u in range(SUBLANES):
                    t = src_ref[st * rows + g * SUBLANES + u]
                    pltpu.make_async_copy(x_hbm.at[pl.ds(t, 1), :],
                                          buf.at[slot, g, pl.ds(u, 1), :],
                                          sem.at[slot]).start(priority=u % 2)
                return c
            lax.fori_loop(0, rows // SUBLANES, issue, 0)

    @pl.when(step == 0)
    def _():
        start_rows(step, 0)

    @pl.when(step + 1 < pl.num_programs(0))
    def _():
        start_rows(step + 1, lax.rem(step + 1, 2))

    slot = lax.rem(step, 2)
    valid = tv_ref[step // steps_per_tile] == 1

    @pl.when(valid)
    def _():
        pltpu.make_async_copy(buf.at[slot], buf.at[slot], sem.at[slot]).wait()
        o_ref[...] = buf[slot].reshape(rows, buf.shape[-1]).astype(o_ref.dtype)

    @pl.when(jnp.logical_not(valid))
    def _():
        o_ref[...] = jnp.zeros_like(o_ref)


def _gather_rows(h, src, tile_valid, *, n_rows):
    mp, d = h.shape
    rows = GATHER_ROWS
    assert n_rows % rows == 0 and MOE_TM % rows == 0
    body = functools.partial(_gather_body, rows=rows, steps_per_tile=MOE_TM // rows)
    return pl.pallas_call(
        body,
        grid_spec=pltpu.PrefetchScalarGridSpec(
            num_scalar_prefetch=2,
            grid=(n_rows // rows,),
            in_specs=[pl.BlockSpec(memory_space=pl.ANY)],
            out_specs=pl.BlockSpec((rows, d), lambda i, src, tv: (i, 0)),
            scratch_shapes=[pltpu.VMEM((2, rows // SUBLANES, SUBLANES, d), F32),
                            pltpu.SemaphoreType.DMA((2,))],
        ),
        out_shape=jax.ShapeDtypeStruct((n_rows, d), BF16),
        compiler_params=_params(1, 32),
        name="moe_gather",
    )(src, tile_valid, h)


def _gmm_body(te_ref, tv_ref, nx_ref, *refs, n_w, swiglu, wbase, bn):
    it = iter(refs)
    x_ref = next(it)
    w_hbm = [next(it) for _ in range(n_w)]
    o_ref = next(it)
    wb_refs = [next(it) for _ in range(n_w)]
    wf_refs = [next(it) for _ in range(n_w)]
    sem = next(it)
    j = pl.program_id(0)
    i = pl.program_id(1)
    fresh = jnp.logical_or(i == 0, te_ref[i] != te_ref[jnp.maximum(i - 1, 0)])

    def weight_copy(w, expert, col_tile):
        cols = pl.ds(pl.multiple_of(col_tile * bn, bn), bn)
        return pltpu.make_async_copy(w_hbm[w].at[wbase + expert, :, cols], wf_refs[w], sem.at[w])

    @pl.when(fresh)
    def _():
        @pl.when(jnp.logical_and(i == 0, j == 0))
        def _():
            for w in range(n_w):
                weight_copy(w, te_ref[0], 0).start()
        for w in range(n_w):
            weight_copy(w, te_ref[i], j).wait()
            wb_refs[w][...] = wf_refs[w][...].astype(BF16)
        nxt = nx_ref[i]

        @pl.when(nxt >= 0)
        def _():
            for w in range(n_w):
                weight_copy(w, nxt, j).start()

        @pl.when(jnp.logical_and(nxt < 0, j + 1 < pl.num_programs(0)))
        def _():
            for w in range(n_w):
                weight_copy(w, te_ref[0], j + 1).start()

    @pl.when(tv_ref[i] == 1)
    def _():
        x = x_ref[...]
        y = jnp.dot(x, wb_refs[0][...], preferred_element_type=F32)
        if swiglu:
            u = jnp.dot(x, wb_refs[1][...], preferred_element_type=F32)
            y = (y * jax.nn.sigmoid(y)) * u
        o_ref[...] = y.astype(o_ref.dtype)

    @pl.when(tv_ref[i] == 0)
    def _():
        o_ref[...] = jnp.zeros_like(o_ref)


def _grouped_matmul(x, ws, tile_expert, tile_valid, next_expert, wbase, *, bn, out_dtype, swiglu,
                    name):
    m, k = x.shape
    n = ws[0].shape[2]
    tm = MOE_TM
    assert m % tm == 0 and n % bn == 0
    in_specs = [pl.BlockSpec((tm, k), lambda j, i, te, tv, nx: (i, 0))]
    in_specs += [pl.BlockSpec(memory_space=pl.ANY) for _ in ws]
    out_bytes = jnp.dtype(out_dtype).itemsize
    vmem = (2 * tm * k * 2 + len(ws) * k * bn * (4 + 2) + 2 * tm * bn * out_bytes
            + 4 * tm * bn * 4)
    body = functools.partial(_gmm_body, n_w=len(ws), swiglu=swiglu, wbase=wbase, bn=bn)
    return pl.pallas_call(
        body,
        grid_spec=pltpu.PrefetchScalarGridSpec(
            num_scalar_prefetch=3,
            grid=(n // bn, m // tm),
            in_specs=in_specs,
            out_specs=pl.BlockSpec((tm, bn), lambda j, i, te, tv, nx: (i, j)),
            scratch_shapes=([pltpu.VMEM((k, bn), BF16) for _ in ws]
                            + [pltpu.VMEM((k, bn), F32) for _ in ws]
                            + [pltpu.SemaphoreType.DMA((len(ws),))]),
        ),
        out_shape=jax.ShapeDtypeStruct((m, n), out_dtype),
        compiler_params=_params(2, vmem // (1024 * 1024) + 4),
        name=name,
    )(tile_expert, tile_valid, next_expert, x, *ws)


def _combine_ln_body(pos_ref, h_ref, w_ref, g_ref, b_ref, y_hbm, o_ref, ob_ref, buf, sem,
                     *, alpha, rows, tp, pad):
    step = pl.program_id(0)

    def start_rows(st, slot):
        def issue(g, c):
            for u in range(SUBLANES):
                for s in range(TOP_K):
                    p = pos_ref[(st * rows + g * SUBLANES + u) * TOP_K + s]
                    pltpu.make_async_copy(y_hbm.at[pl.ds(p, 1), :],
                                          buf.at[slot, s, g, pl.ds(u, 1), :],
                                          sem.at[slot]).start(priority=s % 2)
            return c
        lax.fori_loop(0, rows // SUBLANES, issue, 0)

    @pl.when(step == 0)
    def _():
        start_rows(step, 0)

    @pl.when(step + 1 < pl.num_programs(0))
    def _():
        start_rows(step + 1, lax.rem(step + 1, 2))

    slot = lax.rem(step, 2)
    for s in range(TOP_K):
        pltpu.make_async_copy(buf.at[slot, s], buf.at[slot, s], sem.at[slot]).wait()

    w = w_ref[...]
    lane = lax.broadcasted_iota(jnp.int32, w.shape, 1)
    f = None
    for s in range(TOP_K):
        ws = jnp.sum(jnp.where(lane == s, w, 0.0), axis=-1, keepdims=True)
        term = ws * buf[slot, s].reshape(rows, buf.shape[-1])
        f = term if f is None else f + term
    y = _layernorm_rows(alpha * h_ref[...] + f, g_ref[...], b_ref[...], rows=rows, tp=tp, pad=pad)
    o_ref[...] = y
    ob_ref[...] = y.astype(BF16)


def _combine_ln(h, ys, pos, wts, g, b, layer, *, alpha, tp, pad):
    mp, d = h.shape
    rows = _pick(tp, (384, 256, 128, 64, 32, 16))
    body = functools.partial(_combine_ln_body, alpha=alpha, rows=rows, tp=tp, pad=pad)
    row_spec = pl.BlockSpec((rows, d), lambda i, p: (i, 0))
    par_spec = pl.BlockSpec((None, 1, d), lambda i, p: (layer, 0, 0))
    return pl.pallas_call(
        body,
        grid_spec=pltpu.PrefetchScalarGridSpec(
            num_scalar_prefetch=1,
            grid=(mp // rows,),
            in_specs=[row_spec, pl.BlockSpec((rows, ROUTER_LANES), lambda i, p: (i, 0)),
                      par_spec, par_spec, pl.BlockSpec(memory_space=pl.ANY)],
            out_specs=[row_spec, row_spec],
            scratch_shapes=[pltpu.VMEM((2, TOP_K, rows // SUBLANES, SUBLANES, d), F32),
                            pltpu.SemaphoreType.DMA((2,))],
        ),
        out_shape=[jax.ShapeDtypeStruct((mp, d), F32), jax.ShapeDtypeStruct((mp, d), BF16)],
        compiler_params=_params(1, 40),
        name="moe_combine_layernorm",
    )(pos, h, wts, g, b, ys)


def _route(idx, *, n_experts, tp, pad):
    mp = idx.shape[0]
    tm = MOE_TM
    n_assign = mp * TOP_K
    n_tiles = -(-n_assign // tm) + n_experts
    n_rows = n_tiles * tm
    token = jnp.arange(mp, dtype=jnp.int32)
    real = (token % tp) >= pad
    e = jnp.where(real[:, None], idx[:, :TOP_K], n_experts).reshape(n_assign)
    onehot = (e[:, None] == jnp.arange(n_experts, dtype=jnp.int32)[None, :]).astype(jnp.int32)
    csum = jnp.cumsum(onehot, axis=0)
    counts = csum[-1]
    rank = jnp.sum(onehot * csum, axis=1) - 1
    gsz = ((counts + tm - 1) // tm) * tm
    gend = jnp.cumsum(gsz)
    gstart = gend - gsz
    routed = e < n_experts
    dst = jnp.where(routed, gstart[jnp.minimum(e, n_experts - 1)] + rank, n_rows)
    assign_token = jnp.arange(n_assign, dtype=jnp.int32) // TOP_K
    src = jnp.zeros((n_rows,), jnp.int32).at[dst].set(assign_token, mode="drop")
    pos = jnp.where(routed, dst, 0).astype(jnp.int32)
    tile_start = jnp.arange(n_tiles, dtype=jnp.int32) * tm
    tile_valid = tile_start < gend[-1]
    tile_expert = jnp.sum((tile_start[:, None] >= gend[None, :]).astype(jnp.int32), axis=1)
    last_expert = jnp.max(jnp.where(tile_valid, tile_expert, 0))
    tile_expert = jnp.where(tile_valid, tile_expert, last_expert)
    later = jnp.arange(n_tiles)[None, :] > jnp.arange(n_tiles)[:, None]
    differs = jnp.logical_and(later, tile_expert[None, :] != tile_expert[:, None])
    first = jnp.argmax(differs, axis=1)
    next_expert = jnp.where(jnp.any(differs, axis=1), tile_expert[first], -1).astype(jnp.int32)
    return src, pos, tile_expert, tile_valid.astype(jnp.int32), next_expert, n_rows


def kernel(x, meta_tokens, w_in, ret_gn_g, hg_norm_g, hg_lb_logits, w_ret_out, w_hg_out, w_o,
           ln1_g, ln1_b, ln2_g, ln2_b, ffn_w_gate, ffn_w_up, ffn_w_down,
           moe_router, moe_w_gate, moe_w_up, moe_w_down):
    batch, seq, d = x.shape
    n_meta = meta_tokens.shape[0]
    depth, _, p_in = w_in.shape
    ret_v = w_ret_out.shape[1]
    hg_v = w_hg_out.shape[1]
    hg_k = hg_lb_logits.shape[1]
    ret_qk = (p_in - 2 * ret_v - 2 * hg_k - 2 * hg_v - 2 * d) // 2
    ret_heads = ret_qk // RET_DK
    hg_heads = hg_k // HG_DK
    n_experts = moe_router.shape[2]
    assert ret_v == ret_heads * RET_DV and hg_v == hg_heads * HG_DV and hg_k == hg_v
    assert n_experts <= ROUTER_LANES

    pad = RET_CHUNK - n_meta
    tp = pad + n_meta + seq
    assert tp % RET_CHUNK == 0 and tp % HG_CHUNK == 0
    mp = batch * tp
    alpha = (2 * depth) ** 0.25

    off = {}
    acc = 0
    for nm, wd in (("rq", ret_qk), ("rk", ret_qk), ("rv", ret_v), ("rg", ret_v), ("hq", hg_k),
                   ("hf", hg_k), ("hi", hg_v), ("hg", hg_v), ("ga", d), ("gb", d)):
        off[nm] = acc
        acc += wd

    meta = jnp.broadcast_to(meta_tokens[None].astype(F32), (batch, n_meta, d))
    h = jnp.concatenate([jnp.zeros((batch, pad, d), F32), meta, x.astype(F32)], axis=1)
    h = h.reshape(mp, d)
    hb = h.astype(BF16)

    pos = jnp.arange(tp, dtype=F32) - float(pad)
    inv = 1.0 / (ROPE_BASE ** jnp.linspace(0.0, 1.0, RET_DK // 2, dtype=F32))
    ang = pos[:, None] * inv[None, :]
    cos, sin = jnp.cos(ang), jnp.sin(ang)
    log_gamma = jnp.log1p(-jnp.exp2(-5.0 - jnp.arange(ret_heads, dtype=F32)))

    ret_gn3 = ret_gn_g.astype(F32).reshape(depth, 1, ret_v)
    hg_gn3 = hg_norm_g.astype(F32).reshape(depth, 1, hg_v)
    ln1_g3, ln1_b3 = ln1_g.reshape(depth, 1, d), ln1_b.reshape(depth, 1, d)
    ln2_g3, ln2_b3 = ln2_g.reshape(depth, 1, d), ln2_b.reshape(depth, 1, d)
    n_moe = moe_router.shape[0]
    d_ffe = moe_w_gate.shape[3]
    router_w = jnp.pad(moe_router.astype(F32), ((0, 0), (0, 0), (0, ROUTER_LANES - n_experts)))
    moe_wg = moe_w_gate.reshape(n_moe * n_experts, d, d_ffe)
    moe_wu = moe_w_up.reshape(n_moe * n_experts, d, d_ffe)
    moe_wd = moe_w_down.reshape(n_moe * n_experts, d_ffe, d)

    tile = lambda n, cands: _pick(n, cands)

    for l in range(depth):
        proj = _matmul_rows_resident(hb, [w_in], l, rows=tp, skip=pad,
                                     bn=tile(p_in, (512, 256, 128)), out_dtype=BF16,
                                     name="in_proj")
        yr = _retention(proj, cos, sin, log_gamma, ret_gn3, l, batch=batch, tp=tp,
                        heads=ret_heads, off_q=off["rq"], off_k=off["rk"], off_v=off["rv"],
                        off_g=off["rg"])
        yh, t_r = _hgrn2_with_ret_out(
            proj, hg_lb_logits.astype(F32), hg_gn3, yr, w_ret_out, l, batch=batch, tp=tp,
            heads=hg_heads, off_q=off["hq"], off_f=off["hf"], off_i=off["hi"], off_g=off["hg"],
            off_ga=off["ga"])
        merged = _matmul_gate_add(yh, w_hg_out, l, (proj, off["gb"]), t_r, tp=tp,
                                  name="hg_out_proj")
        h, hb = _matmul_add_ln(merged, w_o, l, h, ln1_g3, ln1_b3, l, alpha=alpha, tp=tp, pad=pad,
                               staged=False, name="mix_out_proj_layernorm")
        if l % 2 == 0:
            d_ff = ffn_w_gate.shape[2]
            act = _matmul_rows_resident(hb, [ffn_w_gate, ffn_w_up], l // 2, rows=tp, skip=pad,
                                        bn=tile(d_ff, (256, 128)), out_dtype=BF16, name="ffn_up")
            h, hb = _matmul_add_ln(act, ffn_w_down, l // 2, h, ln2_g3, ln2_b3, l, alpha=alpha,
                                   tp=tp, pad=pad, staged=True, name="ffn_down_layernorm")
        else:
            idx, wts = _router(h, router_w, l // 2, n_experts, tp)
            src, pos, tile_expert, tile_valid, next_expert, n_rows = _route(
                idx, n_experts=n_experts, tp=tp, pad=pad)
            xs = _gather_rows(h, src, tile_valid, n_rows=n_rows)
            wbase = (l // 2) * n_experts
            act = _grouped_matmul(xs, [moe_wg, moe_wu], tile_expert, tile_valid, next_expert, wbase,
                                  bn=tile(d_ffe, (1408, 256, 128)), out_dtype=BF16, swiglu=True,
                                  name="moe_up")
            ys = _grouped_matmul(act, [moe_wd], tile_expert, tile_valid, next_expert, wbase,
                                 bn=tile(d, (1024, 512, 256, 128)), out_dtype=F32, swiglu=False,
                                 name="moe_down")
            h, hb = _combine_ln(h, ys, pos, wts, ln2_g3, ln2_b3, l, alpha=alpha, tp=tp, pad=pad)

    out = h.reshape(batch, tp, d)[:, pad + n_meta:, :]
    return out.astype(x.dtype)
```
